```python
import math
import jax
import jax.numpy as jnp
from jax import lax

D_MODEL = 1024
BATCH = 8
SEQ = 4096
DEPTH = 4

GRID_W = 64
CTX_LEN = 256
N_EVEN = (DEPTH + 1) // 2
N_ODD = DEPTH // 2

MLSTM_HEADS = 4
MLSTM_DQK = 64
MLSTM_DV = 128
MLSTM_CHUNK = 64
HGRN_HEADS = 4
HGRN_DK = 128
HGRN_DV = 128
HGRN_CHUNK = 32
HYENA_WIDTH = 512
HYENA_GROUPS = 8
HYENA_ORDER = 2
HYENA_EMB = 33
HYENA_FILTER_HIDDEN = 64
HYENA_SHORT = 3
HYENA_TARGET = 1e-2
HYENA_FAST_PCT = 0.3
HYENA_SLOW_PCT = 1.5
RET_HEADS = 4
RET_DQK = 128
RET_DV = 128
RET_CHUNK = 64
ROPE_BASE = 10000.0
N_EXPERTS = 64
TOP_K = 8
EXPERT_HIDDEN = 256
SHARED_HIDDEN = 256
ROUTED_SCALE = 2.5
MOE_BLOCK = 128
DEEPNORM_ALPHA = (2 * DEPTH) ** 0.25
DEEPNORM_BETA = (8 * DEPTH) ** -0.25
EPS = 1e-5
NEG_BIG = -1e30

MLSTM_SPLITS = (MLSTM_HEADS * MLSTM_DQK, MLSTM_HEADS * MLSTM_DQK, MLSTM_HEADS * MLSTM_DV, MLSTM_HEADS * MLSTM_DV, 4 * MLSTM_HEADS)
HGRN_SPLITS = (HGRN_HEADS * HGRN_DK, HGRN_HEADS * HGRN_DV, HGRN_HEADS * HGRN_DK, HGRN_HEADS * HGRN_DK, HGRN_HEADS * HGRN_DV)
EVEN_SPLITS = MLSTM_SPLITS + HGRN_SPLITS
EVEN_PROJ = sum(EVEN_SPLITS)
EVEN_MIX = MLSTM_HEADS * MLSTM_DV + HGRN_HEADS * HGRN_DV
RET_SPLITS = (RET_HEADS * RET_DQK, RET_HEADS * RET_DQK, RET_HEADS * RET_DV, RET_HEADS * RET_DV)
ODD_SPLITS = ((HYENA_ORDER + 1) * HYENA_WIDTH,) + RET_SPLITS
ODD_PROJ = sum(ODD_SPLITS)
ODD_MIX = HYENA_WIDTH + RET_HEADS * RET_DV

F32 = jnp.float32

kernel_name = 'hybrid_mlstm_hgrn2_hyena_retention_moe'


def split_cols(a, sizes):
    out, start = [], 0
    for s in sizes:
        out.append(a[..., start:start + s])
        start += s
    return out


def to_heads(a, n_heads):
    b, l, w = a.shape
    return a.reshape(b, l, n_heads, w // n_heads).transpose(0, 2, 1, 3)


def chunk(a, cs):
    b, h, l = a.shape[:3]
    return a.reshape((b, h, l // cs, cs) + a.shape[3:])


def layer_norm(x, w, b):
    xf = x.astype(F32)
    mu = jnp.mean(xf, -1, keepdims=True)
    var = jnp.mean(jnp.square(xf - mu), -1, keepdims=True)
    return ((xf - mu) * lax.rsqrt(var + EPS) * w + b).astype(x.dtype)


def head_norm(h, w, rms):
    h = h.transpose(0, 2, 1, 3)
    if not rms:
        h = h - jnp.mean(h, -1, keepdims=True)
    h = h * lax.rsqrt(jnp.mean(jnp.square(h), -1, keepdims=True) + EPS)
    b, l, nh, dh = h.shape
    return h.reshape(b, l, nh * dh) * w


def group_rms(z, w):
    b, l, wd = z.shape
    zg = z.reshape(b, l, HYENA_GROUPS, wd // HYENA_GROUPS)
    zg = zg * lax.rsqrt(jnp.mean(jnp.square(zg), -1, keepdims=True) + EPS)
    return zg.reshape(b, l, wd) * w


def run_direction(scan_fn, ctx_seq, lat_seq, const, state0, reverse):
    if reverse:
        ctx_seq = tuple(jnp.flip(a, axis=2) for a in ctx_seq)
        lat_seq = tuple(jnp.flip(a, axis=2) for a in lat_seq)
    y_ctx, state = scan_fn(ctx_seq, const, state0)
    y_lat, _ = scan_fn(lat_seq, const, state)
    if reverse:
        y_ctx = jnp.flip(y_ctx, axis=2)
        y_lat = jnp.flip(y_lat, axis=2)
    return y_lat, y_ctx


def mlstm_scan(seq, const, state):
    q, k, v, ig, lf = seq
    cs = MLSTM_CHUNK
    qc, kc, vc, igc, lfc = (chunk(a, cs) for a in seq)
    b = jnp.cumsum(lfc, axis=-1)
    g = b[..., -1]
    a = g[..., None] - b + igc
    m_loc = jnp.max(a, axis=-1)
    w = jnp.exp(a - m_loc[..., None])
    c_loc = jnp.einsum('bhcs,bhcsv,bhcsk->bhcvk', w, vc, kc)
    n_loc = jnp.einsum('bhcs,bhcsk->bhck', w, kc)

    def step(carry, inp):
        c_st, n_st, m_st = carry
        g_j, ml_j, cl_j, nl_j = inp
        m_new = jnp.maximum(g_j + m_st, ml_j)
        sp = jnp.exp(g_j + m_st - m_new)
        sl = jnp.exp(ml_j - m_new)
        new = (sp[..., None, None] * c_st + sl[..., None, None] * cl_j, sp[..., None] * n_st + sl[..., None] * nl_j, m_new)
        return new, carry

    final, (c_in, n_in, m_in) = lax.scan(step, state, tuple(jnp.moveaxis(t, 2, 0) for t in (g, m_loc, c_loc, n_loc)))
    c_in, n_in, m_in = jnp.moveaxis(c_in, 0, 2), jnp.moveaxis(n_in, 0, 2), jnp.moveaxis(m_in, 0, 2)
    tri = jnp.tril(jnp.ones((cs, cs), bool))
    log_d = jnp.where(tri, b[..., :, None] - b[..., None, :] + igc[..., None, :], -jnp.inf)
    inter = b + m_in[..., None]
    m_out = jnp.maximum(inter, jnp.max(log_d, -1))
    s = jnp.einsum('bhctk,bhcsk->bhcts', qc, kc) * jnp.exp(log_d - m_out[..., None])
    e_int = jnp.exp(inter - m_out)
    num = jnp.einsum('bhcts,bhcsv->bhctv', s, vc) + e_int[..., None] * jnp.einsum('bhctk,bhcvk->bhctv', qc, c_in)
    den = jnp.sum(s, -1) + e_int * jnp.einsum('bhctk,bhck->bhct', qc, n_in)
    h = num / jnp.maximum(jnp.abs(den), jnp.exp(-m_out))[..., None]
    return h.reshape(v.shape), final


def hgrn2_scan(seq, const, state):
    q, k, v, lf = seq
    cs = HGRN_CHUNK
    qc, kc, vc, lfc = (chunk(a, cs) for a in seq)
    a_cum = jnp.cumsum(lfc, axis=3)
    a_end = a_cum[:, :, :, -1]
    s_loc = jnp.einsum('bhcsk,bhcsv->bhckv', kc * jnp.exp(a_end[:, :, :, None] - a_cum), vc)

    def step(s_st, inp):
        a_j, sl_j = inp
        return jnp.exp(a_j)[..., None] * s_st + sl_j, s_st

    final, s_in = lax.scan(step, state, (jnp.moveaxis(a_end, 2, 0), jnp.moveaxis(s_loc, 2, 0)))
    s_in = jnp.moveaxis(s_in, 0, 2)
    inter = jnp.einsum('bhctk,bhckv->bhctv', qc * jnp.exp(a_cum), s_in)
    a_mid = a_cum[:, :, :, cs // 2 - 1:cs // 2]
    tri = jnp.tril(jnp.ones((cs, cs), bool))
    sc = jnp.einsum('bhctk,bhcsk->bhcts', qc * jnp.exp(a_cum - a_mid), kc * jnp.exp(a_mid - a_cum))
    sc = jnp.where(tri, sc, 0.0)
    out = inter + jnp.einsum('bhcts,bhcsv->bhctv', sc, vc)
    return out.reshape(v.shape), final


def retention_scan(seq, log_gamma, state):
    q, k, v = seq
    cs = RET_CHUNK
    qc, kc, vc = (chunk(a, cs) for a in seq)
    pos = jnp.arange(cs, dtype=F32)
    tri = jnp.tril(jnp.ones((cs, cs), bool))
    delta = pos[:, None] - pos[None, :]
    decay = jnp.exp(jnp.where(tri, delta[None] * log_gamma[:, None, None], -jnp.inf))
    sc = jnp.einsum('bhctk,bhcsk->bhcts', qc, kc) * decay[:, None]
    intra = jnp.einsum('bhcts,bhcsv->bhctv', sc, vc)
    q_dec = jnp.exp((pos + 1.0)[None, :] * log_gamma[:, None])
    k_dec = jnp.exp((cs - 1.0 - pos)[None, :] * log_gamma[:, None])
    s_loc = jnp.einsum('bhcsk,hs,bhcsv->bhckv', kc, k_dec, vc)
    chunk_decay = jnp.exp(cs * log_gamma)[None, :, None, None]

    def step(s_st, sl_j):
        return chunk_decay * s_st + sl_j, s_st

    final, s_in = lax.scan(step, state, jnp.moveaxis(s_loc, 2, 0))
    s_in = jnp.moveaxis(s_in, 0, 2)
    inter = jnp.einsum('bhctk,ht,bhckv->bhctv', qc, q_dec, s_in)
    return (inter + intra).reshape(v.shape), final


def rope_1d(x, pos):
    d = x.shape[-1]
    inv = ROPE_BASE ** (-jnp.arange(0, d, 2, dtype=F32) / d)
    ang = pos[:, None] * inv[None, :]
    cos = jnp.cos(ang)[None, :, None, :]
    sin = jnp.sin(ang)[None, :, None, :]
    x1, x2 = x[..., :d // 2], x[..., d // 2:]
    return jnp.concatenate([x1 * cos - x2 * sin, x1 * sin + x2 * cos], -1)


def axial_rope(x, row, col):
    half = x.shape[-1] // 2
    return jnp.concatenate([rope_1d(x[..., :half], row), rope_1d(x[..., half:], col)], -1)


def mlstm_group(cols_lat, cols_ctx, gate_b, norm_w):
    def prep(cols):
        q, k, v, o, gates = cols
        b, l, _ = q.shape
        q = to_heads(q, MLSTM_HEADS) * MLSTM_DQK ** -0.5
        k = to_heads(k, MLSTM_HEADS)
        v = to_heads(v, MLSTM_HEADS)
        gates = (gates.reshape(b, l, 4, MLSTM_HEADS) + gate_b).transpose(2, 0, 3, 1)
        fwd = (q, k, v, gates[0], jax.nn.log_sigmoid(gates[1]))
        bwd = (q, k, v, gates[2], jax.nn.log_sigmoid(gates[3]))
        return fwd, bwd, o

    fwd_l, bwd_l, o_l = prep(cols_lat)
    fwd_c, bwd_c, o_c = prep(cols_ctx)
    batch = o_l.shape[0]
    state0 = (jnp.zeros((batch, MLSTM_HEADS, MLSTM_DV, MLSTM_DQK), F32),
              jnp.zeros((batch, MLSTM_HEADS, MLSTM_DQK), F32),
              jnp.full((batch, MLSTM_HEADS), NEG_BIG, F32))
    hf_lat, hf_ctx = run_direction(mlstm_scan, fwd_c, fwd_l, None, state0, False)
    hb_lat, hb_ctx = run_direction(mlstm_scan, bwd_c, bwd_l, None, state0, True)
    y_lat = head_norm(hf_lat + hb_lat, norm_w, False) * jax.nn.sigmoid(o_l)
    y_ctx = head_norm(hf_ctx + hb_ctx, norm_w, False) * jax.nn.sigmoid(o_c)
    return y_lat, y_ctx


def hgrn_group(cols_lat, cols_ctx, lb, norm_w):
    def prep(cols):
        q, i, f_fwd, f_bwd, g = cols
        q = to_heads(jax.nn.silu(q), HGRN_HEADS)
        v = to_heads(i, HGRN_HEADS)

        def gate(fp):
            f = lb + (1.0 - lb) * jax.nn.sigmoid(fp)
            return to_heads(1.0 - f, HGRN_HEADS), to_heads(jnp.log(f), HGRN_HEADS)

        k_f, lf_f = gate(f_fwd)
        k_b, lf_b = gate(f_bwd)
        return (q, k_f, v, lf_f), (q, k_b, v, lf_b), g

    fwd_l, bwd_l, g_l = prep(cols_lat)
    fwd_c, bwd_c, g_c = prep(cols_ctx)
    state0 = jnp.zeros((g_l.shape[0], HGRN_HEADS, HGRN_DK, HGRN_DV), F32)
    yf_lat, yf_ctx = run_direction(hgrn2_scan, fwd_c, fwd_l, None, state0, False)
    yb_lat, yb_ctx = run_direction(hgrn2_scan, bwd_c, bwd_l, None, state0, True)
    y_lat = head_norm(yf_lat + yb_lat, norm_w, True) * jax.nn.silu(g_l)
    y_ctx = head_norm(yf_ctx + yb_ctx, norm_w, True) * jax.nn.silu(g_c)
    return y_lat, y_ctx


def hyena_filters(length, fw1, fb1, fw2, fb2, fw3, fb3, freq, wout):
    pos = jnp.arange(length, dtype=F32)
    t = (pos / max(length - 1, 1))[:, None]
    bands = (HYENA_EMB - 1) // 2
    w = 2.0 * math.pi * pos[:, None] / length * jnp.linspace(1e-4, bands - 1, bands)[None, :]
    z = jnp.concatenate([t, jnp.cos(w), jnp.sin(w)], -1)
    hid = jnp.sin(freq * (z @ fw1 + fb1))
    hid = jnp.sin(freq * (hid @ fw2 + fb2))
    hid = jnp.sin(freq * (hid @ fw3 + fb3))
    h = (hid @ wout).reshape(length, HYENA_ORDER, 2, HYENA_WIDTH)
    deltas = jnp.abs(jnp.linspace(math.log(HYENA_TARGET) / HYENA_SLOW_PCT, math.log(HYENA_TARGET) / HYENA_FAST_PCT, HYENA_WIDTH))
    decay = jnp.exp(-t * deltas[None, :])
    return (h * decay[:, None, None, :]).transpose(1, 2, 0, 3)


def bidir_fftconv(u, h_fwd, h_bwd, d_skip):
    length, width = h_fwd.shape
    kern = jnp.concatenate([h_fwd, jnp.zeros((1, width), F32), jnp.flip(h_bwd[1:], axis=0)], axis=0)
    spec = jnp.fft.rfft(u, n=2 * length, axis=1) * jnp.fft.rfft(kern, axis=0)[None]
    y = jnp.fft.irfft(spec, n=2 * length, axis=1)[:, :length]
    return y + u * d_skip


def hyena_group(u_lat, u_ctx, conv_w, conv_b, fw1, fb1, fw2, fb2, fw3, fb3, freq, wout, skip, norm_w):
    def run(u):
        length = u.shape[1]
        up = jnp.pad(u, ((0, 0), (1, 1), (0, 0)))
        u = up[:, :-2] * conv_w[0] + up[:, 1:-1] * conv_w[1] + up[:, 2:] * conv_w[2] + conv_b
        streams = jnp.split(u, HYENA_ORDER + 1, axis=-1)
        gates, z = streams[:-1], streams[-1]
        h = hyena_filters(length, fw1, fb1, fw2, fb2, fw3, fb3, freq, wout)
        for n in range(HYENA_ORDER):
            z = gates[n] * bidir_fftconv(z, h[n, 0], h[n, 1], skip[n])
        return group_rms(z, norm_w)

    return run(u_lat), run(u_ctx)


def retention_group(cols_lat, cols_ctx, log_decay, norm_w, row, col):
    def prep(cols, rotary):
        q, k, v, g = cols
        b, l, _ = q.shape
        q = q.reshape(b, l, RET_HEADS, RET_DQK)
        k = k.reshape(b, l, RET_HEADS, RET_DQK)
        if rotary:
            q = axial_rope(q, row, col)
            k = axial_rope(k, row, col)
        q = q.transpose(0, 2, 1, 3)
        k = k.transpose(0, 2, 1, 3) * RET_DQK ** -0.5
        return (q, k, to_heads(v, RET_HEADS)), g

    lat, g_l = prep(cols_lat, True)
    ctx, g_c = prep(cols_ctx, False)
    state0 = jnp.zeros((g_l.shape[0], RET_HEADS, RET_DQK, RET_DV), F32)
    yf_lat, yf_ctx = run_direction(retention_scan, ctx, lat, log_decay[0], state0, False)
    yb_lat, yb_ctx = run_direction(retention_scan, ctx, lat, log_decay[1], state0, True)
    y_lat = head_norm(yf_lat + yb_lat, norm_w, False) * jax.nn.silu(g_l)
    y_ctx = head_norm(yf_ctx + yb_ctx, norm_w, False) * jax.nn.silu(g_c)
    return y_lat, y_ctx


def even_mixer(u_lat, u_ctx, w_in, gate_b, mlstm_norm_w, lb, hgrn_norm_w, w_out):
    w_in = w_in.astype(F32)
    w_out = w_out.astype(F32)
    p_lat = split_cols(u_lat.astype(F32) @ w_in, EVEN_SPLITS)
    p_ctx = split_cols(u_ctx.astype(F32) @ w_in, EVEN_SPLITS)
    n_a = len(MLSTM_SPLITS)
    a_lat, a_ctx = mlstm_group(p_lat[:n_a], p_ctx[:n_a], gate_b.astype(F32), mlstm_norm_w.astype(F32))
    b_lat, b_ctx = hgrn_group(p_lat[n_a:], p_ctx[n_a:], lb, hgrn_norm_w.astype(F32))
    return jnp.concatenate([a_lat, b_lat], -1) @ w_out, jnp.concatenate([a_ctx, b_ctx], -1) @ w_out


def odd_mixer(u_lat, u_ctx, w_in, hyena_p, log_decay, ret_norm_w, w_out, row, col):
    w_in = w_in.astype(F32)
    w_out = w_out.astype(F32)
    p_lat = split_cols(u_lat.astype(F32) @ w_in, ODD_SPLITS)
    p_ctx = split_cols(u_ctx.astype(F32) @ w_in, ODD_SPLITS)
    hy_lat, hy_ctx = hyena_group(p_lat[0], p_ctx[0], *[p.astype(F32) for p in hyena_p])
    r_lat, r_ctx = retention_group(p_lat[1:], p_ctx[1:], log_decay.astype(F32), ret_norm_w.astype(F32), row, col)
    return jnp.concatenate([hy_lat, r_lat], -1) @ w_out, jnp.concatenate([hy_ctx, r_ctx], -1) @ w_out


def moe_ffn(t, router_w, router_bias, w1, w3, w2, sw1, sw3, sw2):
    n_tok, d = t.shape
    tf = t.astype(F32)
    scores = jax.nn.sigmoid(tf @ router_w.astype(F32))
    _, idx = lax.top_k(scores + router_bias.astype(F32), TOP_K)
    gate = jnp.take_along_axis(scores, idx, axis=1)
    gate = ROUTED_SCALE * gate / jnp.sum(gate, -1, keepdims=True)
    n_assign = n_tok * TOP_K
    e_flat = idx.reshape(-1)
    g_flat = gate.reshape(-1)
    tok_flat = jnp.arange(n_assign, dtype=jnp.int32) // TOP_K
    order = jnp.argsort(e_flat)
    e_s, tok_s, g_s = e_flat[order], tok_flat[order], g_flat[order]
    counts = jnp.bincount(e_flat, length=N_EXPERTS)
    padded = (counts + MOE_BLOCK - 1) // MOE_BLOCK * MOE_BLOCK
    start = jnp.cumsum(counts) - counts
    pend = jnp.cumsum(padded)
    pstart = pend - padded
    dest = pstart[e_s] + jnp.arange(n_assign, dtype=jnp.int32) - start[e_s]
    n_blocks = -(-(n_assign + N_EXPERTS * (MOE_BLOCK - 1)) // MOE_BLOCK)
    slot_tok = jnp.full((n_blocks * MOE_BLOCK,), n_tok, jnp.int32).at[dest].set(tok_s)
    slot_gate = jnp.zeros((n_blocks * MOE_BLOCK,), F32).at[dest].set(g_s)
    block_e = jnp.minimum(jnp.searchsorted(pend, jnp.arange(n_blocks, dtype=jnp.int32) * MOE_BLOCK, side='right'), N_EXPERTS - 1)
    t_ext = jnp.concatenate([tf, jnp.zeros((1, d), F32)], 0)

    def expert_block(acc, blk):
        tok, g, e = blk
        xb = t_ext[tok]
        hb = jax.nn.silu(xb @ w1[e]) * (xb @ w3[e])
        return acc.at[tok].add((hb @ w2[e]) * g[:, None]), None

    acc, _ = lax.scan(expert_block, jnp.zeros((n_tok + 1, d), F32),
                      (slot_tok.reshape(n_blocks, MOE_BLOCK), slot_gate.reshape(n_blocks, MOE_BLOCK), block_e))
    shared = (jax.nn.silu(tf @ sw1) * (tf @ sw3)) @ sw2
    return acc[:n_tok] + shared


def setup_inputs(seed: int = 0) -> dict:
    key = jax.random.key(seed)
    keys = iter(jax.random.split(key, 48))

    def nrm(shape, scale=1.0):
        return scale * jax.random.normal(next(keys), shape, F32)

    d = D_MODEL
    hfh = HYENA_FILTER_HIDDEN
    hyw = (HYENA_ORDER + 1) * HYENA_WIDTH
    f_bias = jnp.linspace(3.0, 6.0, MLSTM_HEADS)
    gate_base = jnp.stack([jnp.zeros(MLSTM_HEADS), f_bias, jnp.zeros(MLSTM_HEADS), f_bias])
    ret_base = jnp.log(1.0 - 2.0 ** (-5.0 - jnp.arange(RET_HEADS, dtype=F32)))
    return {
        'x': nrm((BATCH, SEQ, d)),
        'c': nrm((BATCH, d)),
        'ctx': nrm((BATCH, CTX_LEN, d)),
        'c_ctx': nrm((d,)),
        'ada_w': nrm((DEPTH, d, 6 * d), d ** -0.5),
        'ada_b': nrm((DEPTH, 6 * d), 0.02),
        'ln_w': 1.0 + nrm((DEPTH, 2, d), 0.02),
        'ln_b': nrm((DEPTH, 2, d), 0.02),
        'even_w_in': nrm((N_EVEN, d, EVEN_PROJ), d ** -0.5),
        'mlstm_gate_b': gate_base + nrm((N_EVEN, 4, MLSTM_HEADS), 0.1),
        'mlstm_norm_w': 1.0 + nrm((N_EVEN, MLSTM_HEADS * MLSTM_DV), 0.02),
        'hgrn_lb': nrm((N_EVEN, HGRN_HEADS * HGRN_DK), 1.0),
        'hgrn_norm_w': 1.0 + nrm((N_EVEN, HGRN_HEADS * HGRN_DV), 0.02),
        'even_w_out': nrm((N_EVEN, EVEN_MIX, d), DEEPNORM_BETA * EVEN_MIX ** -0.5),
        'odd_w_in': nrm((N_ODD, d, ODD_PROJ), d ** -0.5),
        'hy_conv_w': nrm((N_ODD, HYENA_SHORT, hyw), HYENA_SHORT ** -0.5),
        'hy_conv_b': nrm((N_ODD, hyw), 0.02),
        'hy_f_w1': nrm((N_ODD, HYENA_EMB, hfh), HYENA_EMB ** -0.5),
        'hy_f_b1': nrm((N_ODD, hfh), 0.1),
        'hy_f_w2': nrm((N_ODD, hfh, hfh), hfh ** -0.5),
        'hy_f_b2': nrm((N_ODD, hfh), 0.1),
        'hy_f_w3': nrm((N_ODD, hfh, hfh), hfh ** -0.5),
        'hy_f_b3': nrm((N_ODD, hfh), 0.1),
        'hy_f_freq': 1.0 + nrm((N_ODD, hfh), 0.02),
        'hy_f_wout': nrm((N_ODD, hfh, HYENA_ORDER * 2 * HYENA_WIDTH), hfh ** -0.5),
        'hy_skip': nrm((N_ODD, HYENA_ORDER, HYENA_WIDTH), 1.0),
        'hy_norm_w': 1.0 + nrm((N_ODD, HYENA_WIDTH), 0.02),
        'ret_log_decay': ret_base * (1.0 + nrm((N_ODD, 2, RET_HEADS), 0.05)),
        'ret_norm_w': 1.0 + nrm((N_ODD, RET_HEADS * RET_DV), 0.02),
        'odd_w_out': nrm((N_ODD, ODD_MIX, d), DEEPNORM_BETA * ODD_MIX ** -0.5),
        'router_w': nrm((DEPTH, d, N_EXPERTS), d ** -0.5),
        'router_bias': nrm((DEPTH, N_EXPERTS), 0.01),
        'exp_w1': nrm((DEPTH, N_EXPERTS, d, EXPERT_HIDDEN), d ** -0.5),
        'exp_w3': nrm((DEPTH, N_EXPERTS, d, EXPERT_HIDDEN), d ** -0.5),
        'exp_w2': nrm((DEPTH, N_EXPERTS, EXPERT_HIDDEN, d), DEEPNORM_BETA * EXPERT_HIDDEN ** -0.5),
        'sh_w1': nrm((DEPTH, d, SHARED_HIDDEN), d ** -0.5),
        'sh_w3': nrm((DEPTH, d, SHARED_HIDDEN), d ** -0.5),
        'sh_w2': nrm((DEPTH, SHARED_HIDDEN, d), DEEPNORM_BETA * SHARED_HIDDEN ** -0.5),
    }


def reference(x, c, ctx, c_ctx, ada_w, ada_b, ln_w, ln_b, even_w_in, mlstm_gate_b, mlstm_norm_w, hgrn_lb, hgrn_norm_w, even_w_out, odd_w_in, hy_conv_w, hy_conv_b, hy_f_w1, hy_f_b1, hy_f_w2, hy_f_b2, hy_f_w3, hy_f_b3, hy_f_freq, hy_f_wout, hy_skip, hy_norm_w, ret_log_decay, ret_norm_w, odd_w_out, router_w, router_bias, exp_w1, exp_w3, exp_w2, sh_w1, sh_w3, sh_w2):
    batch, n_lat, d = x.shape
    n_ctx = ctx.shape[1]
    rows = n_lat // GRID_W
    row = jnp.repeat(jnp.arange(rows, dtype=F32), GRID_W)
    col = jnp.tile(jnp.arange(GRID_W, dtype=F32), rows)
    sm = jax.nn.softmax(hgrn_lb.astype(F32), axis=0)
    lower_bounds = jnp.cumsum(sm, axis=0) - sm[0]
    c_act = jax.nn.silu(c.astype(F32))
    cc_act = jax.nn.silu(c_ctx.astype(F32))
    h_lat, h_ctx = x, ctx
    for l in range(DEPTH):
        last = l == DEPTH - 1
        ada_wl = ada_w[l].astype(F32)
        mod_lat = jnp.split((c_act @ ada_wl + ada_b[l])[:, None, :], 6, axis=-1)
        mod_ctx = jnp.split((cc_act @ ada_wl + ada_b[l])[None, None, :], 6, axis=-1)
        u_lat = h_lat * (1.0 + mod_lat[1]) + mod_lat[0]
        u_ctx = h_ctx * (1.0 + mod_ctx[1]) + mod_ctx[0]
        if l % 2 == 0:
            e = l // 2
            y_lat, y_ctx = even_mixer(u_lat, u_ctx, even_w_in[e], mlstm_gate_b[e], mlstm_norm_w[e], lower_bounds[e], hgrn_norm_w[e], even_w_out[e])
        else:
            o = l // 2
            hyena_p = (hy_conv_w[o], hy_conv_b[o], hy_f_w1[o], hy_f_b1[o], hy_f_w2[o], hy_f_b2[o], hy_f_w3[o], hy_f_b3[o], hy_f_freq[o], hy_f_wout[o], hy_skip[o], hy_norm_w[o])
            y_lat, y_ctx = odd_mixer(u_lat, u_ctx, odd_w_in[o], hyena_p, ret_log_decay[o], ret_norm_w[o], odd_w_out[o], row, col)
        h_lat = layer_norm(DEEPNORM_ALPHA * h_lat + (mod_lat[2] * y_lat).astype(h_lat.dtype), ln_w[l, 0], ln_b[l, 0])
        v_lat = (h_lat * (1.0 + mod_lat[4]) + mod_lat[3]).reshape(batch * n_lat, d)
        if last:
            f_lat = moe_ffn(v_lat, router_w[l], router_bias[l], exp_w1[l], exp_w3[l], exp_w2[l], sh_w1[l], sh_w3[l], sh_w2[l])
        else:
            h_ctx = layer_norm(DEEPNORM_ALPHA * h_ctx + (mod_ctx[2] * y_ctx).astype(h_ctx.dtype), ln_w[l, 0], ln_b[l, 0])
            v_ctx = (h_ctx * (1.0 + mod_ctx[4]) + mod_ctx[3]).reshape(batch * n_ctx, d)
            f_all = moe_ffn(jnp.concatenate([v_lat, v_ctx], 0), router_w[l], router_bias[l], exp_w1[l], exp_w3[l], exp_w2[l], sh_w1[l], sh_w3[l], sh_w2[l])
            f_lat = f_all[:batch * n_lat]
            f_ctx = f_all[batch * n_lat:].reshape(batch, n_ctx, d)
            h_ctx = layer_norm(DEEPNORM_ALPHA * h_ctx + (mod_ctx[5] * f_ctx).astype(h_ctx.dtype), ln_w[l, 1], ln_b[l, 1])
        h_lat = layer_norm(DEEPNORM_ALPHA * h_lat + (mod_lat[5] * f_lat.reshape(batch, n_lat, d)).astype(h_lat.dtype), ln_w[l, 1], ln_b[l, 1])
    return h_lat
```

```python
import math
import jax
import jax.numpy as jnp
from jax import lax
from jax.experimental import pallas as pl
from jax.experimental.pallas import tpu as pltpu

D_MODEL = 1024
BATCH = 8
SEQ = 4096
DEPTH = 4

GRID_W = 64
CTX_LEN = 256
N_EVEN = (DEPTH + 1) // 2
N_ODD = DEPTH // 2

MLSTM_HEADS = 4
MLSTM_DQK = 64
MLSTM_DV = 128
MLSTM_CHUNK = 64
HGRN_HEADS = 4
HGRN_DK = 128
HGRN_DV = 128
HGRN_CHUNK = 32
HYENA_WIDTH = 512
HYENA_GROUPS = 8
HYENA_ORDER = 2
HYENA_EMB = 33
HYENA_FILTER_HIDDEN = 64
HYENA_SHORT = 3
HYENA_TARGET = 1e-2
HYENA_FAST_PCT = 0.3
HYENA_SLOW_PCT = 1.5
RET_HEADS = 4
RET_DQK = 128
RET_DV = 128
RET_CHUNK = 64
ROPE_BASE = 10000.0
N_EXPERTS = 64
TOP_K = 8
EXPERT_HIDDEN = 256
SHARED_HIDDEN = 256
ROUTED_SCALE = 2.5
MOE_BLOCK = 128
DEEPNORM_ALPHA = (2 * DEPTH) ** 0.25
DEEPNORM_BETA = (8 * DEPTH) ** -0.25
EPS = 1e-5
NEG_BIG = -1e30

MLSTM_SPLITS = (MLSTM_HEADS * MLSTM_DQK, MLSTM_HEADS * MLSTM_DQK, MLSTM_HEADS * MLSTM_DV, MLSTM_HEADS * MLSTM_DV, 4 * MLSTM_HEADS)
HGRN_SPLITS = (HGRN_HEADS * HGRN_DK, HGRN_HEADS * HGRN_DV, HGRN_HEADS * HGRN_DK, HGRN_HEADS * HGRN_DK, HGRN_HEADS * HGRN_DV)
EVEN_SPLITS = MLSTM_SPLITS + HGRN_SPLITS
EVEN_PROJ = sum(EVEN_SPLITS)
EVEN_MIX = MLSTM_HEADS * MLSTM_DV + HGRN_HEADS * HGRN_DV
RET_SPLITS = (RET_HEADS * RET_DQK, RET_HEADS * RET_DQK, RET_HEADS * RET_DV, RET_HEADS * RET_DV)
ODD_SPLITS = ((HYENA_ORDER + 1) * HYENA_WIDTH,) + RET_SPLITS
ODD_PROJ = sum(ODD_SPLITS)
ODD_MIX = HYENA_WIDTH + RET_HEADS * RET_DV

F32 = jnp.float32
BF16 = jnp.bfloat16


def _mm_kernel(x_ref, w_ref, o_ref):
    o_ref[...] = jnp.dot(x_ref[...].astype(BF16), w_ref[...].astype(BF16), preferred_element_type=F32)


def pallas_matmul(x, w, tm=512, tn=512):
    t, k = x.shape
    n = w.shape[1]
    n_pad = -(-n // tn) * tn
    if n_pad != n:
        w = jnp.pad(w, ((0, 0), (0, n_pad - n)))
    out = pl.pallas_call(
        _mm_kernel,
        grid=(t // tm, n_pad // tn),
        in_specs=[pl.BlockSpec((tm, k), lambda i, j: (i, 0)), pl.BlockSpec((k, tn), lambda i, j: (0, j))],
        out_specs=pl.BlockSpec((tm, tn), lambda i, j: (i, j)),
        out_shape=jax.ShapeDtypeStruct((t, n_pad), F32),
    )(x, w)
    return out[:, :n]


def mm3(x, w):
    b, l, d = x.shape
    return pallas_matmul(x.reshape(b * l, d), w).reshape(b, l, -1)


def split_cols(a, sizes):
    out, start = [], 0
    for s in sizes:
        out.append(a[..., start:start + s])
        start += s
    return out


def to_heads(a, n_heads):
    b, l, w = a.shape
    return a.reshape(b, l, n_heads, w // n_heads).transpose(0, 2, 1, 3)


def chunk(a, cs):
    b, h, l = a.shape[:3]
    return a.reshape((b, h, l // cs, cs) + a.shape[3:])


def layer_norm(x, w, b):
    xf = x.astype(F32)
    mu = jnp.mean(xf, -1, keepdims=True)
    var = jnp.mean(jnp.square(xf - mu), -1, keepdims=True)
    return ((xf - mu) * lax.rsqrt(var + EPS) * w + b).astype(x.dtype)


def head_norm(h, w, rms):
    h = h.transpose(0, 2, 1, 3)
    if not rms:
        h = h - jnp.mean(h, -1, keepdims=True)
    h = h * lax.rsqrt(jnp.mean(jnp.square(h), -1, keepdims=True) + EPS)
    b, l, nh, dh = h.shape
    return h.reshape(b, l, nh * dh) * w


def group_rms(z, w):
    b, l, wd = z.shape
    zg = z.reshape(b, l, HYENA_GROUPS, wd // HYENA_GROUPS)
    zg = zg * lax.rsqrt(jnp.mean(jnp.square(zg), -1, keepdims=True) + EPS)
    return zg.reshape(b, l, wd) * w


def run_direction(scan_fn, ctx_seq, lat_seq, const, state0, reverse):
    if reverse:
        ctx_seq = tuple(jnp.flip(a, axis=2) for a in ctx_seq)
        lat_seq = tuple(jnp.flip(a, axis=2) for a in lat_seq)
    y_ctx, state = scan_fn(ctx_seq, const, state0)
    y_lat, _ = scan_fn(lat_seq, const, state)
    if reverse:
        y_ctx = jnp.flip(y_ctx, axis=2)
        y_lat = jnp.flip(y_lat, axis=2)
    return y_lat, y_ctx


def mlstm_scan(seq, const, state):
    q, k, v, ig, lf = seq
    cs = MLSTM_CHUNK
    qc, kc, vc, igc, lfc = (chunk(a, cs) for a in seq)
    b = jnp.cumsum(lfc, axis=-1)
    g = b[..., -1]
    a = g[..., None] - b + igc
    m_loc = jnp.max(a, axis=-1)
    w = jnp.exp(a - m_loc[..., None])
    c_loc = jnp.einsum('bhcs,bhcsv,bhcsk->bhcvk', w, vc, kc)
    n_loc = jnp.einsum('bhcs,bhcsk->bhck', w, kc)

    def step(carry, inp):
        c_st, n_st, m_st = carry
        g_j, ml_j, cl_j, nl_j = inp
        m_new = jnp.maximum(g_j + m_st, ml_j)
        sp = jnp.exp(g_j + m_st - m_new)
        sl = jnp.exp(ml_j - m_new)
        new = (sp[..., None, None] * c_st + sl[..., None, None] * cl_j, sp[..., None] * n_st + sl[..., None] * nl_j, m_new)
        return new, carry

    final, (c_in, n_in, m_in) = lax.scan(step, state, tuple(jnp.moveaxis(t, 2, 0) for t in (g, m_loc, c_loc, n_loc)))
    c_in, n_in, m_in = jnp.moveaxis(c_in, 0, 2), jnp.moveaxis(n_in, 0, 2), jnp.moveaxis(m_in, 0, 2)
    tri = jnp.tril(jnp.ones((cs, cs), bool))
    log_d = jnp.where(tri, b[..., :, None] - b[..., None, :] + igc[..., None, :], -jnp.inf)
    inter = b + m_in[..., None]
    m_out = jnp.maximum(inter, jnp.max(log_d, -1))
    s = jnp.einsum('bhctk,bhcsk->bhcts', qc, kc) * jnp.exp(log_d - m_out[..., None])
    e_int = jnp.exp(inter - m_out)
    num = jnp.einsum('bhcts,bhcsv->bhctv', s, vc) + e_int[..., None] * jnp.einsum('bhctk,bhcvk->bhctv', qc, c_in)
    den = jnp.sum(s, -1) + e_int * jnp.einsum('bhctk,bhck->bhct', qc, n_in)
    h = num / jnp.maximum(jnp.abs(den), jnp.exp(-m_out))[..., None]
    return h.reshape(v.shape), final


def hgrn2_scan(seq, const, state):
    q, k, v, lf = seq
    cs = HGRN_CHUNK
    qc, kc, vc, lfc = (chunk(a, cs) for a in seq)
    a_cum = jnp.cumsum(lfc, axis=3)
    a_end = a_cum[:, :, :, -1]
    s_loc = jnp.einsum('bhcsk,bhcsv->bhckv', kc * jnp.exp(a_end[:, :, :, None] - a_cum), vc)

    def step(s_st, inp):
        a_j, sl_j = inp
        return jnp.exp(a_j)[..., None] * s_st + sl_j, s_st

    final, s_in = lax.scan(step, state, (jnp.moveaxis(a_end, 2, 0), jnp.moveaxis(s_loc, 2, 0)))
    s_in = jnp.moveaxis(s_in, 0, 2)
    inter = jnp.einsum('bhctk,bhckv->bhctv', qc * jnp.exp(a_cum), s_in)
    a_mid = a_cum[:, :, :, cs // 2 - 1:cs // 2]
    tri = jnp.tril(jnp.ones((cs, cs), bool))
    sc = jnp.einsum('bhctk,bhcsk->bhcts', qc * jnp.exp(a_cum - a_mid), kc * jnp.exp(a_mid - a_cum))
    sc = jnp.where(tri, sc, 0.0)
    out = inter + jnp.einsum('bhcts,bhcsv->bhctv', sc, vc)
    return out.reshape(v.shape), final


def retention_scan(seq, log_gamma, state):
    q, k, v = seq
    cs = RET_CHUNK
    qc, kc, vc = (chunk(a, cs) for a in seq)
    pos = jnp.arange(cs, dtype=F32)
    tri = jnp.tril(jnp.ones((cs, cs), bool))
    delta = pos[:, None] - pos[None, :]
    decay = jnp.exp(jnp.where(tri, delta[None] * log_gamma[:, None, None], -jnp.inf))
    sc = jnp.einsum('bhctk,bhcsk->bhcts', qc, kc) * decay[:, None]
    intra = jnp.einsum('bhcts,bhcsv->bhctv', sc, vc)
    q_dec = jnp.exp((pos + 1.0)[None, :] * log_gamma[:, None])
    k_dec = jnp.exp((cs - 1.0 - pos)[None, :] * log_gamma[:, None])
    s_loc = jnp.einsum('bhcsk,hs,bhcsv->bhckv', kc, k_dec, vc)
    chunk_decay = jnp.exp(cs * log_gamma)[None, :, None, None]

    def step(s_st, sl_j):
        return chunk_decay * s_st + sl_j, s_st

    final, s_in = lax.scan(step, state, jnp.moveaxis(s_loc, 2, 0))
    s_in = jnp.moveaxis(s_in, 0, 2)
    inter = jnp.einsum('bhctk,ht,bhckv->bhctv', qc, q_dec, s_in)
    return (inter + intra).reshape(v.shape), final


def rope_1d(x, pos):
    d = x.shape[-1]
    inv = ROPE_BASE ** (-jnp.arange(0, d, 2, dtype=F32) / d)
    ang = pos[:, None] * inv[None, :]
    cos = jnp.cos(ang)[None, :, None, :]
    sin = jnp.sin(ang)[None, :, None, :]
    x1, x2 = x[..., :d // 2], x[..., d // 2:]
    return jnp.concatenate([x1 * cos - x2 * sin, x1 * sin + x2 * cos], -1)


def axial_rope(x, row, col):
    half = x.shape[-1] // 2
    return jnp.concatenate([rope_1d(x[..., :half], row), rope_1d(x[..., half:], col)], -1)


def mlstm_group(cols_lat, cols_ctx, gate_b, norm_w):
    def prep(cols):
        q, k, v, o, gates = cols
        b, l, _ = q.shape
        q = to_heads(q, MLSTM_HEADS) * MLSTM_DQK ** -0.5
        k = to_heads(k, MLSTM_HEADS)
        v = to_heads(v, MLSTM_HEADS)
        gates = (gates.reshape(b, l, 4, MLSTM_HEADS) + gate_b).transpose(2, 0, 3, 1)
        fwd = (q, k, v, gates[0], jax.nn.log_sigmoid(gates[1]))
        bwd = (q, k, v, gates[2], jax.nn.log_sigmoid(gates[3]))
        return fwd, bwd, o

    fwd_l, bwd_l, o_l = prep(cols_lat)
    fwd_c, bwd_c, o_c = prep(cols_ctx)
    batch = o_l.shape[0]
    state0 = (jnp.zeros((batch, MLSTM_HEADS, MLSTM_DV, MLSTM_DQK), F32),
              jnp.zeros((batch, MLSTM_HEADS, MLSTM_DQK), F32),
              jnp.full((batch, MLSTM_HEADS), NEG_BIG, F32))
    hf_lat, hf_ctx = run_direction(mlstm_scan, fwd_c, fwd_l, None, state0, False)
    hb_lat, hb_ctx = run_direction(mlstm_scan, bwd_c, bwd_l, None, state0, True)
    y_lat = head_norm(hf_lat + hb_lat, norm_w, False) * jax.nn.sigmoid(o_l)
    y_ctx = head_norm(hf_ctx + hb_ctx, norm_w, False) * jax.nn.sigmoid(o_c)
    return y_lat, y_ctx


def hgrn_group(cols_lat, cols_ctx, lb, norm_w):
    def prep(cols):
        q, i, f_fwd, f_bwd, g = cols
        q = to_heads(jax.nn.silu(q), HGRN_HEADS)
        v = to_heads(i, HGRN_HEADS)

        def gate(fp):
            f = lb + (1.0 - lb) * jax.nn.sigmoid(fp)
            return to_heads(1.0 - f, HGRN_HEADS), to_heads(jnp.log(f), HGRN_HEADS)

        k_f, lf_f = gate(f_fwd)
        k_b, lf_b = gate(f_bwd)
        return (q, k_f, v, lf_f), (q, k_b, v, lf_b), g

    fwd_l, bwd_l, g_l = prep(cols_lat)
    fwd_c, bwd_c, g_c = prep(cols_ctx)
    state0 = jnp.zeros((g_l.shape[0], HGRN_HEADS, HGRN_DK, HGRN_DV), F32)
    yf_lat, yf_ctx = run_direction(hgrn2_scan, fwd_c, fwd_l, None, state0, False)
    yb_lat, yb_ctx = run_direction(hgrn2_scan, bwd_c, bwd_l, None, state0, True)
    y_lat = head_norm(yf_lat + yb_lat, norm_w, True) * jax.nn.silu(g_l)
    y_ctx = head_norm(yf_ctx + yb_ctx, norm_w, True) * jax.nn.silu(g_c)
    return y_lat, y_ctx


def hyena_filters(length, fw1, fb1, fw2, fb2, fw3, fb3, freq, wout):
    pos = jnp.arange(length, dtype=F32)
    t = (pos / max(length - 1, 1))[:, None]
    bands = (HYENA_EMB - 1) // 2
    w = 2.0 * math.pi * pos[:, None] / length * jnp.linspace(1e-4, bands - 1, bands)[None, :]
    z = jnp.concatenate([t, jnp.cos(w), jnp.sin(w)], -1)
    hid = jnp.sin(freq * (z @ fw1 + fb1))
    hid = jnp.sin(freq * (hid @ fw2 + fb2))
    hid = jnp.sin(freq * (hid @ fw3 + fb3))
    h = (hid @ wout).reshape(length, HYENA_ORDER, 2, HYENA_WIDTH)
    deltas = jnp.abs(jnp.linspace(math.log(HYENA_TARGET) / HYENA_SLOW_PCT, math.log(HYENA_TARGET) / HYENA_FAST_PCT, HYENA_WIDTH))
    decay = jnp.exp(-t * deltas[None, :])
    return (h * decay[:, None, None, :]).transpose(1, 2, 0, 3)


def bidir_fftconv(u, h_fwd, h_bwd, d_skip):
    length, width = h_fwd.shape
    kern = jnp.concatenate([h_fwd, jnp.zeros((1, width), F32), jnp.flip(h_bwd[1:], axis=0)], axis=0)
    spec = jnp.fft.rfft(u, n=2 * length, axis=1) * jnp.fft.rfft(kern, axis=0)[None]
    y = jnp.fft.irfft(spec, n=2 * length, axis=1)[:, :length]
    return y + u * d_skip


def hyena_group(u_lat, u_ctx, conv_w, conv_b, fw1, fb1, fw2, fb2, fw3, fb3, freq, wout, skip, norm_w):
    def run(u):
        length = u.shape[1]
        up = jnp.pad(u, ((0, 0), (1, 1), (0, 0)))
        u = up[:, :-2] * conv_w[0] + up[:, 1:-1] * conv_w[1] + up[:, 2:] * conv_w[2] + conv_b
        streams = jnp.split(u, HYENA_ORDER + 1, axis=-1)
        gates, z = streams[:-1], streams[-1]
        h = hyena_filters(length, fw1, fb1, fw2, fb2, fw3, fb3, freq, wout)
        for n in range(HYENA_ORDER):
            z = gates[n] * bidir_fftconv(z, h[n, 0], h[n, 1], skip[n])
        return group_rms(z, norm_w)

    return run(u_lat), run(u_ctx)


def retention_group(cols_lat, cols_ctx, log_decay, norm_w, row, col):
    def prep(cols, rotary):
        q, k, v, g = cols
        b, l, _ = q.shape
        q = q.reshape(b, l, RET_HEADS, RET_DQK)
        k = k.reshape(b, l, RET_HEADS, RET_DQK)
        if rotary:
            q = axial_rope(q, row, col)
            k = axial_rope(k, row, col)
        q = q.transpose(0, 2, 1, 3)
        k = k.transpose(0, 2, 1, 3) * RET_DQK ** -0.5
        return (q, k, to_heads(v, RET_HEADS)), g

    lat, g_l = prep(cols_lat, True)
    ctx, g_c = prep(cols_ctx, False)
    state0 = jnp.zeros((g_l.shape[0], RET_HEADS, RET_DQK, RET_DV), F32)
    yf_lat, yf_ctx = run_direction(retention_scan, ctx, lat, log_decay[0], state0, False)
    yb_lat, yb_ctx = run_direction(retention_scan, ctx, lat, log_decay[1], state0, True)
    y_lat = head_norm(yf_lat + yb_lat, norm_w, False) * jax.nn.silu(g_l)
    y_ctx = head_norm(yf_ctx + yb_ctx, norm_w, False) * jax.nn.silu(g_c)
    return y_lat, y_ctx


def even_mixer(u_lat, u_ctx, w_in, gate_b, mlstm_norm_w, lb, hgrn_norm_w, w_out):
    w_in = w_in.astype(F32)
    w_out = w_out.astype(F32)
    p_lat = split_cols(mm3(u_lat.astype(F32), w_in), EVEN_SPLITS)
    p_ctx = split_cols(mm3(u_ctx.astype(F32), w_in), EVEN_SPLITS)
    n_a = len(MLSTM_SPLITS)
    a_lat, a_ctx = mlstm_group(p_lat[:n_a], p_ctx[:n_a], gate_b.astype(F32), mlstm_norm_w.astype(F32))
    b_lat, b_ctx = hgrn_group(p_lat[n_a:], p_ctx[n_a:], lb, hgrn_norm_w.astype(F32))
    return mm3(jnp.concatenate([a_lat, b_lat], -1), w_out), mm3(jnp.concatenate([a_ctx, b_ctx], -1), w_out)


def odd_mixer(u_lat, u_ctx, w_in, hyena_p, log_decay, ret_norm_w, w_out, row, col):
    w_in = w_in.astype(F32)
    w_out = w_out.astype(F32)
    p_lat = split_cols(mm3(u_lat.astype(F32), w_in), ODD_SPLITS)
    p_ctx = split_cols(mm3(u_ctx.astype(F32), w_in), ODD_SPLITS)
    hy_lat, hy_ctx = hyena_group(p_lat[0], p_ctx[0], *[p.astype(F32) for p in hyena_p])
    r_lat, r_ctx = retention_group(p_lat[1:], p_ctx[1:], log_decay.astype(F32), ret_norm_w.astype(F32), row, col)
    return mm3(jnp.concatenate([hy_lat, r_lat], -1), w_out), mm3(jnp.concatenate([hy_ctx, r_ctx], -1), w_out)


def moe_ffn(t, router_w, router_bias, w1, w3, w2, sw1, sw3, sw2):
    n_tok, d = t.shape
    tf = t.astype(F32)
    scores = jax.nn.sigmoid(tf @ router_w.astype(F32))
    _, idx = lax.top_k(scores + router_bias.astype(F32), TOP_K)
    gate = jnp.take_along_axis(scores, idx, axis=1)
    gate = ROUTED_SCALE * gate / jnp.sum(gate, -1, keepdims=True)
    n_assign = n_tok * TOP_K
    e_flat = idx.reshape(-1)
    g_flat = gate.reshape(-1)
    tok_flat = jnp.arange(n_assign, dtype=jnp.int32) // TOP_K
    order = jnp.argsort(e_flat)
    e_s, tok_s, g_s = e_flat[order], tok_flat[order], g_flat[order]
    counts = jnp.bincount(e_flat, length=N_EXPERTS)
    padded = (counts + MOE_BLOCK - 1) // MOE_BLOCK * MOE_BLOCK
    start = jnp.cumsum(counts) - counts
    pend = jnp.cumsum(padded)
    pstart = pend - padded
    dest = pstart[e_s] + jnp.arange(n_assign, dtype=jnp.int32) - start[e_s]
    n_blocks = -(-(n_assign + N_EXPERTS * (MOE_BLOCK - 1)) // MOE_BLOCK)
    slot_tok = jnp.full((n_blocks * MOE_BLOCK,), n_tok, jnp.int32).at[dest].set(tok_s)
    slot_gate = jnp.zeros((n_blocks * MOE_BLOCK,), F32).at[dest].set(g_s)
    block_e = jnp.minimum(jnp.searchsorted(pend, jnp.arange(n_blocks, dtype=jnp.int32) * MOE_BLOCK, side='right'), N_EXPERTS - 1)
    t_ext = jnp.concatenate([tf, jnp.zeros((1, d), F32)], 0)

    def expert_block(acc, blk):
        tok, g, e = blk
        xb = t_ext[tok]
        hb = jax.nn.silu(xb @ w1[e]) * (xb @ w3[e])
        return acc.at[tok].add((hb @ w2[e]) * g[:, None]), None

    acc, _ = lax.scan(expert_block, jnp.zeros((n_tok + 1, d), F32),
                      (slot_tok.reshape(n_blocks, MOE_BLOCK), slot_gate.reshape(n_blocks, MOE_BLOCK), block_e))
    shared = (jax.nn.silu(tf @ sw1) * (tf @ sw3)) @ sw2
    return acc[:n_tok] + shared


def kernel(x, c, ctx, c_ctx, ada_w, ada_b, ln_w, ln_b, even_w_in, mlstm_gate_b, mlstm_norm_w, hgrn_lb, hgrn_norm_w, even_w_out, odd_w_in, hy_conv_w, hy_conv_b, hy_f_w1, hy_f_b1, hy_f_w2, hy_f_b2, hy_f_w3, hy_f_b3, hy_f_freq, hy_f_wout, hy_skip, hy_norm_w, ret_log_decay, ret_norm_w, odd_w_out, router_w, router_bias, exp_w1, exp_w3, exp_w2, sh_w1, sh_w3, sh_w2):
    batch, n_lat, d = x.shape
    n_ctx = ctx.shape[1]
    rows = n_lat // GRID_W
    row = jnp.repeat(jnp.arange(rows, dtype=F32), GRID_W)
    col = jnp.tile(jnp.arange(GRID_W, dtype=F32), rows)
    sm = jax.nn.softmax(hgrn_lb.astype(F32), axis=0)
    lower_bounds = jnp.cumsum(sm, axis=0) - sm[0]
    c_act = jax.nn.silu(c.astype(F32))
    cc_act = jax.nn.silu(c_ctx.astype(F32))
    h_lat, h_ctx = x, ctx
    for l in range(DEPTH):
        last = l == DEPTH - 1
        ada_wl = ada_w[l].astype(F32)
        mod_lat = jnp.split((c_act @ ada_wl + ada_b[l])[:, None, :], 6, axis=-1)
        mod_ctx = jnp.split((cc_act @ ada_wl + ada_b[l])[None, None, :], 6, axis=-1)
        u_lat = h_lat * (1.0 + mod_lat[1]) + mod_lat[0]
        u_ctx = h_ctx * (1.0 + mod_ctx[1]) + mod_ctx[0]
        if l % 2 == 0:
            e = l // 2
            y_lat, y_ctx = even_mixer(u_lat, u_ctx, even_w_in[e], mlstm_gate_b[e], mlstm_norm_w[e], lower_bounds[e], hgrn_norm_w[e], even_w_out[e])
        else:
            o = l // 2
            hyena_p = (hy_conv_w[o], hy_conv_b[o], hy_f_w1[o], hy_f_b1[o], hy_f_w2[o], hy_f_b2[o], hy_f_w3[o], hy_f_b3[o], hy_f_freq[o], hy_f_wout[o], hy_skip[o], hy_norm_w[o])
            y_lat, y_ctx = odd_mixer(u_lat, u_ctx, odd_w_in[o], hyena_p, ret_log_decay[o], ret_norm_w[o], odd_w_out[o], row, col)
        h_lat = layer_norm(DEEPNORM_ALPHA * h_lat + (mod_lat[2] * y_lat).astype(h_lat.dtype), ln_w[l, 0], ln_b[l, 0])
        v_lat = (h_lat * (1.0 + mod_lat[4]) + mod_lat[3]).reshape(batch * n_lat, d)
        if last:
            f_lat = moe_ffn(v_lat, router_w[l], router_bias[l], exp_w1[l], exp_w3[l], exp_w2[l], sh_w1[l], sh_w3[l], sh_w2[l])
        else:
            h_ctx = layer_norm(DEEPNORM_ALPHA * h_ctx + (mod_ctx[2] * y_ctx).astype(h_ctx.dtype), ln_w[l, 0], ln_b[l, 0])
            v_ctx = (h_ctx * (1.0 + mod_ctx[4]) + mod_ctx[3]).reshape(batch * n_ctx, d)
            f_all = moe_ffn(jnp.concatenate([v_lat, v_ctx], 0), router_w[l], router_bias[l], exp_w1[l], exp_w3[l], exp_w2[l], sh_w1[l], sh_w3[l], sh_w2[l])
            f_lat = f_all[:batch * n_lat]
            f_ctx = f_all[batch * n_lat:].reshape(batch, n_ctx, d)
            h_ctx = layer_norm(DEEPNORM_ALPHA * h_ctx + (mod_ctx[5] * f_ctx).astype(h_ctx.dtype), ln_w[l, 1], ln_b[l, 1])
        h_lat = layer_norm(DEEPNORM_ALPHA * h_lat + (mod_lat[5] * f_lat.reshape(batch, n_lat, d)).astype(h_lat.dtype), ln_w[l, 1], ln_b[l, 1])
    return h_lat
```

```python
import math
import jax
import jax.numpy as jnp
from jax import lax
from jax.experimental import pallas as pl
from jax.experimental.pallas import tpu as pltpu

D_MODEL = 1024
BATCH = 8
SEQ = 4096
DEPTH = 4

GRID_W = 64
CTX_LEN = 256
N_EVEN = (DEPTH + 1) // 2
N_ODD = DEPTH // 2

MLSTM_HEADS = 4
MLSTM_DQK = 64
MLSTM_DV = 128
MLSTM_CHUNK = 64
HGRN_HEADS = 4
HGRN_DK = 128
HGRN_DV = 128
HGRN_CHUNK = 32
HYENA_WIDTH = 512
HYENA_GROUPS = 8
HYENA_ORDER = 2
HYENA_EMB = 33
HYENA_FILTER_HIDDEN = 64
HYENA_SHORT = 3
HYENA_TARGET = 1e-2
HYENA_FAST_PCT = 0.3
HYENA_SLOW_PCT = 1.5
RET_HEADS = 4
RET_DQK = 128
RET_DV = 128
RET_CHUNK = 64
ROPE_BASE = 10000.0
N_EXPERTS = 64
TOP_K = 8
EXPERT_HIDDEN = 256
SHARED_HIDDEN = 256
ROUTED_SCALE = 2.5
MOE_BLOCK = 128
DEEPNORM_ALPHA = (2 * DEPTH) ** 0.25
DEEPNORM_BETA = (8 * DEPTH) ** -0.25
EPS = 1e-5
NEG_BIG = -1e30

MLSTM_SPLITS = (MLSTM_HEADS * MLSTM_DQK, MLSTM_HEADS * MLSTM_DQK, MLSTM_HEADS * MLSTM_DV, MLSTM_HEADS * MLSTM_DV, 4 * MLSTM_HEADS)
HGRN_SPLITS = (HGRN_HEADS * HGRN_DK, HGRN_HEADS * HGRN_DV, HGRN_HEADS * HGRN_DK, HGRN_HEADS * HGRN_DK, HGRN_HEADS * HGRN_DV)
EVEN_SPLITS = MLSTM_SPLITS + HGRN_SPLITS
EVEN_PROJ = sum(EVEN_SPLITS)
EVEN_MIX = MLSTM_HEADS * MLSTM_DV + HGRN_HEADS * HGRN_DV
RET_SPLITS = (RET_HEADS * RET_DQK, RET_HEADS * RET_DQK, RET_HEADS * RET_DV, RET_HEADS * RET_DV)
ODD_SPLITS = ((HYENA_ORDER + 1) * HYENA_WIDTH,) + RET_SPLITS
ODD_PROJ = sum(ODD_SPLITS)
ODD_MIX = HYENA_WIDTH + RET_HEADS * RET_DV

F32 = jnp.float32
BF16 = jnp.bfloat16


def _mm_kernel(x_ref, w_ref, o_ref):
    o_ref[...] = jnp.dot(x_ref[...].astype(BF16), w_ref[...].astype(BF16), preferred_element_type=F32)


def pallas_matmul(x, w, tm=512, tn=512):
    t, k = x.shape
    n = w.shape[1]
    n_pad = -(-n // tn) * tn
    if n_pad != n:
        w = jnp.pad(w, ((0, 0), (0, n_pad - n)))
    out = pl.pallas_call(
        _mm_kernel,
        grid=(t // tm, n_pad // tn),
        in_specs=[pl.BlockSpec((tm, k), lambda i, j: (i, 0)), pl.BlockSpec((k, tn), lambda i, j: (0, j))],
        out_specs=pl.BlockSpec((tm, tn), lambda i, j: (i, j)),
        out_shape=jax.ShapeDtypeStruct((t, n_pad), F32),
    )(x, w)
    return out[:, :n]


def mm3(x, w):
    b, l, d = x.shape
    return pallas_matmul(x.reshape(b * l, d), w).reshape(b, l, -1)


def split_cols(a, sizes):
    out, start = [], 0
    for s in sizes:
        out.append(a[..., start:start + s])
        start += s
    return out


def to_heads(a, n_heads):
    b, l, w = a.shape
    return a.reshape(b, l, n_heads, w // n_heads).transpose(0, 2, 1, 3)


def chunk(a, cs):
    b, h, l = a.shape[:3]
    return a.reshape((b, h, l // cs, cs) + a.shape[3:])


def layer_norm(x, w, b):
    xf = x.astype(F32)
    mu = jnp.mean(xf, -1, keepdims=True)
    var = jnp.mean(jnp.square(xf - mu), -1, keepdims=True)
    return ((xf - mu) * lax.rsqrt(var + EPS) * w + b).astype(x.dtype)


def head_norm(h, w, rms):
    h = h.transpose(0, 2, 1, 3)
    if not rms:
        h = h - jnp.mean(h, -1, keepdims=True)
    h = h * lax.rsqrt(jnp.mean(jnp.square(h), -1, keepdims=True) + EPS)
    b, l, nh, dh = h.shape
    return h.reshape(b, l, nh * dh) * w


def group_rms(z, w):
    b, l, wd = z.shape
    zg = z.reshape(b, l, HYENA_GROUPS, wd // HYENA_GROUPS)
    zg = zg * lax.rsqrt(jnp.mean(jnp.square(zg), -1, keepdims=True) + EPS)
    return zg.reshape(b, l, wd) * w


def run_direction(scan_fn, ctx_seq, lat_seq, const, state0, reverse):
    if reverse:
        ctx_seq = tuple(jnp.flip(a, axis=2) for a in ctx_seq)
        lat_seq = tuple(jnp.flip(a, axis=2) for a in lat_seq)
    y_ctx, state = scan_fn(ctx_seq, const, state0)
    y_lat, _ = scan_fn(lat_seq, const, state)
    if reverse:
        y_ctx = jnp.flip(y_ctx, axis=2)
        y_lat = jnp.flip(y_lat, axis=2)
    return y_lat, y_ctx


def mlstm_scan(seq, const, state):
    q, k, v, ig, lf = seq
    cs = MLSTM_CHUNK
    qc, kc, vc, igc, lfc = (chunk(a, cs) for a in seq)
    b = jnp.cumsum(lfc, axis=-1)
    g = b[..., -1]
    a = g[..., None] - b + igc
    m_loc = jnp.max(a, axis=-1)
    w = jnp.exp(a - m_loc[..., None])
    c_loc = jnp.einsum('bhcs,bhcsv,bhcsk->bhcvk', w, vc, kc)
    n_loc = jnp.einsum('bhcs,bhcsk->bhck', w, kc)

    def step(carry, inp):
        c_st, n_st, m_st = carry
        g_j, ml_j, cl_j, nl_j = inp
        m_new = jnp.maximum(g_j + m_st, ml_j)
        sp = jnp.exp(g_j + m_st - m_new)
        sl = jnp.exp(ml_j - m_new)
        new = (sp[..., None, None] * c_st + sl[..., None, None] * cl_j, sp[..., None] * n_st + sl[..., None] * nl_j, m_new)
        return new, carry

    final, (c_in, n_in, m_in) = lax.scan(step, state, tuple(jnp.moveaxis(t, 2, 0) for t in (g, m_loc, c_loc, n_loc)))
    c_in, n_in, m_in = jnp.moveaxis(c_in, 0, 2), jnp.moveaxis(n_in, 0, 2), jnp.moveaxis(m_in, 0, 2)
    tri = jnp.tril(jnp.ones((cs, cs), bool))
    log_d = jnp.where(tri, b[..., :, None] - b[..., None, :] + igc[..., None, :], -jnp.inf)
    inter = b + m_in[..., None]
    m_out = jnp.maximum(inter, jnp.max(log_d, -1))
    s = jnp.einsum('bhctk,bhcsk->bhcts', qc, kc) * jnp.exp(log_d - m_out[..., None])
    e_int = jnp.exp(inter - m_out)
    num = jnp.einsum('bhcts,bhcsv->bhctv', s, vc) + e_int[..., None] * jnp.einsum('bhctk,bhcvk->bhctv', qc, c_in)
    den = jnp.sum(s, -1) + e_int * jnp.einsum('bhctk,bhck->bhct', qc, n_in)
    h = num / jnp.maximum(jnp.abs(den), jnp.exp(-m_out))[..., None]
    return h.reshape(v.shape), final


def hgrn2_scan(seq, const, state):
    q, k, v, lf = seq
    cs = HGRN_CHUNK
    qc, kc, vc, lfc = (chunk(a, cs) for a in seq)
    a_cum = jnp.cumsum(lfc, axis=3)
    a_end = a_cum[:, :, :, -1]
    s_loc = jnp.einsum('bhcsk,bhcsv->bhckv', kc * jnp.exp(a_end[:, :, :, None] - a_cum), vc)

    def step(s_st, inp):
        a_j, sl_j = inp
        return jnp.exp(a_j)[..., None] * s_st + sl_j, s_st

    final, s_in = lax.scan(step, state, (jnp.moveaxis(a_end, 2, 0), jnp.moveaxis(s_loc, 2, 0)))
    s_in = jnp.moveaxis(s_in, 0, 2)
    inter = jnp.einsum('bhctk,bhckv->bhctv', qc * jnp.exp(a_cum), s_in)
    a_mid = a_cum[:, :, :, cs // 2 - 1:cs // 2]
    tri = jnp.tril(jnp.ones((cs, cs), bool))
    sc = jnp.einsum('bhctk,bhcsk->bhcts', qc * jnp.exp(a_cum - a_mid), kc * jnp.exp(a_mid - a_cum))
    sc = jnp.where(tri, sc, 0.0)
    out = inter + jnp.einsum('bhcts,bhcsv->bhctv', sc, vc)
    return out.reshape(v.shape), final


def retention_scan(seq, log_gamma, state):
    q, k, v = seq
    cs = RET_CHUNK
    qc, kc, vc = (chunk(a, cs) for a in seq)
    pos = jnp.arange(cs, dtype=F32)
    tri = jnp.tril(jnp.ones((cs, cs), bool))
    delta = pos[:, None] - pos[None, :]
    decay = jnp.exp(jnp.where(tri, delta[None] * log_gamma[:, None, None], -jnp.inf))
    sc = jnp.einsum('bhctk,bhcsk->bhcts', qc, kc) * decay[:, None]
    intra = jnp.einsum('bhcts,bhcsv->bhctv', sc, vc)
    q_dec = jnp.exp((pos + 1.0)[None, :] * log_gamma[:, None])
    k_dec = jnp.exp((cs - 1.0 - pos)[None, :] * log_gamma[:, None])
    s_loc = jnp.einsum('bhcsk,hs,bhcsv->bhckv', kc, k_dec, vc)
    chunk_decay = jnp.exp(cs * log_gamma)[None, :, None, None]

    def step(s_st, sl_j):
        return chunk_decay * s_st + sl_j, s_st

    final, s_in = lax.scan(step, state, jnp.moveaxis(s_loc, 2, 0))
    s_in = jnp.moveaxis(s_in, 0, 2)
    inter = jnp.einsum('bhctk,ht,bhckv->bhctv', qc, q_dec, s_in)
    return (inter + intra).reshape(v.shape), final


def rope_1d(x, pos):
    d = x.shape[-1]
    inv = ROPE_BASE ** (-jnp.arange(0, d, 2, dtype=F32) / d)
    ang = pos[:, None] * inv[None, :]
    cos = jnp.cos(ang)[None, :, None, :]
    sin = jnp.sin(ang)[None, :, None, :]
    x1, x2 = x[..., :d // 2], x[..., d // 2:]
    return jnp.concatenate([x1 * cos - x2 * sin, x1 * sin + x2 * cos], -1)


def axial_rope(x, row, col):
    half = x.shape[-1] // 2
    return jnp.concatenate([rope_1d(x[..., :half], row), rope_1d(x[..., half:], col)], -1)


def mlstm_group(cols_lat, cols_ctx, gate_b, norm_w):
    def prep(cols):
        q, k, v, o, gates = cols
        b, l, _ = q.shape
        q = to_heads(q, MLSTM_HEADS) * MLSTM_DQK ** -0.5
        k = to_heads(k, MLSTM_HEADS)
        v = to_heads(v, MLSTM_HEADS)
        gates = (gates.reshape(b, l, 4, MLSTM_HEADS) + gate_b).transpose(2, 0, 3, 1)
        fwd = (q, k, v, gates[0], jax.nn.log_sigmoid(gates[1]))
        bwd = (q, k, v, gates[2], jax.nn.log_sigmoid(gates[3]))
        return fwd, bwd, o

    fwd_l, bwd_l, o_l = prep(cols_lat)
    fwd_c, bwd_c, o_c = prep(cols_ctx)
    batch = o_l.shape[0]
    state0 = (jnp.zeros((batch, MLSTM_HEADS, MLSTM_DV, MLSTM_DQK), F32),
              jnp.zeros((batch, MLSTM_HEADS, MLSTM_DQK), F32),
              jnp.full((batch, MLSTM_HEADS), NEG_BIG, F32))
    hf_lat, hf_ctx = run_direction(mlstm_scan, fwd_c, fwd_l, None, state0, False)
    hb_lat, hb_ctx = run_direction(mlstm_scan, bwd_c, bwd_l, None, state0, True)
    y_lat = head_norm(hf_lat + hb_lat, norm_w, False) * jax.nn.sigmoid(o_l)
    y_ctx = head_norm(hf_ctx + hb_ctx, norm_w, False) * jax.nn.sigmoid(o_c)
    return y_lat, y_ctx


def hgrn_group(cols_lat, cols_ctx, lb, norm_w):
    def prep(cols):
        q, i, f_fwd, f_bwd, g = cols
        q = to_heads(jax.nn.silu(q), HGRN_HEADS)
        v = to_heads(i, HGRN_HEADS)

        def gate(fp):
            f = lb + (1.0 - lb) * jax.nn.sigmoid(fp)
            return to_heads(1.0 - f, HGRN_HEADS), to_heads(jnp.log(f), HGRN_HEADS)

        k_f, lf_f = gate(f_fwd)
        k_b, lf_b = gate(f_bwd)
        return (q, k_f, v, lf_f), (q, k_b, v, lf_b), g

    fwd_l, bwd_l, g_l = prep(cols_lat)
    fwd_c, bwd_c, g_c = prep(cols_ctx)
    state0 = jnp.zeros((g_l.shape[0], HGRN_HEADS, HGRN_DK, HGRN_DV), F32)
    yf_lat, yf_ctx = run_direction(hgrn2_scan, fwd_c, fwd_l, None, state0, False)
    yb_lat, yb_ctx = run_direction(hgrn2_scan, bwd_c, bwd_l, None, state0, True)
    y_lat = head_norm(yf_lat + yb_lat, norm_w, True) * jax.nn.silu(g_l)
    y_ctx = head_norm(yf_ctx + yb_ctx, norm_w, True) * jax.nn.silu(g_c)
    return y_lat, y_ctx


def hyena_filters(length, fw1, fb1, fw2, fb2, fw3, fb3, freq, wout):
    pos = jnp.arange(length, dtype=F32)
    t = (pos / max(length - 1, 1))[:, None]
    bands = (HYENA_EMB - 1) // 2
    w = 2.0 * math.pi * pos[:, None] / length * jnp.linspace(1e-4, bands - 1, bands)[None, :]
    z = jnp.concatenate([t, jnp.cos(w), jnp.sin(w)], -1)
    hid = jnp.sin(freq * (z @ fw1 + fb1))
    hid = jnp.sin(freq * (hid @ fw2 + fb2))
    hid = jnp.sin(freq * (hid @ fw3 + fb3))
    h = (hid @ wout).reshape(length, HYENA_ORDER, 2, HYENA_WIDTH)
    deltas = jnp.abs(jnp.linspace(math.log(HYENA_TARGET) / HYENA_SLOW_PCT, math.log(HYENA_TARGET) / HYENA_FAST_PCT, HYENA_WIDTH))
    decay = jnp.exp(-t * deltas[None, :])
    return (h * decay[:, None, None, :]).transpose(1, 2, 0, 3)


def bidir_fftconv(u, h_fwd, h_bwd, d_skip):
    length, width = h_fwd.shape
    kern = jnp.concatenate([h_fwd, jnp.zeros((1, width), F32), jnp.flip(h_bwd[1:], axis=0)], axis=0)
    spec = jnp.fft.rfft(u, n=2 * length, axis=1) * jnp.fft.rfft(kern, axis=0)[None]
    y = jnp.fft.irfft(spec, n=2 * length, axis=1)[:, :length]
    return y + u * d_skip


def hyena_group(u_lat, u_ctx, conv_w, conv_b, fw1, fb1, fw2, fb2, fw3, fb3, freq, wout, skip, norm_w):
    def run(u):
        length = u.shape[1]
        up = jnp.pad(u, ((0, 0), (1, 1), (0, 0)))
        u = up[:, :-2] * conv_w[0] + up[:, 1:-1] * conv_w[1] + up[:, 2:] * conv_w[2] + conv_b
        streams = jnp.split(u, HYENA_ORDER + 1, axis=-1)
        gates, z = streams[:-1], streams[-1]
        h = hyena_filters(length, fw1, fb1, fw2, fb2, fw3, fb3, freq, wout)
        for n in range(HYENA_ORDER):
            z = gates[n] * bidir_fftconv(z, h[n, 0], h[n, 1], skip[n])
        return group_rms(z, norm_w)

    return run(u_lat), run(u_ctx)


def retention_group(cols_lat, cols_ctx, log_decay, norm_w, row, col):
    def prep(cols, rotary):
        q, k, v, g = cols
        b, l, _ = q.shape
        q = q.reshape(b, l, RET_HEADS, RET_DQK)
        k = k.reshape(b, l, RET_HEADS, RET_DQK)
        if rotary:
            q = axial_rope(q, row, col)
            k = axial_rope(k, row, col)
        q = q.transpose(0, 2, 1, 3)
        k = k.transpose(0, 2, 1, 3) * RET_DQK ** -0.5
        return (q, k, to_heads(v, RET_HEADS)), g

    lat, g_l = prep(cols_lat, True)
    ctx, g_c = prep(cols_ctx, False)
    state0 = jnp.zeros((g_l.shape[0], RET_HEADS, RET_DQK, RET_DV), F32)
    yf_lat, yf_ctx = run_direction(retention_scan, ctx, lat, log_decay[0], state0, False)
    yb_lat, yb_ctx = run_direction(retention_scan, ctx, lat, log_decay[1], state0, True)
    y_lat = head_norm(yf_lat + yb_lat, norm_w, False) * jax.nn.silu(g_l)
    y_ctx = head_norm(yf_ctx + yb_ctx, norm_w, False) * jax.nn.silu(g_c)
    return y_lat, y_ctx


def even_mixer(u_lat, u_ctx, w_in, gate_b, mlstm_norm_w, lb, hgrn_norm_w, w_out):
    w_in = w_in.astype(F32)
    w_out = w_out.astype(F32)
    p_lat = split_cols(mm3(u_lat.astype(F32), w_in), EVEN_SPLITS)
    p_ctx = split_cols(mm3(u_ctx.astype(F32), w_in), EVEN_SPLITS)
    n_a = len(MLSTM_SPLITS)
    a_lat, a_ctx = mlstm_group(p_lat[:n_a], p_ctx[:n_a], gate_b.astype(F32), mlstm_norm_w.astype(F32))
    b_lat, b_ctx = hgrn_group(p_lat[n_a:], p_ctx[n_a:], lb, hgrn_norm_w.astype(F32))
    return mm3(jnp.concatenate([a_lat, b_lat], -1), w_out), mm3(jnp.concatenate([a_ctx, b_ctx], -1), w_out)


def odd_mixer(u_lat, u_ctx, w_in, hyena_p, log_decay, ret_norm_w, w_out, row, col):
    w_in = w_in.astype(F32)
    w_out = w_out.astype(F32)
    p_lat = split_cols(mm3(u_lat.astype(F32), w_in), ODD_SPLITS)
    p_ctx = split_cols(mm3(u_ctx.astype(F32), w_in), ODD_SPLITS)
    hy_lat, hy_ctx = hyena_group(p_lat[0], p_ctx[0], *[p.astype(F32) for p in hyena_p])
    r_lat, r_ctx = retention_group(p_lat[1:], p_ctx[1:], log_decay.astype(F32), ret_norm_w.astype(F32), row, col)
    return mm3(jnp.concatenate([hy_lat, r_lat], -1), w_out), mm3(jnp.concatenate([hy_ctx, r_ctx], -1), w_out)


ROUTE_TB = 512
DISPATCH_TB = 512
COMBINE_TB = 256
FFN_BM = 512


def _dot(a, b):
    return jnp.dot(a, b, preferred_element_type=F32)


def _route_kernel(v_ref, rw_ref, rb_ref, e_ref, rk_ref, g_ref, cnt_ref, carry_ref):
    tb, n_e = v_ref.shape[0], rw_ref.shape[1]

    @pl.when(pl.program_id(0) == 0)
    def _():
        carry_ref[...] = jnp.zeros_like(carry_ref)

    x = v_ref[...]
    w = rw_ref[...]
    xh = x.astype(BF16)
    xl = (x - xh.astype(F32)).astype(BF16)
    wh = w.astype(BF16)
    wl = (w - wh.astype(F32)).astype(BF16)
    logits = _dot(xh, wh) + (_dot(xh, wl) + _dot(xl, wh))
    scores = jax.nn.sigmoid(logits)
    work = scores + rb_ref[...]
    lane = lax.broadcasted_iota(jnp.int32, (tb, n_e), 1)
    col = lax.broadcasted_iota(jnp.int32, (tb, TOP_K), 1)
    e_out = jnp.zeros((tb, TOP_K), jnp.int32)
    g_out = jnp.zeros((tb, TOP_K), F32)
    mask = jnp.zeros((tb, n_e), F32)
    onehots = []
    for j in range(TOP_K):
        m = jnp.max(work, axis=1, keepdims=True)
        idx = jnp.min(jnp.where(work == m, lane, n_e), axis=1, keepdims=True)
        oh = lane == idx
        gj = jnp.sum(jnp.where(oh, scores, 0.0), axis=1, keepdims=True)
        e_out = jnp.where(col == j, idx, e_out)
        g_out = jnp.where(col == j, gj, g_out)
        work = jnp.where(oh, -jnp.inf, work)
        mask = mask + oh.astype(F32)
        onehots.append(oh)
    r_i = lax.broadcasted_iota(jnp.int32, (tb, tb), 0)
    c_i = lax.broadcasted_iota(jnp.int32, (tb, tb), 1)
    tri = (r_i > c_i).astype(BF16)
    rank = _dot(tri, mask.astype(BF16)) + carry_ref[...]
    rk_out = jnp.zeros((tb, TOP_K), jnp.int32)
    for j in range(TOP_K):
        rkj = jnp.sum(jnp.where(onehots[j], rank, 0.0), axis=1, keepdims=True)
        rk_out = jnp.where(col == j, rkj.astype(jnp.int32), rk_out)
    carry_ref[...] = carry_ref[...] + jnp.sum(mask, axis=0, keepdims=True)
    e_ref[...] = e_out
    rk_ref[...] = rk_out
    g_ref[...] = ROUTED_SCALE * g_out / jnp.sum(g_out, axis=1, keepdims=True)
    cnt_ref[...] = carry_ref[...]


def moe_route(v, router_w, router_bias):
    n_tok, d = v.shape
    tb = ROUTE_TB
    return pl.pallas_call(
        _route_kernel,
        grid=(n_tok // tb,),
        in_specs=[pl.BlockSpec((tb, d), lambda i: (i, 0)),
                  pl.BlockSpec((d, N_EXPERTS), lambda i: (0, 0)),
                  pl.BlockSpec((1, N_EXPERTS), lambda i: (0, 0))],
        out_specs=[pl.BlockSpec((tb, TOP_K), lambda i: (i, 0)),
                   pl.BlockSpec((tb, TOP_K), lambda i: (i, 0)),
                   pl.BlockSpec((tb, TOP_K), lambda i: (i, 0)),
                   pl.BlockSpec((1, N_EXPERTS), lambda i: (0, 0))],
        out_shape=[jax.ShapeDtypeStruct((n_tok, TOP_K), jnp.int32),
                   jax.ShapeDtypeStruct((n_tok, TOP_K), jnp.int32),
                   jax.ShapeDtypeStruct((n_tok, TOP_K), F32),
                   jax.ShapeDtypeStruct((1, N_EXPERTS), F32)],
        scratch_shapes=[pltpu.VMEM((1, N_EXPERTS), F32)],
        compiler_params=pltpu.CompilerParams(dimension_semantics=("arbitrary",)),
        name="moe_route",
    )(v, router_w, router_bias.reshape(1, N_EXPERTS))


def _dispatch_kernel(zs_ref, zf_ref, nu_ref, pos_hbm, v_ref, xs_hbm, pos_smem, zbuf, sem_idx, sem_sc, sem_z):
    i = pl.program_id(0)
    tb = v_ref.shape[0]
    bm = zbuf.shape[0]
    n_blocks = xs_hbm.shape[0] // bm
    idx_cp = pltpu.make_async_copy(pos_hbm.at[i], pos_smem, sem_idx)
    idx_cp.start()

    @pl.when(i == 0)
    def _():
        zbuf[...] = jnp.zeros_like(zbuf)
        for e in range(N_EXPERTS):
            @pl.when(zf_ref[e] != 0)
            def _():
                start = pl.multiple_of(zs_ref[e], bm)
                pltpu.make_async_copy(zbuf, xs_hbm.at[pl.ds(start, bm)], sem_z).start()
        for e in range(N_EXPERTS):
            @pl.when(zf_ref[e] != 0)
            def _():
                pltpu.make_async_copy(zbuf, xs_hbm.at[pl.ds(0, bm)], sem_z).wait()

        def tail(b, carry):
            cp = pltpu.make_async_copy(zbuf, xs_hbm.at[pl.ds(pl.multiple_of(b * bm, bm), bm)], sem_z)
            cp.start()
            cp.wait()
            return carry

        lax.fori_loop(nu_ref[0], n_blocks, tail, 0)

    idx_cp.wait()

    def body(t, carry):
        for k in range(TOP_K):
            p = pos_smem[0, t * TOP_K + k]
            pltpu.make_async_copy(v_ref.at[pl.ds(t, 1)], xs_hbm.at[pl.ds(p, 1)], sem_sc).start()
        return carry

    lax.fori_loop(0, tb, body, 0)
    for k in range(TOP_K):
        pltpu.make_async_copy(v_ref, v_ref, sem_sc).wait()


def moe_dispatch(v, pos3, zero_start, zero_flag, n_used, n_rows):
    n_tok, d = v.shape
    tb = DISPATCH_TB
    grid_spec = pltpu.PrefetchScalarGridSpec(
        num_scalar_prefetch=3,
        grid=(n_tok // tb,),
        in_specs=[pl.BlockSpec(memory_space=pl.ANY),
                  pl.BlockSpec((tb, d), lambda i, zs, zf, nu: (i, 0))],
        out_specs=pl.BlockSpec(memory_space=pl.ANY),
        scratch_shapes=[pltpu.SMEM((1, tb * TOP_K), jnp.int32),
                        pltpu.VMEM((FFN_BM, d), F32),
                        pltpu.SemaphoreType.DMA(()),
                        pltpu.SemaphoreType.DMA(()),
                        pltpu.SemaphoreType.DMA(())],
    )
    return pl.pallas_call(
        _dispatch_kernel,
        grid_spec=grid_spec,
        out_shape=jax.ShapeDtypeStruct((n_rows, d), F32),
        compiler_params=pltpu.CompilerParams(dimension_semantics=("arbitrary",)),
        name="moe_dispatch",
    )(zero_start, zero_flag, n_used, pos3, v)


def _ffn_kernel(be_ref, nu_ref, xs_ref, w1_ref, w3_ref, w2_ref, y_ref):
    used = pl.program_id(0) < nu_ref[0]

    @pl.when(used)
    def _():
        x = xs_ref[...].astype(BF16)
        h1 = _dot(x, w1_ref[0])
        h3 = _dot(x, w3_ref[0])
        h = (h1 * jax.nn.sigmoid(h1) * h3).astype(BF16)
        y_ref[...] = _dot(h, w2_ref[0])

    @pl.when(jnp.logical_not(used))
    def _():
        y_ref[...] = jnp.zeros_like(y_ref)


def moe_expert_ffn(xs, block_e, n_used, w1, w3, w2):
    n_rows, d = xs.shape
    bm = FFN_BM
    hid = w1.shape[2]

    def row_map(i, be, nu):
        return (jnp.minimum(i, nu[0] - 1), 0)

    def w_map(i, be, nu):
        return (be[jnp.minimum(i, nu[0] - 1)], 0, 0)

    grid_spec = pltpu.PrefetchScalarGridSpec(
        num_scalar_prefetch=2,
        grid=(n_rows // bm,),
        in_specs=[pl.BlockSpec((bm, d), row_map),
                  pl.BlockSpec((1, d, hid), w_map),
                  pl.BlockSpec((1, d, hid), w_map),
                  pl.BlockSpec((1, hid, d), w_map)],
        out_specs=pl.BlockSpec((bm, d), lambda i, be, nu: (i, 0)),
    )
    return pl.pallas_call(
        _ffn_kernel,
        grid_spec=grid_spec,
        out_shape=jax.ShapeDtypeStruct((n_rows, d), F32),
        compiler_params=pltpu.CompilerParams(dimension_semantics=("arbitrary",)),
        name="moe_expert_ffn",
    )(block_e, n_used, xs, w1, w3, w2)


def _combine_kernel(pos_hbm, y_hbm, v_ref, g_ref, sw1_ref, sw3_ref, sw2_ref, f_ref, pos_smem, ybuf, sem_idx, sem_y):
    i = pl.program_id(0)
    n = pl.num_programs(0)
    tb = v_ref.shape[0]
    slot = lax.rem(i, 2)

    def fetch_idx(step, s):
        return pltpu.make_async_copy(pos_hbm.at[step], pos_smem.at[s], sem_idx.at[s])

    def issue_gather(s):
        def body(t, carry):
            for k in range(TOP_K):
                p = pos_smem[s, 0, t * TOP_K + k]
                pltpu.make_async_copy(y_hbm.at[pl.ds(p, 1)], ybuf.at[s, k, pl.ds(t, 1)], sem_y.at[s]).start()
            return carry
        lax.fori_loop(0, tb, body, 0)

    @pl.when(i == 0)
    def _():
        cp = fetch_idx(0, 0)
        cp.start()
        cp.wait()
        issue_gather(0)

    @pl.when(i + 1 < n)
    def _():
        cp = fetch_idx(i + 1, 1 - slot)
        cp.start()
        cp.wait()
        issue_gather(1 - slot)

    x = v_ref[...].astype(BF16)
    h1 = _dot(x, sw1_ref[...])
    h3 = _dot(x, sw3_ref[...])
    acc = _dot((h1 * jax.nn.sigmoid(h1) * h3).astype(BF16), sw2_ref[...])
    for k in range(TOP_K):
        pltpu.make_async_copy(ybuf.at[slot, k], ybuf.at[slot, k], sem_y.at[slot]).wait()
    g = g_ref[...]
    for k in range(TOP_K):
        acc = acc + g[:, k:k + 1] * ybuf[slot, k]
    f_ref[...] = acc


def moe_combine(pos3, y, v, gate, sw1, sw3, sw2):
    n_tok, d = v.shape
    tb = COMBINE_TB
    hid = sw1.shape[1]
    return pl.pallas_call(
        _combine_kernel,
        grid=(n_tok // tb,),
        in_specs=[pl.BlockSpec(memory_space=pl.ANY),
                  pl.BlockSpec(memory_space=pl.ANY),
                  pl.BlockSpec((tb, d), lambda i: (i, 0)),
                  pl.BlockSpec((tb, TOP_K), lambda i: (i, 0)),
                  pl.BlockSpec((d, hid), lambda i: (0, 0)),
                  pl.BlockSpec((d, hid), lambda i: (0, 0)),
                  pl.BlockSpec((hid, d), lambda i: (0, 0))],
        out_specs=pl.BlockSpec((tb, d), lambda i: (i, 0)),
        out_shape=jax.ShapeDtypeStruct((n_tok, d), F32),
        scratch_shapes=[pltpu.SMEM((2, 1, tb * TOP_K), jnp.int32),
                        pltpu.VMEM((2, TOP_K, tb, d), F32),
                        pltpu.SemaphoreType.DMA((2,)),
                        pltpu.SemaphoreType.DMA((2,))],
        compiler_params=pltpu.CompilerParams(dimension_semantics=("arbitrary",), vmem_limit_bytes=48 * 1024 * 1024),
        name="moe_combine",
    )(pos3, y, v, gate, sw1, sw3, sw2)


def moe_ffn(t, router_w, router_bias, w1, w3, w2, sw1, sw3, sw2):
    n_tok, d = t.shape
    tf = t.astype(F32)
    bm = FFN_BM
    e_sel, rank, gate, cnt = moe_route(tf, router_w.astype(F32), router_bias.astype(F32))
    counts = cnt[0].astype(jnp.int32)
    padded = (counts + bm - 1) // bm * bm
    pend = jnp.cumsum(padded)
    pstart = pend - padded
    n_blocks = -(-(n_tok * TOP_K + N_EXPERTS * (bm - 1)) // bm)
    expert_ids = jnp.arange(N_EXPERTS, dtype=jnp.int32)
    pos = rank + jnp.sum(jnp.where(e_sel[..., None] == expert_ids, pstart, 0), -1)
    block_e = jnp.minimum(jnp.searchsorted(pend, jnp.arange(n_blocks, dtype=jnp.int32) * bm, side='right'), N_EXPERTS - 1).astype(jnp.int32)
    n_used = (pend[-1:] // bm).astype(jnp.int32)
    zero_start = jnp.maximum(pend - bm, 0).astype(jnp.int32)
    zero_flag = (counts > 0).astype(jnp.int32)
    xs = moe_dispatch(tf, pos.reshape(n_tok // DISPATCH_TB, 1, DISPATCH_TB * TOP_K), zero_start, zero_flag, n_used, n_blocks * bm)
    y = moe_expert_ffn(xs, block_e, n_used, w1.astype(BF16), w3.astype(BF16), w2.astype(BF16))
    return moe_combine(pos.reshape(n_tok // COMBINE_TB, 1, COMBINE_TB * TOP_K), y, tf, gate,
                       sw1.astype(BF16), sw3.astype(BF16), sw2.astype(BF16))


def kernel(x, c, ctx, c_ctx, ada_w, ada_b, ln_w, ln_b, even_w_in, mlstm_gate_b, mlstm_norm_w, hgrn_lb, hgrn_norm_w, even_w_out, odd_w_in, hy_conv_w, hy_conv_b, hy_f_w1, hy_f_b1, hy_f_w2, hy_f_b2, hy_f_w3, hy_f_b3, hy_f_freq, hy_f_wout, hy_skip, hy_norm_w, ret_log_decay, ret_norm_w, odd_w_out, router_w, router_bias, exp_w1, exp_w3, exp_w2, sh_w1, sh_w3, sh_w2):
    batch, n_lat, d = x.shape
    n_ctx = ctx.shape[1]
    rows = n_lat // GRID_W
    row = jnp.repeat(jnp.arange(rows, dtype=F32), GRID_W)
    col = jnp.tile(jnp.arange(GRID_W, dtype=F32), rows)
    sm = jax.nn.softmax(hgrn_lb.astype(F32), axis=0)
    lower_bounds = jnp.cumsum(sm, axis=0) - sm[0]
    c_act = jax.nn.silu(c.astype(F32))
    cc_act = jax.nn.silu(c_ctx.astype(F32))
    h_lat, h_ctx = x, ctx
    for l in range(DEPTH):
        last = l == DEPTH - 1
        ada_wl = ada_w[l].astype(F32)
        mod_lat = jnp.split((c_act @ ada_wl + ada_b[l])[:, None, :], 6, axis=-1)
        mod_ctx = jnp.split((cc_act @ ada_wl + ada_b[l])[None, None, :], 6, axis=-1)
        u_lat = h_lat * (1.0 + mod_lat[1]) + mod_lat[0]
        u_ctx = h_ctx * (1.0 + mod_ctx[1]) + mod_ctx[0]
        if l % 2 == 0:
            e = l // 2
            y_lat, y_ctx = even_mixer(u_lat, u_ctx, even_w_in[e], mlstm_gate_b[e], mlstm_norm_w[e], lower_bounds[e], hgrn_norm_w[e], even_w_out[e])
        else:
            o = l // 2
            hyena_p = (hy_conv_w[o], hy_conv_b[o], hy_f_w1[o], hy_f_b1[o], hy_f_w2[o], hy_f_b2[o], hy_f_w3[o], hy_f_b3[o], hy_f_freq[o], hy_f_wout[o], hy_skip[o], hy_norm_w[o])
            y_lat, y_ctx = odd_mixer(u_lat, u_ctx, odd_w_in[o], hyena_p, ret_log_decay[o], ret_norm_w[o], odd_w_out[o], row, col)
        h_lat = layer_norm(DEEPNORM_ALPHA * h_lat + (mod_lat[2] * y_lat).astype(h_lat.dtype), ln_w[l, 0], ln_b[l, 0])
        v_lat = (h_lat * (1.0 + mod_lat[4]) + mod_lat[3]).reshape(batch * n_lat, d)
        if last:
            f_lat = moe_ffn(v_lat, router_w[l], router_bias[l], exp_w1[l], exp_w3[l], exp_w2[l], sh_w1[l], sh_w3[l], sh_w2[l])
        else:
            h_ctx = layer_norm(DEEPNORM_ALPHA * h_ctx + (mod_ctx[2] * y_ctx).astype(h_ctx.dtype), ln_w[l, 0], ln_b[l, 0])
            v_ctx = (h_ctx * (1.0 + mod_ctx[4]) + mod_ctx[3]).reshape(batch * n_ctx, d)
            f_all = moe_ffn(jnp.concatenate([v_lat, v_ctx], 0), router_w[l], router_bias[l], exp_w1[l], exp_w3[l], exp_w2[l], sh_w1[l], sh_w3[l], sh_w2[l])
            f_lat = f_all[:batch * n_lat]
            f_ctx = f_all[batch * n_lat:].reshape(batch, n_ctx, d)
            h_ctx = layer_norm(DEEPNORM_ALPHA * h_ctx + (mod_ctx[5] * f_ctx).astype(h_ctx.dtype), ln_w[l, 1], ln_b[l, 1])
        h_lat = layer_norm(DEEPNORM_ALPHA * h_lat + (mod_lat[5] * f_lat.reshape(batch, n_lat, d)).astype(h_lat.dtype), ln_w[l, 1], ln_b[l, 1])
    return h_lat
```

```python
import functools
import math
import jax
import jax.numpy as jnp
from jax import lax
from jax.experimental import pallas as pl
from jax.experimental.pallas import tpu as pltpu

D_MODEL = 1024
BATCH = 8
SEQ = 4096
DEPTH = 4

GRID_W = 64
CTX_LEN = 256
N_EVEN = (DEPTH + 1) // 2
N_ODD = DEPTH // 2

MLSTM_HEADS = 4
MLSTM_DQK = 64
MLSTM_DV = 128
MLSTM_CHUNK = 64
HGRN_HEADS = 4
HGRN_DK = 128
HGRN_DV = 128
HGRN_CHUNK = 32
HYENA_WIDTH = 512
HYENA_GROUPS = 8
HYENA_ORDER = 2
HYENA_EMB = 33
HYENA_FILTER_HIDDEN = 64
HYENA_SHORT = 3
HYENA_TARGET = 1e-2
HYENA_FAST_PCT = 0.3
HYENA_SLOW_PCT = 1.5
RET_HEADS = 4
RET_DQK = 128
RET_DV = 128
RET_CHUNK = 64
ROPE_BASE = 10000.0
N_EXPERTS = 64
TOP_K = 8
EXPERT_HIDDEN = 256
SHARED_HIDDEN = 256
ROUTED_SCALE = 2.5
MOE_BLOCK = 128
DEEPNORM_ALPHA = (2 * DEPTH) ** 0.25
DEEPNORM_BETA = (8 * DEPTH) ** -0.25
EPS = 1e-5
NEG_BIG = -1e30

MLSTM_SPLITS = (MLSTM_HEADS * MLSTM_DQK, MLSTM_HEADS * MLSTM_DQK, MLSTM_HEADS * MLSTM_DV, MLSTM_HEADS * MLSTM_DV, 4 * MLSTM_HEADS)
HGRN_SPLITS = (HGRN_HEADS * HGRN_DK, HGRN_HEADS * HGRN_DV, HGRN_HEADS * HGRN_DK, HGRN_HEADS * HGRN_DK, HGRN_HEADS * HGRN_DV)
EVEN_SPLITS = MLSTM_SPLITS + HGRN_SPLITS
EVEN_PROJ = sum(EVEN_SPLITS)
EVEN_MIX = MLSTM_HEADS * MLSTM_DV + HGRN_HEADS * HGRN_DV
RET_SPLITS = (RET_HEADS * RET_DQK, RET_HEADS * RET_DQK, RET_HEADS * RET_DV, RET_HEADS * RET_DV)
ODD_SPLITS = ((HYENA_ORDER + 1) * HYENA_WIDTH,) + RET_SPLITS
ODD_PROJ = sum(ODD_SPLITS)
ODD_MIX = HYENA_WIDTH + RET_HEADS * RET_DV

F32 = jnp.float32
BF16 = jnp.bfloat16


def _mm_kernel(x_ref, w_ref, o_ref):
    o_ref[...] = jnp.dot(x_ref[...].astype(BF16), w_ref[...].astype(BF16), preferred_element_type=F32)


def pallas_matmul(x, w, tm=512, tn=512):
    t, k = x.shape
    n = w.shape[1]
    n_pad = -(-n // tn) * tn
    if n_pad != n:
        w = jnp.pad(w, ((0, 0), (0, n_pad - n)))
    out = pl.pallas_call(
        _mm_kernel,
        grid=(t // tm, n_pad // tn),
        in_specs=[pl.BlockSpec((tm, k), lambda i, j: (i, 0)), pl.BlockSpec((k, tn), lambda i, j: (0, j))],
        out_specs=pl.BlockSpec((tm, tn), lambda i, j: (i, j)),
        out_shape=jax.ShapeDtypeStruct((t, n_pad), F32),
    )(x, w)
    return out[:, :n]


def mm3(x, w):
    b, l, d = x.shape
    return pallas_matmul(x.reshape(b * l, d), w).reshape(b, l, -1)


def split_cols(a, sizes):
    out, start = [], 0
    for s in sizes:
        out.append(a[..., start:start + s])
        start += s
    return out


def to_heads(a, n_heads):
    b, l, w = a.shape
    return a.reshape(b, l, n_heads, w // n_heads).transpose(0, 2, 1, 3)


def chunk(a, cs):
    b, h, l = a.shape[:3]
    return a.reshape((b, h, l // cs, cs) + a.shape[3:])


def layer_norm(x, w, b):
    xf = x.astype(F32)
    mu = jnp.mean(xf, -1, keepdims=True)
    var = jnp.mean(jnp.square(xf - mu), -1, keepdims=True)
    return ((xf - mu) * lax.rsqrt(var + EPS) * w + b).astype(x.dtype)


def head_norm(h, w, rms):
    h = h.transpose(0, 2, 1, 3)
    if not rms:
        h = h - jnp.mean(h, -1, keepdims=True)
    h = h * lax.rsqrt(jnp.mean(jnp.square(h), -1, keepdims=True) + EPS)
    b, l, nh, dh = h.shape
    return h.reshape(b, l, nh * dh) * w


def group_rms(z, w):
    b, l, wd = z.shape
    zg = z.reshape(b, l, HYENA_GROUPS, wd // HYENA_GROUPS)
    zg = zg * lax.rsqrt(jnp.mean(jnp.square(zg), -1, keepdims=True) + EPS)
    return zg.reshape(b, l, wd) * w


def run_direction(scan_fn, ctx_seq, lat_seq, const, state0, reverse):
    if reverse:
        ctx_seq = tuple(jnp.flip(a, axis=2) for a in ctx_seq)
        lat_seq = tuple(jnp.flip(a, axis=2) for a in lat_seq)
    y_ctx, state = scan_fn(ctx_seq, const, state0)
    y_lat, _ = scan_fn(lat_seq, const, state)
    if reverse:
        y_ctx = jnp.flip(y_ctx, axis=2)
        y_lat = jnp.flip(y_lat, axis=2)
    return y_lat, y_ctx


def mlstm_scan(seq, const, state):
    q, k, v, ig, lf = seq
    cs = MLSTM_CHUNK
    qc, kc, vc, igc, lfc = (chunk(a, cs) for a in seq)
    b = jnp.cumsum(lfc, axis=-1)
    g = b[..., -1]
    a = g[..., None] - b + igc
    m_loc = jnp.max(a, axis=-1)
    w = jnp.exp(a - m_loc[..., None])
    c_loc = jnp.einsum('bhcs,bhcsv,bhcsk->bhcvk', w, vc, kc)
    n_loc = jnp.einsum('bhcs,bhcsk->bhck', w, kc)

    def step(carry, inp):
        c_st, n_st, m_st = carry
        g_j, ml_j, cl_j, nl_j = inp
        m_new = jnp.maximum(g_j + m_st, ml_j)
        sp = jnp.exp(g_j + m_st - m_new)
        sl = jnp.exp(ml_j - m_new)
        new = (sp[..., None, None] * c_st + sl[..., None, None] * cl_j, sp[..., None] * n_st + sl[..., None] * nl_j, m_new)
        return new, carry

    final, (c_in, n_in, m_in) = lax.scan(step, state, tuple(jnp.moveaxis(t, 2, 0) for t in (g, m_loc, c_loc, n_loc)))
    c_in, n_in, m_in = jnp.moveaxis(c_in, 0, 2), jnp.moveaxis(n_in, 0, 2), jnp.moveaxis(m_in, 0, 2)
    tri = jnp.tril(jnp.ones((cs, cs), bool))
    log_d = jnp.where(tri, b[..., :, None] - b[..., None, :] + igc[..., None, :], -jnp.inf)
    inter = b + m_in[..., None]
    m_out = jnp.maximum(inter, jnp.max(log_d, -1))
    s = jnp.einsum('bhctk,bhcsk->bhcts', qc, kc) * jnp.exp(log_d - m_out[..., None])
    e_int = jnp.exp(inter - m_out)
    num = jnp.einsum('bhcts,bhcsv->bhctv', s, vc) + e_int[..., None] * jnp.einsum('bhctk,bhcvk->bhctv', qc, c_in)
    den = jnp.sum(s, -1) + e_int * jnp.einsum('bhctk,bhck->bhct', qc, n_in)
    h = num / jnp.maximum(jnp.abs(den), jnp.exp(-m_out))[..., None]
    return h.reshape(v.shape), final


def hgrn2_scan(seq, const, state):
    q, k, v, lf = seq
    cs = HGRN_CHUNK
    qc, kc, vc, lfc = (chunk(a, cs) for a in seq)
    a_cum = jnp.cumsum(lfc, axis=3)
    a_end = a_cum[:, :, :, -1]
    s_loc = jnp.einsum('bhcsk,bhcsv->bhckv', kc * jnp.exp(a_end[:, :, :, None] - a_cum), vc)

    def step(s_st, inp):
        a_j, sl_j = inp
        return jnp.exp(a_j)[..., None] * s_st + sl_j, s_st

    final, s_in = lax.scan(step, state, (jnp.moveaxis(a_end, 2, 0), jnp.moveaxis(s_loc, 2, 0)))
    s_in = jnp.moveaxis(s_in, 0, 2)
    inter = jnp.einsum('bhctk,bhckv->bhctv', qc * jnp.exp(a_cum), s_in)
    a_mid = a_cum[:, :, :, cs // 2 - 1:cs // 2]
    tri = jnp.tril(jnp.ones((cs, cs), bool))
    sc = jnp.einsum('bhctk,bhcsk->bhcts', qc * jnp.exp(a_cum - a_mid), kc * jnp.exp(a_mid - a_cum))
    sc = jnp.where(tri, sc, 0.0)
    out = inter + jnp.einsum('bhcts,bhcsv->bhctv', sc, vc)
    return out.reshape(v.shape), final


def retention_scan(seq, log_gamma, state):
    q, k, v = seq
    cs = RET_CHUNK
    qc, kc, vc = (chunk(a, cs) for a in seq)
    pos = jnp.arange(cs, dtype=F32)
    tri = jnp.tril(jnp.ones((cs, cs), bool))
    delta = pos[:, None] - pos[None, :]
    decay = jnp.exp(jnp.where(tri, delta[None] * log_gamma[:, None, None], -jnp.inf))
    sc = jnp.einsum('bhctk,bhcsk->bhcts', qc, kc) * decay[:, None]
    intra = jnp.einsum('bhcts,bhcsv->bhctv', sc, vc)
    q_dec = jnp.exp((pos + 1.0)[None, :] * log_gamma[:, None])
    k_dec = jnp.exp((cs - 1.0 - pos)[None, :] * log_gamma[:, None])
    s_loc = jnp.einsum('bhcsk,hs,bhcsv->bhckv', kc, k_dec, vc)
    chunk_decay = jnp.exp(cs * log_gamma)[None, :, None, None]

    def step(s_st, sl_j):
        return chunk_decay * s_st + sl_j, s_st

    final, s_in = lax.scan(step, state, jnp.moveaxis(s_loc, 2, 0))
    s_in = jnp.moveaxis(s_in, 0, 2)
    inter = jnp.einsum('bhctk,ht,bhckv->bhctv', qc, q_dec, s_in)
    return (inter + intra).reshape(v.shape), final


def rope_1d(x, pos):
    d = x.shape[-1]
    inv = ROPE_BASE ** (-jnp.arange(0, d, 2, dtype=F32) / d)
    ang = pos[:, None] * inv[None, :]
    cos = jnp.cos(ang)[None, :, None, :]
    sin = jnp.sin(ang)[None, :, None, :]
    x1, x2 = x[..., :d // 2], x[..., d // 2:]
    return jnp.concatenate([x1 * cos - x2 * sin, x1 * sin + x2 * cos], -1)


def axial_rope(x, row, col):
    half = x.shape[-1] // 2
    return jnp.concatenate([rope_1d(x[..., :half], row), rope_1d(x[..., half:], col)], -1)


def mlstm_group(cols_lat, cols_ctx, gate_b, norm_w):
    def prep(cols):
        q, k, v, o, gates = cols
        b, l, _ = q.shape
        q = to_heads(q, MLSTM_HEADS) * MLSTM_DQK ** -0.5
        k = to_heads(k, MLSTM_HEADS)
        v = to_heads(v, MLSTM_HEADS)
        gates = (gates.reshape(b, l, 4, MLSTM_HEADS) + gate_b).transpose(2, 0, 3, 1)
        fwd = (q, k, v, gates[0], jax.nn.log_sigmoid(gates[1]))
        bwd = (q, k, v, gates[2], jax.nn.log_sigmoid(gates[3]))
        return fwd, bwd, o

    fwd_l, bwd_l, o_l = prep(cols_lat)
    fwd_c, bwd_c, o_c = prep(cols_ctx)
    batch = o_l.shape[0]
    state0 = (jnp.zeros((batch, MLSTM_HEADS, MLSTM_DV, MLSTM_DQK), F32),
              jnp.zeros((batch, MLSTM_HEADS, MLSTM_DQK), F32),
              jnp.full((batch, MLSTM_HEADS), NEG_BIG, F32))
    hf_lat, hf_ctx = run_direction(mlstm_scan, fwd_c, fwd_l, None, state0, False)
    hb_lat, hb_ctx = run_direction(mlstm_scan, bwd_c, bwd_l, None, state0, True)
    y_lat = head_norm(hf_lat + hb_lat, norm_w, False) * jax.nn.sigmoid(o_l)
    y_ctx = head_norm(hf_ctx + hb_ctx, norm_w, False) * jax.nn.sigmoid(o_c)
    return y_lat, y_ctx


def hgrn_group(cols_lat, cols_ctx, lb, norm_w):
    def prep(cols):
        q, i, f_fwd, f_bwd, g = cols
        q = to_heads(jax.nn.silu(q), HGRN_HEADS)
        v = to_heads(i, HGRN_HEADS)

        def gate(fp):
            f = lb + (1.0 - lb) * jax.nn.sigmoid(fp)
            return to_heads(1.0 - f, HGRN_HEADS), to_heads(jnp.log(f), HGRN_HEADS)

        k_f, lf_f = gate(f_fwd)
        k_b, lf_b = gate(f_bwd)
        return (q, k_f, v, lf_f), (q, k_b, v, lf_b), g

    fwd_l, bwd_l, g_l = prep(cols_lat)
    fwd_c, bwd_c, g_c = prep(cols_ctx)
    state0 = jnp.zeros((g_l.shape[0], HGRN_HEADS, HGRN_DK, HGRN_DV), F32)
    yf_lat, yf_ctx = run_direction(hgrn2_scan, fwd_c, fwd_l, None, state0, False)
    yb_lat, yb_ctx = run_direction(hgrn2_scan, bwd_c, bwd_l, None, state0, True)
    y_lat = head_norm(yf_lat + yb_lat, norm_w, True) * jax.nn.silu(g_l)
    y_ctx = head_norm(yf_ctx + yb_ctx, norm_w, True) * jax.nn.silu(g_c)
    return y_lat, y_ctx


def hyena_filters(length, fw1, fb1, fw2, fb2, fw3, fb3, freq, wout):
    pos = jnp.arange(length, dtype=F32)
    t = (pos / max(length - 1, 1))[:, None]
    bands = (HYENA_EMB - 1) // 2
    w = 2.0 * math.pi * pos[:, None] / length * jnp.linspace(1e-4, bands - 1, bands)[None, :]
    z = jnp.concatenate([t, jnp.cos(w), jnp.sin(w)], -1)
    hid = jnp.sin(freq * (z @ fw1 + fb1))
    hid = jnp.sin(freq * (hid @ fw2 + fb2))
    hid = jnp.sin(freq * (hid @ fw3 + fb3))
    h = (hid @ wout).reshape(length, HYENA_ORDER, 2, HYENA_WIDTH)
    deltas = jnp.abs(jnp.linspace(math.log(HYENA_TARGET) / HYENA_SLOW_PCT, math.log(HYENA_TARGET) / HYENA_FAST_PCT, HYENA_WIDTH))
    decay = jnp.exp(-t * deltas[None, :])
    return (h * decay[:, None, None, :]).transpose(1, 2, 0, 3)


def bidir_fftconv(u, h_fwd, h_bwd, d_skip):
    length, width = h_fwd.shape
    kern = jnp.concatenate([h_fwd, jnp.zeros((1, width), F32), jnp.flip(h_bwd[1:], axis=0)], axis=0)
    spec = jnp.fft.rfft(u, n=2 * length, axis=1) * jnp.fft.rfft(kern, axis=0)[None]
    y = jnp.fft.irfft(spec, n=2 * length, axis=1)[:, :length]
    return y + u * d_skip


def hyena_group(u_lat, u_ctx, conv_w, conv_b, fw1, fb1, fw2, fb2, fw3, fb3, freq, wout, skip, norm_w):
    def run(u):
        length = u.shape[1]
        up = jnp.pad(u, ((0, 0), (1, 1), (0, 0)))
        u = up[:, :-2] * conv_w[0] + up[:, 1:-1] * conv_w[1] + up[:, 2:] * conv_w[2] + conv_b
        streams = jnp.split(u, HYENA_ORDER + 1, axis=-1)
        gates, z = streams[:-1], streams[-1]
        h = hyena_filters(length, fw1, fb1, fw2, fb2, fw3, fb3, freq, wout)
        for n in range(HYENA_ORDER):
            z = gates[n] * bidir_fftconv(z, h[n, 0], h[n, 1], skip[n])
        return group_rms(z, norm_w)

    return run(u_lat), run(u_ctx)


def retention_group(cols_lat, cols_ctx, log_decay, norm_w, row, col):
    def prep(cols, rotary):
        q, k, v, g = cols
        b, l, _ = q.shape
        q = q.reshape(b, l, RET_HEADS, RET_DQK)
        k = k.reshape(b, l, RET_HEADS, RET_DQK)
        if rotary:
            q = axial_rope(q, row, col)
            k = axial_rope(k, row, col)
        q = q.transpose(0, 2, 1, 3)
        k = k.transpose(0, 2, 1, 3) * RET_DQK ** -0.5
        return (q, k, to_heads(v, RET_HEADS)), g

    lat, g_l = prep(cols_lat, True)
    ctx, g_c = prep(cols_ctx, False)
    state0 = jnp.zeros((g_l.shape[0], RET_HEADS, RET_DQK, RET_DV), F32)
    yf_lat, yf_ctx = run_direction(retention_scan, ctx, lat, log_decay[0], state0, False)
    yb_lat, yb_ctx = run_direction(retention_scan, ctx, lat, log_decay[1], state0, True)
    y_lat = head_norm(yf_lat + yb_lat, norm_w, False) * jax.nn.silu(g_l)
    y_ctx = head_norm(yf_ctx + yb_ctx, norm_w, False) * jax.nn.silu(g_c)
    return y_lat, y_ctx


def even_mixer(u_lat, u_ctx, w_in, gate_b, mlstm_norm_w, lb, hgrn_norm_w, w_out):
    w_in = w_in.astype(F32)
    w_out = w_out.astype(F32)
    p_lat = split_cols(mm3(u_lat.astype(F32), w_in), EVEN_SPLITS)
    p_ctx = split_cols(mm3(u_ctx.astype(F32), w_in), EVEN_SPLITS)
    n_a = len(MLSTM_SPLITS)
    a_lat, a_ctx = mlstm_group(p_lat[:n_a], p_ctx[:n_a], gate_b.astype(F32), mlstm_norm_w.astype(F32))
    b_lat, b_ctx = hgrn_group(p_lat[n_a:], p_ctx[n_a:], lb, hgrn_norm_w.astype(F32))
    return mm3(jnp.concatenate([a_lat, b_lat], -1), w_out), mm3(jnp.concatenate([a_ctx, b_ctx], -1), w_out)


def odd_mixer(u_lat, u_ctx, w_in, hyena_p, log_decay, ret_norm_w, w_out, row, col):
    w_in = w_in.astype(F32)
    w_out = w_out.astype(F32)
    p_lat = split_cols(mm3(u_lat.astype(F32), w_in), ODD_SPLITS)
    p_ctx = split_cols(mm3(u_ctx.astype(F32), w_in), ODD_SPLITS)
    hy_lat, hy_ctx = hyena_group(p_lat[0], p_ctx[0], *[p.astype(F32) for p in hyena_p])
    r_lat, r_ctx = retention_group(p_lat[1:], p_ctx[1:], log_decay.astype(F32), ret_norm_w.astype(F32), row, col)
    return mm3(jnp.concatenate([hy_lat, r_lat], -1), w_out), mm3(jnp.concatenate([hy_ctx, r_ctx], -1), w_out)


ROUTE_TB = 512
DISPATCH_TB = 512
COMBINE_TB = 256
FFN_BM = 512


def _dot(a, b):
    return jnp.dot(a, b, preferred_element_type=F32)


def _route_kernel(v_ref, rw_ref, rb_ref, e_ref, rk_ref, g_ref, cnt_ref, carry_ref):
    tb, n_e = v_ref.shape[0], rw_ref.shape[1]

    @pl.when(pl.program_id(0) == 0)
    def _():
        carry_ref[...] = jnp.zeros_like(carry_ref)

    x = v_ref[...]
    w = rw_ref[...]
    xh = x.astype(BF16)
    xl = (x - xh.astype(F32)).astype(BF16)
    wh = w.astype(BF16)
    wl = (w - wh.astype(F32)).astype(BF16)
    logits = _dot(xh, wh) + (_dot(xh, wl) + _dot(xl, wh))
    scores = jax.nn.sigmoid(logits)
    work = scores + rb_ref[...]
    lane = lax.broadcasted_iota(jnp.int32, (tb, n_e), 1)
    col = lax.broadcasted_iota(jnp.int32, (tb, TOP_K), 1)
    e_out = jnp.zeros((tb, TOP_K), jnp.int32)
    g_out = jnp.zeros((tb, TOP_K), F32)
    mask = jnp.zeros((tb, n_e), F32)
    onehots = []
    for j in range(TOP_K):
        m = jnp.max(work, axis=1, keepdims=True)
        idx = jnp.min(jnp.where(work == m, lane, n_e), axis=1, keepdims=True)
        oh = lane == idx
        gj = jnp.sum(jnp.where(oh, scores, 0.0), axis=1, keepdims=True)
        e_out = jnp.where(col == j, idx, e_out)
        g_out = jnp.where(col == j, gj, g_out)
        work = jnp.where(oh, -jnp.inf, work)
        mask = mask + oh.astype(F32)
        onehots.append(oh)
    r_i = lax.broadcasted_iota(jnp.int32, (tb, tb), 0)
    c_i = lax.broadcasted_iota(jnp.int32, (tb, tb), 1)
    tri = (r_i > c_i).astype(BF16)
    rank = _dot(tri, mask.astype(BF16)) + carry_ref[...]
    rk_out = jnp.zeros((tb, TOP_K), jnp.int32)
    for j in range(TOP_K):
        rkj = jnp.sum(jnp.where(onehots[j], rank, 0.0), axis=1, keepdims=True)
        rk_out = jnp.where(col == j, rkj.astype(jnp.int32), rk_out)
    carry_ref[...] = carry_ref[...] + jnp.sum(mask, axis=0, keepdims=True)
    e_ref[...] = e_out
    rk_ref[...] = rk_out
    g_ref[...] = ROUTED_SCALE * g_out / jnp.sum(g_out, axis=1, keepdims=True)
    cnt_ref[...] = carry_ref[...]


def moe_route(v, router_w, router_bias):
    n_tok, d = v.shape
    tb = ROUTE_TB
    return pl.pallas_call(
        _route_kernel,
        grid=(n_tok // tb,),
        in_specs=[pl.BlockSpec((tb, d), lambda i: (i, 0)),
                  pl.BlockSpec((d, N_EXPERTS), lambda i: (0, 0)),
                  pl.BlockSpec((1, N_EXPERTS), lambda i: (0, 0))],
        out_specs=[pl.BlockSpec((tb, TOP_K), lambda i: (i, 0)),
                   pl.BlockSpec((tb, TOP_K), lambda i: (i, 0)),
                   pl.BlockSpec((tb, TOP_K), lambda i: (i, 0)),
                   pl.BlockSpec((1, N_EXPERTS), lambda i: (0, 0))],
        out_shape=[jax.ShapeDtypeStruct((n_tok, TOP_K), jnp.int32),
                   jax.ShapeDtypeStruct((n_tok, TOP_K), jnp.int32),
                   jax.ShapeDtypeStruct((n_tok, TOP_K), F32),
                   jax.ShapeDtypeStruct((1, N_EXPERTS), F32)],
        scratch_shapes=[pltpu.VMEM((1, N_EXPERTS), F32)],
        compiler_params=pltpu.CompilerParams(dimension_semantics=("arbitrary",)),
        name="moe_route",
    )(v, router_w, router_bias.reshape(1, N_EXPERTS))


def _dispatch_kernel(zs_ref, zf_ref, nu_ref, pos_hbm, v_ref, xs_hbm, pos_smem, zbuf, sem_idx, sem_sc, sem_z):
    i = pl.program_id(0)
    tb = v_ref.shape[0]
    bm = zbuf.shape[0]
    n_blocks = xs_hbm.shape[0] // bm
    idx_cp = pltpu.make_async_copy(pos_hbm.at[i], pos_smem, sem_idx)
    idx_cp.start()

    @pl.when(i == 0)
    def _():
        zbuf[...] = jnp.zeros_like(zbuf)
        for e in range(N_EXPERTS):
            @pl.when(zf_ref[e] != 0)
            def _():
                start = pl.multiple_of(zs_ref[e], bm)
                pltpu.make_async_copy(zbuf, xs_hbm.at[pl.ds(start, bm)], sem_z).start()
        for e in range(N_EXPERTS):
            @pl.when(zf_ref[e] != 0)
            def _():
                pltpu.make_async_copy(zbuf, xs_hbm.at[pl.ds(0, bm)], sem_z).wait()

        def tail(b, carry):
            cp = pltpu.make_async_copy(zbuf, xs_hbm.at[pl.ds(pl.multiple_of(b * bm, bm), bm)], sem_z)
            cp.start()
            cp.wait()
            return carry

        lax.fori_loop(nu_ref[0], n_blocks, tail, 0)

    idx_cp.wait()

    def body(t, carry):
        for k in range(TOP_K):
            p = pos_smem[0, t * TOP_K + k]
            pltpu.make_async_copy(v_ref.at[pl.ds(t, 1)], xs_hbm.at[pl.ds(p, 1)], sem_sc).start()
        return carry

    lax.fori_loop(0, tb, body, 0)
    for k in range(TOP_K):
        pltpu.make_async_copy(v_ref, v_ref, sem_sc).wait()


def moe_dispatch(v, pos3, zero_start, zero_flag, n_used, n_rows):
    n_tok, d = v.shape
    tb = DISPATCH_TB
    grid_spec = pltpu.PrefetchScalarGridSpec(
        num_scalar_prefetch=3,
        grid=(n_tok // tb,),
        in_specs=[pl.BlockSpec(memory_space=pl.ANY),
                  pl.BlockSpec((tb, d), lambda i, zs, zf, nu: (i, 0))],
        out_specs=pl.BlockSpec(memory_space=pl.ANY),
        scratch_shapes=[pltpu.SMEM((1, tb * TOP_K), jnp.int32),
                        pltpu.VMEM((FFN_BM, d), F32),
                        pltpu.SemaphoreType.DMA(()),
                        pltpu.SemaphoreType.DMA(()),
                        pltpu.SemaphoreType.DMA(())],
    )
    return pl.pallas_call(
        _dispatch_kernel,
        grid_spec=grid_spec,
        out_shape=jax.ShapeDtypeStruct((n_rows, d), F32),
        compiler_params=pltpu.CompilerParams(dimension_semantics=("arbitrary",)),
        name="moe_dispatch",
    )(zero_start, zero_flag, n_used, pos3, v)


def _ffn_kernel(be_ref, nu_ref, xs_ref, w1_ref, w3_ref, w2_ref, y_ref):
    used = pl.program_id(0) < nu_ref[0]

    @pl.when(used)
    def _():
        x = xs_ref[...].astype(BF16)
        h1 = _dot(x, w1_ref[0])
        h3 = _dot(x, w3_ref[0])
        h = (h1 * jax.nn.sigmoid(h1) * h3).astype(BF16)
        y_ref[...] = _dot(h, w2_ref[0])

    @pl.when(jnp.logical_not(used))
    def _():
        y_ref[...] = jnp.zeros_like(y_ref)


def moe_expert_ffn(xs, block_e, n_used, w1, w3, w2):
    n_rows, d = xs.shape
    bm = FFN_BM
    hid = w1.shape[2]

    def row_map(i, be, nu):
        return (jnp.minimum(i, nu[0] - 1), 0)

    def w_map(i, be, nu):
        return (be[jnp.minimum(i, nu[0] - 1)], 0, 0)

    grid_spec = pltpu.PrefetchScalarGridSpec(
        num_scalar_prefetch=2,
        grid=(n_rows // bm,),
        in_specs=[pl.BlockSpec((bm, d), row_map),
                  pl.BlockSpec((1, d, hid), w_map),
                  pl.BlockSpec((1, d, hid), w_map),
                  pl.BlockSpec((1, hid, d), w_map)],
        out_specs=pl.BlockSpec((bm, d), lambda i, be, nu: (i, 0)),
    )
    return pl.pallas_call(
        _ffn_kernel,
        grid_spec=grid_spec,
        out_shape=jax.ShapeDtypeStruct((n_rows, d), F32),
        compiler_params=pltpu.CompilerParams(dimension_semantics=("arbitrary",)),
        name="moe_expert_ffn",
    )(block_e, n_used, xs, w1, w3, w2)


def _combine_kernel(pos_hbm, y_hbm, v_ref, g_ref, sw1_ref, sw3_ref, sw2_ref, f_ref, pos_smem, ybuf, sem_idx, sem_y):
    i = pl.program_id(0)
    n = pl.num_programs(0)
    tb = v_ref.shape[0]
    slot = lax.rem(i, 2)

    def fetch_idx(step, s):
        return pltpu.make_async_copy(pos_hbm.at[step], pos_smem.at[s], sem_idx.at[s])

    def issue_gather(s):
        def body(t, carry):
            for k in range(TOP_K):
                p = pos_smem[s, 0, t * TOP_K + k]
                pltpu.make_async_copy(y_hbm.at[pl.ds(p, 1)], ybuf.at[s, k, pl.ds(t, 1)], sem_y.at[s]).start()
            return carry
        lax.fori_loop(0, tb, body, 0)

    @pl.when(i == 0)
    def _():
        cp = fetch_idx(0, 0)
        cp.start()
        cp.wait()
        issue_gather(0)

    @pl.when(i + 1 < n)
    def _():
        cp = fetch_idx(i + 1, 1 - slot)
        cp.start()
        cp.wait()
        issue_gather(1 - slot)

    x = v_ref[...].astype(BF16)
    h1 = _dot(x, sw1_ref[...])
    h3 = _dot(x, sw3_ref[...])
    acc = _dot((h1 * jax.nn.sigmoid(h1) * h3).astype(BF16), sw2_ref[...])
    for k in range(TOP_K):
        pltpu.make_async_copy(ybuf.at[slot, k], ybuf.at[slot, k], sem_y.at[slot]).wait()
    g = g_ref[...]
    for k in range(TOP_K):
        acc = acc + g[:, k:k + 1] * ybuf[slot, k]
    f_ref[...] = acc


def moe_combine(pos3, y, v, gate, sw1, sw3, sw2):
    n_tok, d = v.shape
    tb = COMBINE_TB
    hid = sw1.shape[1]
    return pl.pallas_call(
        _combine_kernel,
        grid=(n_tok // tb,),
        in_specs=[pl.BlockSpec(memory_space=pl.ANY),
                  pl.BlockSpec(memory_space=pl.ANY),
                  pl.BlockSpec((tb, d), lambda i: (i, 0)),
                  pl.BlockSpec((tb, TOP_K), lambda i: (i, 0)),
                  pl.BlockSpec((d, hid), lambda i: (0, 0)),
                  pl.BlockSpec((d, hid), lambda i: (0, 0)),
                  pl.BlockSpec((hid, d), lambda i: (0, 0))],
        out_specs=pl.BlockSpec((tb, d), lambda i: (i, 0)),
        out_shape=jax.ShapeDtypeStruct((n_tok, d), F32),
        scratch_shapes=[pltpu.SMEM((2, 1, tb * TOP_K), jnp.int32),
                        pltpu.VMEM((2, TOP_K, tb, d), F32),
                        pltpu.SemaphoreType.DMA((2,)),
                        pltpu.SemaphoreType.DMA((2,))],
        compiler_params=pltpu.CompilerParams(dimension_semantics=("arbitrary",), vmem_limit_bytes=48 * 1024 * 1024),
        name="moe_combine",
    )(pos3, y, v, gate, sw1, sw3, sw2)


def moe_ffn(t, router_w, router_bias, w1, w3, w2, sw1, sw3, sw2):
    n_tok, d = t.shape
    tf = t.astype(F32)
    bm = FFN_BM
    e_sel, rank, gate, cnt = moe_route(tf, router_w.astype(F32), router_bias.astype(F32))
    counts = cnt[0].astype(jnp.int32)
    padded = (counts + bm - 1) // bm * bm
    pend = jnp.cumsum(padded)
    pstart = pend - padded
    n_blocks = -(-(n_tok * TOP_K + N_EXPERTS * (bm - 1)) // bm)
    expert_ids = jnp.arange(N_EXPERTS, dtype=jnp.int32)
    pos = rank + jnp.sum(jnp.where(e_sel[..., None] == expert_ids, pstart, 0), -1)
    block_e = jnp.minimum(jnp.searchsorted(pend, jnp.arange(n_blocks, dtype=jnp.int32) * bm, side='right'), N_EXPERTS - 1).astype(jnp.int32)
    n_used = (pend[-1:] // bm).astype(jnp.int32)
    zero_start = jnp.maximum(pend - bm, 0).astype(jnp.int32)
    zero_flag = (counts > 0).astype(jnp.int32)
    xs = moe_dispatch(tf, pos.reshape(n_tok // DISPATCH_TB, 1, DISPATCH_TB * TOP_K), zero_start, zero_flag, n_used, n_blocks * bm)
    y = moe_expert_ffn(xs, block_e, n_used, w1.astype(BF16), w3.astype(BF16), w2.astype(BF16))
    return moe_combine(pos.reshape(n_tok // COMBINE_TB, 1, COMBINE_TB * TOP_K), y, tf, gate,
                       sw1.astype(BF16), sw3.astype(BF16), sw2.astype(BF16))


HEAD_W = 128
MIX_VMEM = 48 * 1024 * 1024
NORM_ROWS = 256


def _dot_nt(a, b):
    return lax.dot_general(a, b, (((1,), (1,)), ((), ())), preferred_element_type=F32)


def _dot_tn(a, b):
    return lax.dot_general(a, b, (((0,), (0,)), ((), ())), preferred_element_type=F32)


def _dot_hi(a, b):
    return jnp.dot(a, b, preferred_element_type=F32, precision=lax.Precision.HIGHEST)


def _tri(n, upper):
    r = lax.broadcasted_iota(jnp.int32, (n, n), 0)
    c = lax.broadcasted_iota(jnp.int32, (n, n), 1)
    return (r <= c) if upper else (r >= c)


def _bidir_loop(n_ctx_chunks, n_chunks, step, unroll):
    def body(c, carry):
        cb = jnp.where(c < n_ctx_chunks, n_ctx_chunks - 1 - c, n_chunks - 1 + n_ctx_chunks - c)
        step(c, False)
        step(cb, True)
        return carry
    lax.fori_loop(0, n_chunks, body, 0, unroll=unroll)


def _norm_gate_epilogue(hf_ref, hb_ref, gate_ref, nw_ref, y_ref, n_rows, rms, gate_fn):
    def body(i, carry):
        rows = pl.ds(pl.multiple_of(i * NORM_ROWS, NORM_ROWS), NORM_ROWS)
        h = hf_ref[rows, :] + hb_ref[rows, :]
        if not rms:
            h = h - jnp.mean(h, axis=1, keepdims=True)
        h = h * lax.rsqrt(jnp.mean(h * h, axis=1, keepdims=True) + EPS)
        y_ref[rows, :] = (h * nw_ref[...] * gate_fn(gate_ref[rows, :].astype(F32))).astype(y_ref.dtype)
        return carry
    lax.fori_loop(0, n_rows // NORM_ROWS, body, 0)


def _silu(x):
    return x * jax.nn.sigmoid(x)


def _retention_kernel(ld_ref, q_ref, k_ref, v_ref, g_ref, cos_ref, sin_ref, perm_ref, nw_ref, y_ref,
                      qr_ref, kr_ref, hf_ref, hb_ref, sf_ref, sb_ref, *, n_ctx):
    cs = RET_CHUNK
    lt = q_ref.shape[0]
    hd = pl.program_id(1)

    def rope(i, carry):
        rows = pl.ds(pl.multiple_of(i * NORM_ROWS, NORM_ROWS), NORM_ROWS)
        cs_, sn_ = cos_ref[rows, :], sin_ref[rows, :]
        q = q_ref[rows, :]
        k = k_ref[rows, :]
        qr_ref[rows, :] = (q.astype(F32) * cs_ + _dot(q, perm_ref[...]) * sn_).astype(BF16)
        kr_ref[rows, :] = ((k.astype(F32) * cs_ + _dot(k, perm_ref[...]) * sn_) * RET_DQK ** -0.5).astype(BF16)
        return carry
    lax.fori_loop(0, lt // NORM_ROWS, rope, 0)

    pos_r = lax.broadcasted_iota(jnp.int32, (cs, cs), 0).astype(F32)
    pos_c = lax.broadcasted_iota(jnp.int32, (cs, cs), 1).astype(F32)
    pos_t = lax.broadcasted_iota(jnp.int32, (cs, HEAD_W), 0).astype(F32)
    consts = []
    for d in range(2):
        lg = ld_ref[d, hd]
        if d == 0:
            decay = jnp.where(pos_r >= pos_c, jnp.exp((pos_r - pos_c) * lg), 0.0)
            q_dec = jnp.exp((pos_t + 1.0) * lg)
            k_dec = jnp.exp((cs - 1.0 - pos_t) * lg)
        else:
            decay = jnp.where(pos_r <= pos_c, jnp.exp((pos_c - pos_r) * lg), 0.0)
            q_dec = jnp.exp((cs - pos_t) * lg)
            k_dec = jnp.exp(pos_t * lg)
        consts.append((decay, q_dec, k_dec, jnp.exp(cs * lg)))
    sf_ref[...] = jnp.zeros_like(sf_ref)
    sb_ref[...] = jnp.zeros_like(sb_ref)

    def step(chunk, reverse):
        decay, q_dec, k_dec, c_dec = consts[1 if reverse else 0]
        s_ref = sb_ref if reverse else sf_ref
        h_ref = hb_ref if reverse else hf_ref
        rows = pl.ds(pl.multiple_of(chunk * cs, cs), cs)
        q = qr_ref[rows, :]
        k = kr_ref[rows, :]
        v = v_ref[rows, :]
        sc = _dot_nt(q, k) * decay
        s_in = s_ref[...]
        inter = _dot((q.astype(F32) * q_dec).astype(BF16), s_in.astype(BF16))
        h_ref[rows, :] = inter + _dot(sc.astype(BF16), v)
        s_ref[...] = c_dec * s_in + _dot_tn((k.astype(F32) * k_dec).astype(BF16), v)

    _bidir_loop(n_ctx // cs, lt // cs, step, 2)
    _norm_gate_epilogue(hf_ref, hb_ref, g_ref, nw_ref, y_ref, lt, False, _silu)


def retention_mixer(p16, col0, log_decay, norm_w, cos_t, sin_t, batch, n_ctx):
    lt = p16.shape[0] // batch
    nh = RET_HEADS
    half = RET_DQK // 2
    src = jnp.arange(HEAD_W)
    swapped = jnp.where((src % half) < half // 2, src + half // 2, src - half // 2)
    perm = (src[:, None] == swapped[None, :]).astype(BF16)

    def blk(off):
        return pl.BlockSpec((lt, HEAD_W), lambda b, h, ld: (b, off + h))

    grid_spec = pltpu.PrefetchScalarGridSpec(
        num_scalar_prefetch=1,
        grid=(batch, nh),
        in_specs=[blk(col0), blk(col0 + nh), blk(col0 + 2 * nh), blk(col0 + 3 * nh),
                  pl.BlockSpec((lt, HEAD_W), lambda b, h, ld: (0, 0)),
                  pl.BlockSpec((lt, HEAD_W), lambda b, h, ld: (0, 0)),
                  pl.BlockSpec((HEAD_W, HEAD_W), lambda b, h, ld: (0, 0)),
                  pl.BlockSpec((1, HEAD_W), lambda b, h, ld: (0, h))],
        out_specs=pl.BlockSpec((lt, HEAD_W), lambda b, h, ld: (b, h)),
        scratch_shapes=[pltpu.VMEM((lt, HEAD_W), BF16), pltpu.VMEM((lt, HEAD_W), BF16),
                        pltpu.VMEM((lt, HEAD_W), F32), pltpu.VMEM((lt, HEAD_W), F32),
                        pltpu.VMEM((RET_DQK, RET_DV), F32), pltpu.VMEM((RET_DQK, RET_DV), F32)],
    )
    return pl.pallas_call(
        functools.partial(_retention_kernel, n_ctx=n_ctx),
        grid_spec=grid_spec,
        out_shape=jax.ShapeDtypeStruct((batch * lt, nh * HEAD_W), BF16),
        compiler_params=pltpu.CompilerParams(dimension_semantics=("arbitrary", "arbitrary"), vmem_limit_bytes=MIX_VMEM),
        name="retention_mixer",
    )(log_decay.astype(F32), p16, p16, p16, p16, cos_t, sin_t, perm, norm_w.reshape(1, -1).astype(F32))


def _hgrn_kernel(q_ref, i_ref, g_ref, ff_ref, fb_ref, lb_ref, nw_ref, y_ref, hf_ref, hb_ref, sf_ref, sb_ref, *, n_ctx):
    cs = HGRN_CHUNK
    lt = q_ref.shape[0]
    lb = lb_ref[...]
    sf_ref[...] = jnp.zeros_like(sf_ref)
    sb_ref[...] = jnp.zeros_like(sb_ref)
    tri = (_tri(cs, False).astype(F32), _tri(cs, True).astype(F32))
    mask = (_tri(cs, False), _tri(cs, True))

    def step(chunk, reverse):
        d = 1 if reverse else 0
        f_ref = fb_ref if reverse else ff_ref
        s_ref = sb_ref if reverse else sf_ref
        h_ref = hb_ref if reverse else hf_ref
        rows = pl.ds(pl.multiple_of(chunk * cs, cs), cs)
        f = lb + (1.0 - lb) * jax.nn.sigmoid(f_ref[rows, :])
        kk = 1.0 - f
        a_cum = _dot_hi(tri[d], jnp.log(f))
        a_end = a_cum[0:1, :] if reverse else a_cum[cs - 1:cs, :]
        mid = cs // 2 if reverse else cs // 2 - 1
        a_mid = a_cum[mid:mid + 1, :]
        q = _silu(q_ref[rows, :].astype(F32))
        v = i_ref[rows, :]
        s_in = s_ref[...]
        inter = _dot_nt((q * jnp.exp(a_cum)).astype(BF16), s_in.astype(BF16))
        sc = _dot_nt((q * jnp.exp(a_cum - a_mid)).astype(BF16), (kk * jnp.exp(a_mid - a_cum)).astype(BF16))
        sc = jnp.where(mask[d], sc, 0.0)
        h_ref[rows, :] = inter + _dot(sc.astype(BF16), v)
        s_ref[...] = s_in * jnp.exp(a_end) + _dot_tn(v, (kk * jnp.exp(a_end - a_cum)).astype(BF16))

    _bidir_loop(n_ctx // cs, lt // cs, step, 2)
    _norm_gate_epilogue(hf_ref, hb_ref, g_ref, nw_ref, y_ref, lt, True, _silu)


EVEN16_MLSTM_Q, EVEN16_MLSTM_K, EVEN16_MLSTM_V, EVEN16_MLSTM_O = 0, 2, 4, 8
EVEN16_HGRN_Q, EVEN16_HGRN_I, EVEN16_HGRN_G = 12, 16, 20
EVEN32_GATES, EVEN32_FF, EVEN32_FB = 0, 1, 5


def hgrn_mixer(p16, p32, lb, norm_w, batch, n_ctx):
    lt = p16.shape[0] // batch
    nh = HGRN_HEADS

    def blk(off):
        return pl.BlockSpec((lt, HEAD_W), lambda b, h: (b, off + h))

    def vec():
        return pl.BlockSpec((1, HEAD_W), lambda b, h: (0, h))

    return pl.pallas_call(
        functools.partial(_hgrn_kernel, n_ctx=n_ctx),
        grid=(batch, nh),
        in_specs=[blk(EVEN16_HGRN_Q), blk(EVEN16_HGRN_I), blk(EVEN16_HGRN_G), blk(EVEN32_FF), blk(EVEN32_FB), vec(), vec()],
        out_specs=pl.BlockSpec((lt, HEAD_W), lambda b, h: (b, h)),
        out_shape=jax.ShapeDtypeStruct((batch * lt, nh * HEAD_W), BF16),
        scratch_shapes=[pltpu.VMEM((lt, HEAD_W), F32), pltpu.VMEM((lt, HEAD_W), F32),
                        pltpu.VMEM((HGRN_DV, HGRN_DK), F32), pltpu.VMEM((HGRN_DV, HGRN_DK), F32)],
        compiler_params=pltpu.CompilerParams(dimension_semantics=("arbitrary", "arbitrary"), vmem_limit_bytes=MIX_VMEM),
        name="hgrn_mixer",
    )(p16, p16, p16, p32, p32, lb.reshape(1, -1).astype(F32), norm_w.reshape(1, -1).astype(F32))


def _mlstm_kernel(q_ref, k_ref, v_ref, o_ref, gt_ref, gb_ref, nw_ref, y_ref, hf_ref, hb_ref, cf_ref, cb_ref, mf_ref, mb_ref, *, n_ctx):
    cs = MLSTM_CHUNK
    lt = q_ref.shape[0]
    hd = pl.program_id(1)
    lane = lax.broadcasted_iota(jnp.int32, (1, HEAD_W), 1)
    head_mask = ((lane // MLSTM_DQK) == (hd % 2)).astype(F32)
    gcol = lax.broadcasted_iota(jnp.int32, (1, HEAD_W), 1)
    tri = (_tri(cs, False).astype(F32), _tri(cs, True).astype(F32))
    mask = (_tri(cs, False), _tri(cs, True))
    ones_v = jnp.ones((cs, HEAD_W), BF16)
    for ref in (cf_ref, cb_ref):
        ref[...] = jnp.zeros_like(ref)
    for ref in (mf_ref, mb_ref):
        ref[...] = jnp.full_like(ref, NEG_BIG)

    def pick(tile, col):
        return jnp.sum(jnp.where(gcol == col, tile, 0.0), axis=1, keepdims=True)

    def step(chunk, reverse):
        d = 1 if reverse else 0
        c_ref = cb_ref if reverse else cf_ref
        m_ref = mb_ref if reverse else mf_ref
        h_ref = hb_ref if reverse else hf_ref
        rows = pl.ds(pl.multiple_of(chunk * cs, cs), cs)
        x = gt_ref[rows, :] + gb_ref[...]
        ci = hd + (2 * MLSTM_HEADS if reverse else 0)
        ig = pick(x, ci)
        lf = pick(jax.nn.log_sigmoid(x), ci + MLSTM_HEADS)
        lf_t = jnp.broadcast_to(lf, (cs, HEAD_W))
        b = _dot_hi(tri[d], lf_t)
        g = b[0:1, :] if reverse else b[cs - 1:cs, :]
        ig_t = jnp.broadcast_to(ig, (cs, HEAD_W))
        a = g - b + ig_t
        m_loc = jnp.max(a, axis=0, keepdims=True)
        w = jnp.exp(a - m_loc)
        m_in = m_ref[...]
        r_row = jnp.transpose(ig_t - b)[0:1, 0:cs]
        log_d = jnp.where(mask[d], b[:, 0:cs] + r_row, -jnp.inf)
        inter = b + m_in
        m_out = jnp.maximum(inter, jnp.max(log_d, axis=1, keepdims=True))
        q = q_ref[rows, :]
        k = k_ref[rows, :]
        v = v_ref[rows, :]
        qm = (q.astype(F32) * (head_mask * MLSTM_DQK ** -0.5)).astype(BF16)
        s = _dot_nt(qm, k) * jnp.exp(log_d - m_out[:, 0:cs])
        e_int = jnp.exp(inter - m_out)
        vo = jnp.concatenate([v, ones_v], axis=1)
        c_in = c_ref[...]
        r = _dot(s.astype(BF16), vo) + jnp.concatenate([e_int, e_int], axis=1) * _dot(qm, c_in.astype(BF16))
        num, den = r[:, :HEAD_W], r[:, HEAD_W:]
        h_ref[rows, :] = num / jnp.maximum(jnp.abs(den), jnp.exp(-m_out))
        m_new = jnp.maximum(g + m_in, m_loc)
        sp = jnp.exp(g + m_in - m_new)
        sl = jnp.exp(m_loc - m_new)
        kw = (k.astype(F32) * head_mask * w).astype(BF16)
        c_loc = _dot_tn(kw, vo)
        c_ref[...] = jnp.concatenate([sp, sp], axis=1) * c_in + jnp.concatenate([sl, sl], axis=1) * c_loc
        m_ref[...] = m_new

    _bidir_loop(n_ctx // cs, lt // cs, step, 2)
    _norm_gate_epilogue(hf_ref, hb_ref, o_ref, nw_ref, y_ref, lt, False, jax.nn.sigmoid)


def mlstm_mixer(p16, p32, gate_b, norm_w, batch, n_ctx):
    lt = p16.shape[0] // batch
    nh = MLSTM_HEADS
    gb = jnp.zeros((1, HEAD_W), F32).at[0, :4 * nh].set(gate_b.astype(F32).reshape(-1))

    return pl.pallas_call(
        functools.partial(_mlstm_kernel, n_ctx=n_ctx),
        grid=(batch, nh),
        in_specs=[pl.BlockSpec((lt, HEAD_W), lambda b, h: (b, EVEN16_MLSTM_Q + h // 2)),
                  pl.BlockSpec((lt, HEAD_W), lambda b, h: (b, EVEN16_MLSTM_K + h // 2)),
                  pl.BlockSpec((lt, HEAD_W), lambda b, h: (b, EVEN16_MLSTM_V + h)),
                  pl.BlockSpec((lt, HEAD_W), lambda b, h: (b, EVEN16_MLSTM_O + h)),
                  pl.BlockSpec((lt, HEAD_W), lambda b, h: (b, EVEN32_GATES)),
                  pl.BlockSpec((1, HEAD_W), lambda b, h: (0, 0)),
                  pl.BlockSpec((1, HEAD_W), lambda b, h: (0, h))],
        out_specs=pl.BlockSpec((lt, HEAD_W), lambda b, h: (b, h)),
        out_shape=jax.ShapeDtypeStruct((batch * lt, nh * HEAD_W), BF16),
        scratch_shapes=[pltpu.VMEM((lt, HEAD_W), F32), pltpu.VMEM((lt, HEAD_W), F32),
                        pltpu.VMEM((HEAD_W, 2 * HEAD_W), F32), pltpu.VMEM((HEAD_W, 2 * HEAD_W), F32),
                        pltpu.VMEM((1, HEAD_W), F32), pltpu.VMEM((1, HEAD_W), F32)],
        compiler_params=pltpu.CompilerParams(dimension_semantics=("arbitrary", "arbitrary"), vmem_limit_bytes=MIX_VMEM),
        name="mlstm_mixer",
    )(p16, p16, p16, p16, p32, gb, norm_w.reshape(1, -1).astype(F32))


ROW_BLOCKS_PER_BATCH_IN = 4
ROW_BLOCKS_PER_BATCH_OUT = 8
PROJ_TN = 512


def _row_mod(mod_ref, idx, is_ctx):
    return jnp.where(is_ctx, mod_ref[0, 0, idx:idx + 1, :], mod_ref[0, 1, idx:idx + 1, :])


def _is_ctx_rows(tm, blocks_per_batch, n_ctx):
    row = (pl.program_id(0) % blocks_per_batch) * tm + lax.broadcasted_iota(jnp.int32, (tm, 1), 0)
    return row < n_ctx


def _proj_in_kernel(h_ref, mod_ref, w_ref, o_ref, u_ref, *, n_ctx, blocks_per_batch):
    tm = h_ref.shape[0]

    @pl.when(pl.program_id(1) == 0)
    def _():
        is_ctx = _is_ctx_rows(tm, blocks_per_batch, n_ctx)
        u = h_ref[...] * (1.0 + _row_mod(mod_ref, 1, is_ctx)) + _row_mod(mod_ref, 0, is_ctx)
        u_ref[...] = u.astype(BF16)

    o_ref[...] = _dot(u_ref[...], w_ref[...]).astype(o_ref.dtype)


def proj_in(h, modtab, w, out_dtype, batch, n_ctx, tn=PROJ_TN):
    t, d = h.shape
    n = w.shape[1]
    bpb = ROW_BLOCKS_PER_BATCH_IN
    tm = t // batch // bpb
    return pl.pallas_call(
        functools.partial(_proj_in_kernel, n_ctx=n_ctx, blocks_per_batch=bpb),
        grid=(t // tm, n // tn),
        in_specs=[pl.BlockSpec((tm, d), lambda i, j: (i, 0)),
                  pl.BlockSpec((1, 2, 6, d), lambda i, j: (i // bpb, 0, 0, 0)),
                  pl.BlockSpec((d, tn), lambda i, j: (0, j))],
        out_specs=pl.BlockSpec((tm, tn), lambda i, j: (i, j)),
        out_shape=jax.ShapeDtypeStruct((t, n), out_dtype),
        scratch_shapes=[pltpu.VMEM((tm, d), BF16)],
        compiler_params=pltpu.CompilerParams(dimension_semantics=("arbitrary", "arbitrary"), vmem_limit_bytes=MIX_VMEM),
        name="proj_in",
    )(h, modtab, w)


def _layer_norm_rows(z, w, b):
    mu = jnp.mean(z, axis=1, keepdims=True)
    zc = z - mu
    var = jnp.mean(zc * zc, axis=1, keepdims=True)
    return zc * lax.rsqrt(var + EPS) * w + b


def _proj_out_kernel(ya_ref, yb_ref, wa_ref, wb_ref, h_ref, mod_ref, lnw_ref, lnb_ref, h1_ref, v_ref, *, n_ctx, blocks_per_batch):
    tm = h_ref.shape[0]
    is_ctx = _is_ctx_rows(tm, blocks_per_batch, n_ctx)
    y = _dot(ya_ref[...], wa_ref[...]) + _dot(yb_ref[...], wb_ref[...])
    z = DEEPNORM_ALPHA * h_ref[...] + _row_mod(mod_ref, 2, is_ctx) * y
    h1 = _layer_norm_rows(z, lnw_ref[...], lnb_ref[...])
    h1_ref[...] = h1
    v_ref[...] = h1 * (1.0 + _row_mod(mod_ref, 4, is_ctx)) + _row_mod(mod_ref, 3, is_ctx)


def proj_out_ln(ya, yb, wa, wb, h, modtab, ln_w, ln_b, batch, n_ctx):
    t, d = h.shape
    bpb = ROW_BLOCKS_PER_BATCH_OUT
    tm = t // batch // bpb
    ka, kb = ya.shape[1], yb.shape[1]
    row = lambda i: (i, 0)
    fixed = lambda i: (0, 0)
    return pl.pallas_call(
        functools.partial(_proj_out_kernel, n_ctx=n_ctx, blocks_per_batch=bpb),
        grid=(t // tm,),
        in_specs=[pl.BlockSpec((tm, ka), row), pl.BlockSpec((tm, kb), row),
                  pl.BlockSpec((ka, d), fixed), pl.BlockSpec((kb, d), fixed),
                  pl.BlockSpec((tm, d), row),
                  pl.BlockSpec((1, 2, 6, d), lambda i: (i // bpb, 0, 0, 0)),
                  pl.BlockSpec((1, d), fixed), pl.BlockSpec((1, d), fixed)],
        out_specs=[pl.BlockSpec((tm, d), row), pl.BlockSpec((tm, d), row)],
        out_shape=[jax.ShapeDtypeStruct((t, d), F32), jax.ShapeDtypeStruct((t, d), F32)],
        compiler_params=pltpu.CompilerParams(dimension_semantics=("arbitrary",), vmem_limit_bytes=MIX_VMEM),
        name="proj_out_ln",
    )(ya, yb, wa, wb, h, modtab, ln_w.reshape(1, d).astype(F32), ln_b.reshape(1, d).astype(F32))


def rope_tables(n_ctx, n_lat):
    half = RET_DQK // 2
    inv = ROPE_BASE ** (-jnp.arange(0, half, 2, dtype=F32) / half)
    t = jnp.arange(n_lat)
    row = (t // GRID_W).astype(F32)
    col = (t % GRID_W).astype(F32)
    a_row = row[:, None] * inv[None, :]
    a_col = col[:, None] * inv[None, :]
    cos = jnp.concatenate([jnp.cos(a_row), jnp.cos(a_row), jnp.cos(a_col), jnp.cos(a_col)], -1)
    sin = jnp.concatenate([-jnp.sin(a_row), jnp.sin(a_row), -jnp.sin(a_col), jnp.sin(a_col)], -1)
    cos = jnp.concatenate([jnp.ones((n_ctx, HEAD_W), F32), cos], 0)
    sin = jnp.concatenate([jnp.zeros((n_ctx, HEAD_W), F32), sin], 0)
    return cos, sin


def kernel(x, c, ctx, c_ctx, ada_w, ada_b, ln_w, ln_b, even_w_in, mlstm_gate_b, mlstm_norm_w, hgrn_lb, hgrn_norm_w, even_w_out, odd_w_in, hy_conv_w, hy_conv_b, hy_f_w1, hy_f_b1, hy_f_w2, hy_f_b2, hy_f_w3, hy_f_b3, hy_f_freq, hy_f_wout, hy_skip, hy_norm_w, ret_log_decay, ret_norm_w, odd_w_out, router_w, router_bias, exp_w1, exp_w3, exp_w2, sh_w1, sh_w3, sh_w2):
    batch, n_lat, d = x.shape
    n_ctx = ctx.shape[1]
    lt = n_ctx + n_lat
    t = batch * lt
    sm = jax.nn.softmax(hgrn_lb.astype(F32), axis=0)
    lower_bounds = jnp.cumsum(sm, axis=0) - sm[0]
    acts = jnp.concatenate([jax.nn.silu(c.astype(F32)), jax.nn.silu(c_ctx.astype(F32))[None]], 0)
    acts = jnp.pad(acts, ((0, 16 - (batch + 1) % 16), (0, 0)))
    cos_t, sin_t = rope_tables(n_ctx, n_lat)
    is_ctx = (jnp.arange(lt) < n_ctx)[None, :, None]
    h = jnp.concatenate([ctx, x], axis=1).reshape(t, d).astype(F32)
    for l in range(DEPTH):
        mods = pallas_matmul(acts, ada_w[l].astype(F32), tm=acts.shape[0], tn=512)[:batch + 1] + ada_b[l]
        mod_lat = mods[:batch].reshape(batch, 6, d)
        mod_ctx = jnp.broadcast_to(mods[batch].reshape(1, 6, d), (batch, 6, d))
        modtab = jnp.stack([mod_ctx, mod_lat], axis=1)
        if l % 2 == 0:
            e = l // 2
            w = even_w_in[e].astype(F32)
            c0 = sum(MLSTM_SPLITS[:4])
            c1 = c0 + MLSTM_SPLITS[4]
            c2 = c1 + HGRN_SPLITS[0] + HGRN_SPLITS[1]
            c3 = c2 + HGRN_SPLITS[2] + HGRN_SPLITS[3]
            w16 = jnp.concatenate([w[:, :c0], w[:, c1:c2], w[:, c3:]], 1).astype(BF16)
            w32 = jnp.concatenate([w[:, c0:c1], jnp.zeros((d, HEAD_W - MLSTM_SPLITS[4]), F32), w[:, c2:c3]], 1).astype(BF16)
            p16 = proj_in(h, modtab, w16, BF16, batch, n_ctx)
            p32 = proj_in(h, modtab, w32, F32, batch, n_ctx, tn=3 * HEAD_W)
            ya = mlstm_mixer(p16, p32, mlstm_gate_b[e], mlstm_norm_w[e], batch, n_ctx)
            yb = hgrn_mixer(p16, p32, lower_bounds[e], hgrn_norm_w[e], batch, n_ctx)
            w_out = even_w_out[e]
        else:
            o = l // 2
            hw = ODD_SPLITS[0]
            p16 = proj_in(h, modtab, odd_w_in[o].astype(BF16), BF16, batch, n_ctx)
            u = p16[:, :hw].astype(F32).reshape(batch, lt, hw)
            hyena_p = (hy_conv_w[o], hy_conv_b[o], hy_f_w1[o], hy_f_b1[o], hy_f_w2[o], hy_f_b2[o], hy_f_w3[o], hy_f_b3[o], hy_f_freq[o], hy_f_wout[o], hy_skip[o], hy_norm_w[o])
            hy_lat, hy_ctx = hyena_group(u[:, n_ctx:], u[:, :n_ctx], *[p.astype(F32) for p in hyena_p])
            ya = jnp.concatenate([hy_ctx, hy_lat], 1).reshape(t, HYENA_WIDTH).astype(BF16)
            yb = retention_mixer(p16, hw // HEAD_W, ret_log_decay[o], ret_norm_w[o], cos_t, sin_t, batch, n_ctx)
            w_out = odd_w_out[o]
        ka = ya.shape[1]
        h1, v = proj_out_ln(ya, yb, w_out[:ka].astype(BF16), w_out[ka:].astype(BF16), h, modtab, ln_w[l, 0], ln_b[l, 0], batch, n_ctx)
        f = moe_ffn(v, router_w[l], router_bias[l], exp_w1[l], exp_w3[l], exp_w2[l], sh_w1[l], sh_w3[l], sh_w2[l])
        gate2 = jnp.where(is_ctx, mod_ctx[:, None, 5], mod_lat[:, None, 5])
        h = layer_norm(DEEPNORM_ALPHA * h1.reshape(batch, lt, d) + gate2 * f.reshape(batch, lt, d), ln_w[l, 1], ln_b[l, 1]).reshape(t, d)
    return h.reshape(batch, lt, d)[:, n_ctx:]
```

```python
import functools
import math
import jax
import jax.numpy as jnp
from jax import lax
from jax.experimental import pallas as pl
from jax.experimental.pallas import tpu as pltpu

D_MODEL = 1024
BATCH = 8
SEQ = 4096
DEPTH = 4

GRID_W = 64
CTX_LEN = 256
N_EVEN = (DEPTH + 1) // 2
N_ODD = DEPTH // 2

MLSTM_HEADS = 4
MLSTM_DQK = 64
MLSTM_DV = 128
MLSTM_CHUNK = 64
HGRN_HEADS = 4
HGRN_DK = 128
HGRN_DV = 128
HGRN_CHUNK = 32
HYENA_WIDTH = 512
HYENA_GROUPS = 8
HYENA_ORDER = 2
HYENA_EMB = 33
HYENA_FILTER_HIDDEN = 64
HYENA_SHORT = 3
HYENA_TARGET = 1e-2
HYENA_FAST_PCT = 0.3
HYENA_SLOW_PCT = 1.5
RET_HEADS = 4
RET_DQK = 128
RET_DV = 128
RET_CHUNK = 64
ROPE_BASE = 10000.0
N_EXPERTS = 64
TOP_K = 8
EXPERT_HIDDEN = 256
SHARED_HIDDEN = 256
ROUTED_SCALE = 2.5
MOE_BLOCK = 128
DEEPNORM_ALPHA = (2 * DEPTH) ** 0.25
DEEPNORM_BETA = (8 * DEPTH) ** -0.25
EPS = 1e-5
NEG_BIG = -1e30

MLSTM_SPLITS = (MLSTM_HEADS * MLSTM_DQK, MLSTM_HEADS * MLSTM_DQK, MLSTM_HEADS * MLSTM_DV, MLSTM_HEADS * MLSTM_DV, 4 * MLSTM_HEADS)
HGRN_SPLITS = (HGRN_HEADS * HGRN_DK, HGRN_HEADS * HGRN_DV, HGRN_HEADS * HGRN_DK, HGRN_HEADS * HGRN_DK, HGRN_HEADS * HGRN_DV)
EVEN_SPLITS = MLSTM_SPLITS + HGRN_SPLITS
EVEN_PROJ = sum(EVEN_SPLITS)
EVEN_MIX = MLSTM_HEADS * MLSTM_DV + HGRN_HEADS * HGRN_DV
RET_SPLITS = (RET_HEADS * RET_DQK, RET_HEADS * RET_DQK, RET_HEADS * RET_DV, RET_HEADS * RET_DV)
ODD_SPLITS = ((HYENA_ORDER + 1) * HYENA_WIDTH,) + RET_SPLITS
ODD_PROJ = sum(ODD_SPLITS)
ODD_MIX = HYENA_WIDTH + RET_HEADS * RET_DV

F32 = jnp.float32
BF16 = jnp.bfloat16


def _mm_kernel(x_ref, w_ref, o_ref):
    o_ref[...] = jnp.dot(x_ref[...].astype(BF16), w_ref[...].astype(BF16), preferred_element_type=F32)


def pallas_matmul(x, w, tm=512, tn=512):
    t, k = x.shape
    n = w.shape[1]
    n_pad = -(-n // tn) * tn
    if n_pad != n:
        w = jnp.pad(w, ((0, 0), (0, n_pad - n)))
    out = pl.pallas_call(
        _mm_kernel,
        grid=(t // tm, n_pad // tn),
        in_specs=[pl.BlockSpec((tm, k), lambda i, j: (i, 0)), pl.BlockSpec((k, tn), lambda i, j: (0, j))],
        out_specs=pl.BlockSpec((tm, tn), lambda i, j: (i, j)),
        out_shape=jax.ShapeDtypeStruct((t, n_pad), F32),
    )(x, w)
    return out[:, :n]


def mm3(x, w):
    b, l, d = x.shape
    return pallas_matmul(x.reshape(b * l, d), w).reshape(b, l, -1)


def split_cols(a, sizes):
    out, start = [], 0
    for s in sizes:
        out.append(a[..., start:start + s])
        start += s
    return out


def to_heads(a, n_heads):
    b, l, w = a.shape
    return a.reshape(b, l, n_heads, w // n_heads).transpose(0, 2, 1, 3)


def chunk(a, cs):
    b, h, l = a.shape[:3]
    return a.reshape((b, h, l // cs, cs) + a.shape[3:])


def layer_norm(x, w, b):
    xf = x.astype(F32)
    mu = jnp.mean(xf, -1, keepdims=True)
    var = jnp.mean(jnp.square(xf - mu), -1, keepdims=True)
    return ((xf - mu) * lax.rsqrt(var + EPS) * w + b).astype(x.dtype)


def head_norm(h, w, rms):
    h = h.transpose(0, 2, 1, 3)
    if not rms:
        h = h - jnp.mean(h, -1, keepdims=True)
    h = h * lax.rsqrt(jnp.mean(jnp.square(h), -1, keepdims=True) + EPS)
    b, l, nh, dh = h.shape
    return h.reshape(b, l, nh * dh) * w


def group_rms(z, w):
    b, l, wd = z.shape
    zg = z.reshape(b, l, HYENA_GROUPS, wd // HYENA_GROUPS)
    zg = zg * lax.rsqrt(jnp.mean(jnp.square(zg), -1, keepdims=True) + EPS)
    return zg.reshape(b, l, wd) * w


def run_direction(scan_fn, ctx_seq, lat_seq, const, state0, reverse):
    if reverse:
        ctx_seq = tuple(jnp.flip(a, axis=2) for a in ctx_seq)
        lat_seq = tuple(jnp.flip(a, axis=2) for a in lat_seq)
    y_ctx, state = scan_fn(ctx_seq, const, state0)
    y_lat, _ = scan_fn(lat_seq, const, state)
    if reverse:
        y_ctx = jnp.flip(y_ctx, axis=2)
        y_lat = jnp.flip(y_lat, axis=2)
    return y_lat, y_ctx


def mlstm_scan(seq, const, state):
    q, k, v, ig, lf = seq
    cs = MLSTM_CHUNK
    qc, kc, vc, igc, lfc = (chunk(a, cs) for a in seq)
    b = jnp.cumsum(lfc, axis=-1)
    g = b[..., -1]
    a = g[..., None] - b + igc
    m_loc = jnp.max(a, axis=-1)
    w = jnp.exp(a - m_loc[..., None])
    c_loc = jnp.einsum('bhcs,bhcsv,bhcsk->bhcvk', w, vc, kc)
    n_loc = jnp.einsum('bhcs,bhcsk->bhck', w, kc)

    def step(carry, inp):
        c_st, n_st, m_st = carry
        g_j, ml_j, cl_j, nl_j = inp
        m_new = jnp.maximum(g_j + m_st, ml_j)
        sp = jnp.exp(g_j + m_st - m_new)
        sl = jnp.exp(ml_j - m_new)
        new = (sp[..., None, None] * c_st + sl[..., None, None] * cl_j, sp[..., None] * n_st + sl[..., None] * nl_j, m_new)
        return new, carry

    final, (c_in, n_in, m_in) = lax.scan(step, state, tuple(jnp.moveaxis(t, 2, 0) for t in (g, m_loc, c_loc, n_loc)))
    c_in, n_in, m_in = jnp.moveaxis(c_in, 0, 2), jnp.moveaxis(n_in, 0, 2), jnp.moveaxis(m_in, 0, 2)
    tri = jnp.tril(jnp.ones((cs, cs), bool))
    log_d = jnp.where(tri, b[..., :, None] - b[..., None, :] + igc[..., None, :], -jnp.inf)
    inter = b + m_in[..., None]
    m_out = jnp.maximum(inter, jnp.max(log_d, -1))
    s = jnp.einsum('bhctk,bhcsk->bhcts', qc, kc) * jnp.exp(log_d - m_out[..., None])
    e_int = jnp.exp(inter - m_out)
    num = jnp.einsum('bhcts,bhcsv->bhctv', s, vc) + e_int[..., None] * jnp.einsum('bhctk,bhcvk->bhctv', qc, c_in)
    den = jnp.sum(s, -1) + e_int * jnp.einsum('bhctk,bhck->bhct', qc, n_in)
    h = num / jnp.maximum(jnp.abs(den), jnp.exp(-m_out))[..., None]
    return h.reshape(v.shape), final


def hgrn2_scan(seq, const, state):
    q, k, v, lf = seq
    cs = HGRN_CHUNK
    qc, kc, vc, lfc = (chunk(a, cs) for a in seq)
    a_cum = jnp.cumsum(lfc, axis=3)
    a_end = a_cum[:, :, :, -1]
    s_loc = jnp.einsum('bhcsk,bhcsv->bhckv', kc * jnp.exp(a_end[:, :, :, None] - a_cum), vc)

    def step(s_st, inp):
        a_j, sl_j = inp
        return jnp.exp(a_j)[..., None] * s_st + sl_j, s_st

    final, s_in = lax.scan(step, state, (jnp.moveaxis(a_end, 2, 0), jnp.moveaxis(s_loc, 2, 0)))
    s_in = jnp.moveaxis(s_in, 0, 2)
    inter = jnp.einsum('bhctk,bhckv->bhctv', qc * jnp.exp(a_cum), s_in)
    a_mid = a_cum[:, :, :, cs // 2 - 1:cs // 2]
    tri = jnp.tril(jnp.ones((cs, cs), bool))
    sc = jnp.einsum('bhctk,bhcsk->bhcts', qc * jnp.exp(a_cum - a_mid), kc * jnp.exp(a_mid - a_cum))
    sc = jnp.where(tri, sc, 0.0)
    out = inter + jnp.einsum('bhcts,bhcsv->bhctv', sc, vc)
    return out.reshape(v.shape), final


def retention_scan(seq, log_gamma, state):
    q, k, v = seq
    cs = RET_CHUNK
    qc, kc, vc = (chunk(a, cs) for a in seq)
    pos = jnp.arange(cs, dtype=F32)
    tri = jnp.tril(jnp.ones((cs, cs), bool))
    delta = pos[:, None] - pos[None, :]
    decay = jnp.exp(jnp.where(tri, delta[None] * log_gamma[:, None, None], -jnp.inf))
    sc = jnp.einsum('bhctk,bhcsk->bhcts', qc, kc) * decay[:, None]
    intra = jnp.einsum('bhcts,bhcsv->bhctv', sc, vc)
    q_dec = jnp.exp((pos + 1.0)[None, :] * log_gamma[:, None])
    k_dec = jnp.exp((cs - 1.0 - pos)[None, :] * log_gamma[:, None])
    s_loc = jnp.einsum('bhcsk,hs,bhcsv->bhckv', kc, k_dec, vc)
    chunk_decay = jnp.exp(cs * log_gamma)[None, :, None, None]

    def step(s_st, sl_j):
        return chunk_decay * s_st + sl_j, s_st

    final, s_in = lax.scan(step, state, jnp.moveaxis(s_loc, 2, 0))
    s_in = jnp.moveaxis(s_in, 0, 2)
    inter = jnp.einsum('bhctk,ht,bhckv->bhctv', qc, q_dec, s_in)
    return (inter + intra).reshape(v.shape), final


def rope_1d(x, pos):
    d = x.shape[-1]
    inv = ROPE_BASE ** (-jnp.arange(0, d, 2, dtype=F32) / d)
    ang = pos[:, None] * inv[None, :]
    cos = jnp.cos(ang)[None, :, None, :]
    sin = jnp.sin(ang)[None, :, None, :]
    x1, x2 = x[..., :d // 2], x[..., d // 2:]
    return jnp.concatenate([x1 * cos - x2 * sin, x1 * sin + x2 * cos], -1)


def axial_rope(x, row, col):
    half = x.shape[-1] // 2
    return jnp.concatenate([rope_1d(x[..., :half], row), rope_1d(x[..., half:], col)], -1)


def mlstm_group(cols_lat, cols_ctx, gate_b, norm_w):
    def prep(cols):
        q, k, v, o, gates = cols
        b, l, _ = q.shape
        q = to_heads(q, MLSTM_HEADS) * MLSTM_DQK ** -0.5
        k = to_heads(k, MLSTM_HEADS)
        v = to_heads(v, MLSTM_HEADS)
        gates = (gates.reshape(b, l, 4, MLSTM_HEADS) + gate_b).transpose(2, 0, 3, 1)
        fwd = (q, k, v, gates[0], jax.nn.log_sigmoid(gates[1]))
        bwd = (q, k, v, gates[2], jax.nn.log_sigmoid(gates[3]))
        return fwd, bwd, o

    fwd_l, bwd_l, o_l = prep(cols_lat)
    fwd_c, bwd_c, o_c = prep(cols_ctx)
    batch = o_l.shape[0]
    state0 = (jnp.zeros((batch, MLSTM_HEADS, MLSTM_DV, MLSTM_DQK), F32),
              jnp.zeros((batch, MLSTM_HEADS, MLSTM_DQK), F32),
              jnp.full((batch, MLSTM_HEADS), NEG_BIG, F32))
    hf_lat, hf_ctx = run_direction(mlstm_scan, fwd_c, fwd_l, None, state0, False)
    hb_lat, hb_ctx = run_direction(mlstm_scan, bwd_c, bwd_l, None, state0, True)
    y_lat = head_norm(hf_lat + hb_lat, norm_w, False) * jax.nn.sigmoid(o_l)
    y_ctx = head_norm(hf_ctx + hb_ctx, norm_w, False) * jax.nn.sigmoid(o_c)
    return y_lat, y_ctx


def hgrn_group(cols_lat, cols_ctx, lb, norm_w):
    def prep(cols):
        q, i, f_fwd, f_bwd, g = cols
        q = to_heads(jax.nn.silu(q), HGRN_HEADS)
        v = to_heads(i, HGRN_HEADS)

        def gate(fp):
            f = lb + (1.0 - lb) * jax.nn.sigmoid(fp)
            return to_heads(1.0 - f, HGRN_HEADS), to_heads(jnp.log(f), HGRN_HEADS)

        k_f, lf_f = gate(f_fwd)
        k_b, lf_b = gate(f_bwd)
        return (q, k_f, v, lf_f), (q, k_b, v, lf_b), g

    fwd_l, bwd_l, g_l = prep(cols_lat)
    fwd_c, bwd_c, g_c = prep(cols_ctx)
    state0 = jnp.zeros((g_l.shape[0], HGRN_HEADS, HGRN_DK, HGRN_DV), F32)
    yf_lat, yf_ctx = run_direction(hgrn2_scan, fwd_c, fwd_l, None, state0, False)
    yb_lat, yb_ctx = run_direction(hgrn2_scan, bwd_c, bwd_l, None, state0, True)
    y_lat = head_norm(yf_lat + yb_lat, norm_w, True) * jax.nn.silu(g_l)
    y_ctx = head_norm(yf_ctx + yb_ctx, norm_w, True) * jax.nn.silu(g_c)
    return y_lat, y_ctx


def hyena_filters(length, fw1, fb1, fw2, fb2, fw3, fb3, freq, wout):
    pos = jnp.arange(length, dtype=F32)
    t = (pos / max(length - 1, 1))[:, None]
    bands = (HYENA_EMB - 1) // 2
    w = 2.0 * math.pi * pos[:, None] / length * jnp.linspace(1e-4, bands - 1, bands)[None, :]
    z = jnp.concatenate([t, jnp.cos(w), jnp.sin(w)], -1)
    hid = jnp.sin(freq * (z @ fw1 + fb1))
    hid = jnp.sin(freq * (hid @ fw2 + fb2))
    hid = jnp.sin(freq * (hid @ fw3 + fb3))
    h = (hid @ wout).reshape(length, HYENA_ORDER, 2, HYENA_WIDTH)
    deltas = jnp.abs(jnp.linspace(math.log(HYENA_TARGET) / HYENA_SLOW_PCT, math.log(HYENA_TARGET) / HYENA_FAST_PCT, HYENA_WIDTH))
    decay = jnp.exp(-t * deltas[None, :])
    return (h * decay[:, None, None, :]).transpose(1, 2, 0, 3)


def bidir_fftconv(u, h_fwd, h_bwd, d_skip):
    length, width = h_fwd.shape
    kern = jnp.concatenate([h_fwd, jnp.zeros((1, width), F32), jnp.flip(h_bwd[1:], axis=0)], axis=0)
    spec = jnp.fft.rfft(u, n=2 * length, axis=1) * jnp.fft.rfft(kern, axis=0)[None]
    y = jnp.fft.irfft(spec, n=2 * length, axis=1)[:, :length]
    return y + u * d_skip


def hyena_group(u_lat, u_ctx, conv_w, conv_b, fw1, fb1, fw2, fb2, fw3, fb3, freq, wout, skip, norm_w):
    def run(u):
        length = u.shape[1]
        up = jnp.pad(u, ((0, 0), (1, 1), (0, 0)))
        u = up[:, :-2] * conv_w[0] + up[:, 1:-1] * conv_w[1] + up[:, 2:] * conv_w[2] + conv_b
        streams = jnp.split(u, HYENA_ORDER + 1, axis=-1)
        gates, z = streams[:-1], streams[-1]
        h = hyena_filters(length, fw1, fb1, fw2, fb2, fw3, fb3, freq, wout)
        for n in range(HYENA_ORDER):
            z = gates[n] * bidir_fftconv(z, h[n, 0], h[n, 1], skip[n])
        return group_rms(z, norm_w)

    return run(u_lat), run(u_ctx)


def retention_group(cols_lat, cols_ctx, log_decay, norm_w, row, col):
    def prep(cols, rotary):
        q, k, v, g = cols
        b, l, _ = q.shape
        q = q.reshape(b, l, RET_HEADS, RET_DQK)
        k = k.reshape(b, l, RET_HEADS, RET_DQK)
        if rotary:
            q = axial_rope(q, row, col)
            k = axial_rope(k, row, col)
        q = q.transpose(0, 2, 1, 3)
        k = k.transpose(0, 2, 1, 3) * RET_DQK ** -0.5
        return (q, k, to_heads(v, RET_HEADS)), g

    lat, g_l = prep(cols_lat, True)
    ctx, g_c = prep(cols_ctx, False)
    state0 = jnp.zeros((g_l.shape[0], RET_HEADS, RET_DQK, RET_DV), F32)
    yf_lat, yf_ctx = run_direction(retention_scan, ctx, lat, log_decay[0], state0, False)
    yb_lat, yb_ctx = run_direction(retention_scan, ctx, lat, log_decay[1], state0, True)
    y_lat = head_norm(yf_lat + yb_lat, norm_w, False) * jax.nn.silu(g_l)
    y_ctx = head_norm(yf_ctx + yb_ctx, norm_w, False) * jax.nn.silu(g_c)
    return y_lat, y_ctx


def even_mixer(u_lat, u_ctx, w_in, gate_b, mlstm_norm_w, lb, hgrn_norm_w, w_out):
    w_in = w_in.astype(F32)
    w_out = w_out.astype(F32)
    p_lat = split_cols(mm3(u_lat.astype(F32), w_in), EVEN_SPLITS)
    p_ctx = split_cols(mm3(u_ctx.astype(F32), w_in), EVEN_SPLITS)
    n_a = len(MLSTM_SPLITS)
    a_lat, a_ctx = mlstm_group(p_lat[:n_a], p_ctx[:n_a], gate_b.astype(F32), mlstm_norm_w.astype(F32))
    b_lat, b_ctx = hgrn_group(p_lat[n_a:], p_ctx[n_a:], lb, hgrn_norm_w.astype(F32))
    return mm3(jnp.concatenate([a_lat, b_lat], -1), w_out), mm3(jnp.concatenate([a_ctx, b_ctx], -1), w_out)


def odd_mixer(u_lat, u_ctx, w_in, hyena_p, log_decay, ret_norm_w, w_out, row, col):
    w_in = w_in.astype(F32)
    w_out = w_out.astype(F32)
    p_lat = split_cols(mm3(u_lat.astype(F32), w_in), ODD_SPLITS)
    p_ctx = split_cols(mm3(u_ctx.astype(F32), w_in), ODD_SPLITS)
    hy_lat, hy_ctx = hyena_group(p_lat[0], p_ctx[0], *[p.astype(F32) for p in hyena_p])
    r_lat, r_ctx = retention_group(p_lat[1:], p_ctx[1:], log_decay.astype(F32), ret_norm_w.astype(F32), row, col)
    return mm3(jnp.concatenate([hy_lat, r_lat], -1), w_out), mm3(jnp.concatenate([hy_ctx, r_ctx], -1), w_out)


ROUTE_TB = 512
DISPATCH_TB = 512
COMBINE_TB = 256
FFN_BM = 512


def _dot(a, b):
    return jnp.dot(a, b, preferred_element_type=F32)


def _route_kernel(v_ref, rw_ref, rb_ref, e_ref, rk_ref, g_ref, cnt_ref, carry_ref):
    tb, n_e = v_ref.shape[0], rw_ref.shape[1]

    @pl.when(pl.program_id(0) == 0)
    def _():
        carry_ref[...] = jnp.zeros_like(carry_ref)

    x = v_ref[...]
    w = rw_ref[...]
    xh = x.astype(BF16)
    xl = (x - xh.astype(F32)).astype(BF16)
    wh = w.astype(BF16)
    wl = (w - wh.astype(F32)).astype(BF16)
    logits = _dot(xh, wh) + (_dot(xh, wl) + _dot(xl, wh))
    scores = jax.nn.sigmoid(logits)
    work = scores + rb_ref[...]
    lane = lax.broadcasted_iota(jnp.int32, (tb, n_e), 1)
    col = lax.broadcasted_iota(jnp.int32, (tb, TOP_K), 1)
    e_out = jnp.zeros((tb, TOP_K), jnp.int32)
    g_out = jnp.zeros((tb, TOP_K), F32)
    mask = jnp.zeros((tb, n_e), F32)
    onehots = []
    for j in range(TOP_K):
        m = jnp.max(work, axis=1, keepdims=True)
        idx = jnp.min(jnp.where(work == m, lane, n_e), axis=1, keepdims=True)
        oh = lane == idx
        gj = jnp.sum(jnp.where(oh, scores, 0.0), axis=1, keepdims=True)
        e_out = jnp.where(col == j, idx, e_out)
        g_out = jnp.where(col == j, gj, g_out)
        work = jnp.where(oh, -jnp.inf, work)
        mask = mask + oh.astype(F32)
        onehots.append(oh)
    r_i = lax.broadcasted_iota(jnp.int32, (tb, tb), 0)
    c_i = lax.broadcasted_iota(jnp.int32, (tb, tb), 1)
    tri = (r_i > c_i).astype(BF16)
    rank = _dot(tri, mask.astype(BF16)) + carry_ref[...]
    rk_out = jnp.zeros((tb, TOP_K), jnp.int32)
    for j in range(TOP_K):
        rkj = jnp.sum(jnp.where(onehots[j], rank, 0.0), axis=1, keepdims=True)
        rk_out = jnp.where(col == j, rkj.astype(jnp.int32), rk_out)
    carry_ref[...] = carry_ref[...] + jnp.sum(mask, axis=0, keepdims=True)
    e_ref[...] = e_out
    rk_ref[...] = rk_out
    g_ref[...] = ROUTED_SCALE * g_out / jnp.sum(g_out, axis=1, keepdims=True)
    cnt_ref[...] = carry_ref[...]


def moe_route(v, router_w, router_bias):
    n_tok, d = v.shape
    tb = ROUTE_TB
    return pl.pallas_call(
        _route_kernel,
        grid=(n_tok // tb,),
        in_specs=[pl.BlockSpec((tb, d), lambda i: (i, 0)),
                  pl.BlockSpec((d, N_EXPERTS), lambda i: (0, 0)),
                  pl.BlockSpec((1, N_EXPERTS), lambda i: (0, 0))],
        out_specs=[pl.BlockSpec((tb, TOP_K), lambda i: (i, 0)),
                   pl.BlockSpec((tb, TOP_K), lambda i: (i, 0)),
                   pl.BlockSpec((tb, TOP_K), lambda i: (i, 0)),
                   pl.BlockSpec((1, N_EXPERTS), lambda i: (0, 0))],
        out_shape=[jax.ShapeDtypeStruct((n_tok, TOP_K), jnp.int32),
                   jax.ShapeDtypeStruct((n_tok, TOP_K), jnp.int32),
                   jax.ShapeDtypeStruct((n_tok, TOP_K), F32),
                   jax.ShapeDtypeStruct((1, N_EXPERTS), F32)],
        scratch_shapes=[pltpu.VMEM((1, N_EXPERTS), F32)],
        compiler_params=pltpu.CompilerParams(dimension_semantics=("arbitrary",)),
        name="moe_route",
    )(v, router_w, router_bias.reshape(1, N_EXPERTS))


def _dispatch_kernel(zs_ref, zf_ref, nu_ref, pos_hbm, v_ref, xs_hbm, pos_smem, zbuf, sem_idx, sem_sc, sem_z):
    i = pl.program_id(0)
    tb = v_ref.shape[0]
    bm = zbuf.shape[0]
    n_blocks = xs_hbm.shape[0] // bm
    idx_cp = pltpu.make_async_copy(pos_hbm.at[i], pos_smem, sem_idx)
    idx_cp.start()

    @pl.when(i == 0)
    def _():
        zbuf[...] = jnp.zeros_like(zbuf)
        for e in range(N_EXPERTS):
            @pl.when(zf_ref[e] != 0)
            def _():
                start = pl.multiple_of(zs_ref[e], bm)
                pltpu.make_async_copy(zbuf, xs_hbm.at[pl.ds(start, bm)], sem_z).start()
        for e in range(N_EXPERTS):
            @pl.when(zf_ref[e] != 0)
            def _():
                pltpu.make_async_copy(zbuf, xs_hbm.at[pl.ds(0, bm)], sem_z).wait()

        def tail(b, carry):
            cp = pltpu.make_async_copy(zbuf, xs_hbm.at[pl.ds(pl.multiple_of(b * bm, bm), bm)], sem_z)
            cp.start()
            cp.wait()
            return carry

        lax.fori_loop(nu_ref[0], n_blocks, tail, 0)

    idx_cp.wait()

    def body(t, carry):
        for k in range(TOP_K):
            p = pos_smem[0, t * TOP_K + k]
            pltpu.make_async_copy(v_ref.at[pl.ds(t, 1)], xs_hbm.at[pl.ds(p, 1)], sem_sc).start()
        return carry

    lax.fori_loop(0, tb, body, 0)
    for k in range(TOP_K):
        pltpu.make_async_copy(v_ref, v_ref, sem_sc).wait()


def moe_dispatch(v, pos3, zero_start, zero_flag, n_used, n_rows):
    n_tok, d = v.shape
    tb = DISPATCH_TB
    grid_spec = pltpu.PrefetchScalarGridSpec(
        num_scalar_prefetch=3,
        grid=(n_tok // tb,),
        in_specs=[pl.BlockSpec(memory_space=pl.ANY),
                  pl.BlockSpec((tb, d), lambda i, zs, zf, nu: (i, 0))],
        out_specs=pl.BlockSpec(memory_space=pl.ANY),
        scratch_shapes=[pltpu.SMEM((1, tb * TOP_K), jnp.int32),
                        pltpu.VMEM((FFN_BM, d), F32),
                        pltpu.SemaphoreType.DMA(()),
                        pltpu.SemaphoreType.DMA(()),
                        pltpu.SemaphoreType.DMA(())],
    )
    return pl.pallas_call(
        _dispatch_kernel,
        grid_spec=grid_spec,
        out_shape=jax.ShapeDtypeStruct((n_rows, d), F32),
        compiler_params=pltpu.CompilerParams(dimension_semantics=("arbitrary",)),
        name="moe_dispatch",
    )(zero_start, zero_flag, n_used, pos3, v)


def _ffn_kernel(be_ref, nu_ref, xs_ref, w1_ref, w3_ref, w2_ref, y_ref):
    used = pl.program_id(0) < nu_ref[0]

    @pl.when(used)
    def _():
        x = xs_ref[...].astype(BF16)
        h1 = _dot(x, w1_ref[0])
        h3 = _dot(x, w3_ref[0])
        h = (h1 * jax.nn.sigmoid(h1) * h3).astype(BF16)
        y_ref[...] = _dot(h, w2_ref[0])

    @pl.when(jnp.logical_not(used))
    def _():
        y_ref[...] = jnp.zeros_like(y_ref)


def moe_expert_ffn(xs, block_e, n_used, w1, w3, w2):
    n_rows, d = xs.shape
    bm = FFN_BM
    hid = w1.shape[2]

    def row_map(i, be, nu):
        return (jnp.minimum(i, nu[0] - 1), 0)

    def w_map(i, be, nu):
        return (be[jnp.minimum(i, nu[0] - 1)], 0, 0)

    grid_spec = pltpu.PrefetchScalarGridSpec(
        num_scalar_prefetch=2,
        grid=(n_rows // bm,),
        in_specs=[pl.BlockSpec((bm, d), row_map),
                  pl.BlockSpec((1, d, hid), w_map),
                  pl.BlockSpec((1, d, hid), w_map),
                  pl.BlockSpec((1, hid, d), w_map)],
        out_specs=pl.BlockSpec((bm, d), lambda i, be, nu: (i, 0)),
    )
    return pl.pallas_call(
        _ffn_kernel,
        grid_spec=grid_spec,
        out_shape=jax.ShapeDtypeStruct((n_rows, d), F32),
        compiler_params=pltpu.CompilerParams(dimension_semantics=("arbitrary",)),
        name="moe_expert_ffn",
    )(block_e, n_used, xs, w1, w3, w2)


def _combine_kernel(pos_hbm, y_hbm, v_ref, g_ref, sw1_ref, sw3_ref, sw2_ref, f_ref, pos_smem, ybuf, sem_idx, sem_y):
    i = pl.program_id(0)
    n = pl.num_programs(0)
    tb = v_ref.shape[0]
    slot = lax.rem(i, 2)

    def fetch_idx(step, s):
        return pltpu.make_async_copy(pos_hbm.at[step], pos_smem.at[s], sem_idx.at[s])

    def issue_gather(s):
        def body(t, carry):
            for k in range(TOP_K):
                p = pos_smem[s, 0, t * TOP_K + k]
                pltpu.make_async_copy(y_hbm.at[pl.ds(p, 1)], ybuf.at[s, k, pl.ds(t, 1)], sem_y.at[s]).start()
            return carry
        lax.fori_loop(0, tb, body, 0)

    @pl.when(i == 0)
    def _():
        cp = fetch_idx(0, 0)
        cp.start()
        cp.wait()
        issue_gather(0)

    @pl.when(i + 1 < n)
    def _():
        cp = fetch_idx(i + 1, 1 - slot)
        cp.start()
        cp.wait()
        issue_gather(1 - slot)

    x = v_ref[...].astype(BF16)
    h1 = _dot(x, sw1_ref[...])
    h3 = _dot(x, sw3_ref[...])
    acc = _dot((h1 * jax.nn.sigmoid(h1) * h3).astype(BF16), sw2_ref[...])
    for k in range(TOP_K):
        pltpu.make_async_copy(ybuf.at[slot, k], ybuf.at[slot, k], sem_y.at[slot]).wait()
    g = g_ref[...]
    for k in range(TOP_K):
        acc = acc + g[:, k:k + 1] * ybuf[slot, k]
    f_ref[...] = acc


def moe_combine(pos3, y, v, gate, sw1, sw3, sw2):
    n_tok, d = v.shape
    tb = COMBINE_TB
    hid = sw1.shape[1]
    return pl.pallas_call(
        _combine_kernel,
        grid=(n_tok // tb,),
        in_specs=[pl.BlockSpec(memory_space=pl.ANY),
                  pl.BlockSpec(memory_space=pl.ANY),
                  pl.BlockSpec((tb, d), lambda i: (i, 0)),
                  pl.BlockSpec((tb, TOP_K), lambda i: (i, 0)),
                  pl.BlockSpec((d, hid), lambda i: (0, 0)),
                  pl.BlockSpec((d, hid), lambda i: (0, 0)),
                  pl.BlockSpec((hid, d), lambda i: (0, 0))],
        out_specs=pl.BlockSpec((tb, d), lambda i: (i, 0)),
        out_shape=jax.ShapeDtypeStruct((n_tok, d), F32),
        scratch_shapes=[pltpu.SMEM((2, 1, tb * TOP_K), jnp.int32),
                        pltpu.VMEM((2, TOP_K, tb, d), F32),
                        pltpu.SemaphoreType.DMA((2,)),
                        pltpu.SemaphoreType.DMA((2,))],
        compiler_params=pltpu.CompilerParams(dimension_semantics=("arbitrary",), vmem_limit_bytes=48 * 1024 * 1024),
        name="moe_combine",
    )(pos3, y, v, gate, sw1, sw3, sw2)


def moe_ffn(t, router_w, router_bias, w1, w3, w2, sw1, sw3, sw2):
    n_tok, d = t.shape
    tf = t.astype(F32)
    bm = FFN_BM
    e_sel, rank, gate, cnt = moe_route(tf, router_w.astype(F32), router_bias.astype(F32))
    counts = cnt[0].astype(jnp.int32)
    padded = (counts + bm - 1) // bm * bm
    pend = jnp.cumsum(padded)
    pstart = pend - padded
    n_blocks = -(-(n_tok * TOP_K + N_EXPERTS * (bm - 1)) // bm)
    expert_ids = jnp.arange(N_EXPERTS, dtype=jnp.int32)
    pos = rank + jnp.sum(jnp.where(e_sel[..., None] == expert_ids, pstart, 0), -1)
    block_e = jnp.minimum(jnp.searchsorted(pend, jnp.arange(n_blocks, dtype=jnp.int32) * bm, side='right'), N_EXPERTS - 1).astype(jnp.int32)
    n_used = (pend[-1:] // bm).astype(jnp.int32)
    zero_start = jnp.maximum(pend - bm, 0).astype(jnp.int32)
    zero_flag = (counts > 0).astype(jnp.int32)
    xs = moe_dispatch(tf, pos.reshape(n_tok // DISPATCH_TB, 1, DISPATCH_TB * TOP_K), zero_start, zero_flag, n_used, n_blocks * bm)
    y = moe_expert_ffn(xs, block_e, n_used, w1.astype(BF16), w3.astype(BF16), w2.astype(BF16))
    return moe_combine(pos.reshape(n_tok // COMBINE_TB, 1, COMBINE_TB * TOP_K), y, tf, gate,
                       sw1.astype(BF16), sw3.astype(BF16), sw2.astype(BF16))


HEAD_W = 128
MIX_VMEM = 48 * 1024 * 1024
NORM_ROWS = 256


def _dot_nt(a, b):
    return lax.dot_general(a, b, (((1,), (1,)), ((), ())), preferred_element_type=F32)


def _dot_tn(a, b):
    return lax.dot_general(a, b, (((0,), (0,)), ((), ())), preferred_element_type=F32)


def _dot_hi(a, b):
    return jnp.dot(a, b, preferred_element_type=F32, precision=lax.Precision.HIGHEST)


def _tri(n, upper):
    r = lax.broadcasted_iota(jnp.int32, (n, n), 0)
    c = lax.broadcasted_iota(jnp.int32, (n, n), 1)
    return (r <= c) if upper else (r >= c)


def _bidir_loop(n_ctx_chunks, n_chunks, step, unroll):
    def body(c, carry):
        cb = jnp.where(c < n_ctx_chunks, n_ctx_chunks - 1 - c, n_chunks - 1 + n_ctx_chunks - c)
        step(c, False)
        step(cb, True)
        return carry
    lax.fori_loop(0, n_chunks, body, 0, unroll=unroll)


def _norm_gate_epilogue(hf_ref, hb_ref, gate_ref, nw_ref, y_ref, n_rows, rms, gate_fn):
    def body(i, carry):
        rows = pl.ds(pl.multiple_of(i * NORM_ROWS, NORM_ROWS), NORM_ROWS)
        h = hf_ref[rows, :] + hb_ref[rows, :]
        if not rms:
            h = h - jnp.mean(h, axis=1, keepdims=True)
        h = h * lax.rsqrt(jnp.mean(h * h, axis=1, keepdims=True) + EPS)
        y_ref[rows, :] = (h * nw_ref[...] * gate_fn(gate_ref[rows, :].astype(F32))).astype(y_ref.dtype)
        return carry
    lax.fori_loop(0, n_rows // NORM_ROWS, body, 0)


def _silu(x):
    return x * jax.nn.sigmoid(x)


def _retention_kernel(ld_ref, q_ref, k_ref, v_ref, g_ref, cos_ref, sin_ref, perm_ref, nw_ref, y_ref,
                      qr_ref, kr_ref, hf_ref, hb_ref, sf_ref, sb_ref, *, n_ctx):
    cs = RET_CHUNK
    lt = q_ref.shape[0]
    hd = pl.program_id(1)

    def rope(i, carry):
        rows = pl.ds(pl.multiple_of(i * NORM_ROWS, NORM_ROWS), NORM_ROWS)
        cs_, sn_ = cos_ref[rows, :], sin_ref[rows, :]
        q = q_ref[rows, :]
        k = k_ref[rows, :]
        qr_ref[rows, :] = (q.astype(F32) * cs_ + _dot(q, perm_ref[...]) * sn_).astype(BF16)
        kr_ref[rows, :] = ((k.astype(F32) * cs_ + _dot(k, perm_ref[...]) * sn_) * RET_DQK ** -0.5).astype(BF16)
        return carry
    lax.fori_loop(0, lt // NORM_ROWS, rope, 0)

    pos_r = lax.broadcasted_iota(jnp.int32, (cs, cs), 0).astype(F32)
    pos_c = lax.broadcasted_iota(jnp.int32, (cs, cs), 1).astype(F32)
    pos_t = lax.broadcasted_iota(jnp.int32, (cs, HEAD_W), 0).astype(F32)
    consts = []
    for d in range(2):
        lg = ld_ref[d, hd]
        if d == 0:
            decay = jnp.where(pos_r >= pos_c, jnp.exp((pos_r - pos_c) * lg), 0.0)
            q_dec = jnp.exp((pos_t + 1.0) * lg)
            k_dec = jnp.exp((cs - 1.0 - pos_t) * lg)
        else:
            decay = jnp.where(pos_r <= pos_c, jnp.exp((pos_c - pos_r) * lg), 0.0)
            q_dec = jnp.exp((cs - pos_t) * lg)
            k_dec = jnp.exp(pos_t * lg)
        consts.append((decay, q_dec, k_dec, jnp.exp(cs * lg)))
    sf_ref[...] = jnp.zeros_like(sf_ref)
    sb_ref[...] = jnp.zeros_like(sb_ref)

    def step(chunk, reverse):
        decay, q_dec, k_dec, c_dec = consts[1 if reverse else 0]
        s_ref = sb_ref if reverse else sf_ref
        h_ref = hb_ref if reverse else hf_ref
        rows = pl.ds(pl.multiple_of(chunk * cs, cs), cs)
        q = qr_ref[rows, :]
        k = kr_ref[rows, :]
        v = v_ref[rows, :]
        sc = _dot_nt(q, k) * decay
        s_in = s_ref[...]
        inter = _dot((q.astype(F32) * q_dec).astype(BF16), s_in.astype(BF16))
        h_ref[rows, :] = inter + _dot(sc.astype(BF16), v)
        s_ref[...] = c_dec * s_in + _dot_tn((k.astype(F32) * k_dec).astype(BF16), v)

    _bidir_loop(n_ctx // cs, lt // cs, step, 2)
    _norm_gate_epilogue(hf_ref, hb_ref, g_ref, nw_ref, y_ref, lt, False, _silu)


def retention_mixer(p16, col0, log_decay, norm_w, cos_t, sin_t, batch, n_ctx):
    lt = p16.shape[0] // batch
    nh = RET_HEADS
    half = RET_DQK // 2
    src = jnp.arange(HEAD_W)
    swapped = jnp.where((src % half) < half // 2, src + half // 2, src - half // 2)
    perm = (src[:, None] == swapped[None, :]).astype(BF16)

    def blk(off):
        return pl.BlockSpec((lt, HEAD_W), lambda b, h, ld: (b, off + h))

    grid_spec = pltpu.PrefetchScalarGridSpec(
        num_scalar_prefetch=1,
        grid=(batch, nh),
        in_specs=[blk(col0), blk(col0 + nh), blk(col0 + 2 * nh), blk(col0 + 3 * nh),
                  pl.BlockSpec((lt, HEAD_W), lambda b, h, ld: (0, 0)),
                  pl.BlockSpec((lt, HEAD_W), lambda b, h, ld: (0, 0)),
                  pl.BlockSpec((HEAD_W, HEAD_W), lambda b, h, ld: (0, 0)),
                  pl.BlockSpec((1, HEAD_W), lambda b, h, ld: (0, h))],
        out_specs=pl.BlockSpec((lt, HEAD_W), lambda b, h, ld: (b, h)),
        scratch_shapes=[pltpu.VMEM((lt, HEAD_W), BF16), pltpu.VMEM((lt, HEAD_W), BF16),
                        pltpu.VMEM((lt, HEAD_W), F32), pltpu.VMEM((lt, HEAD_W), F32),
                        pltpu.VMEM((RET_DQK, RET_DV), F32), pltpu.VMEM((RET_DQK, RET_DV), F32)],
    )
    return pl.pallas_call(
        functools.partial(_retention_kernel, n_ctx=n_ctx),
        grid_spec=grid_spec,
        out_shape=jax.ShapeDtypeStruct((batch * lt, nh * HEAD_W), BF16),
        compiler_params=pltpu.CompilerParams(dimension_semantics=("arbitrary", "arbitrary"), vmem_limit_bytes=MIX_VMEM),
        name="retention_mixer",
    )(log_decay.astype(F32), p16, p16, p16, p16, cos_t, sin_t, perm, norm_w.reshape(1, -1).astype(F32))


def _hgrn_kernel(q_ref, i_ref, g_ref, ff_ref, fb_ref, lb_ref, nw_ref, y_ref, hf_ref, hb_ref, sf_ref, sb_ref, *, n_ctx):
    cs = HGRN_CHUNK
    lt = q_ref.shape[0]
    lb = lb_ref[...]
    sf_ref[...] = jnp.zeros_like(sf_ref)
    sb_ref[...] = jnp.zeros_like(sb_ref)
    tri = (_tri(cs, False).astype(F32), _tri(cs, True).astype(F32))
    mask = (_tri(cs, False), _tri(cs, True))

    def step(chunk, reverse):
        d = 1 if reverse else 0
        f_ref = fb_ref if reverse else ff_ref
        s_ref = sb_ref if reverse else sf_ref
        h_ref = hb_ref if reverse else hf_ref
        rows = pl.ds(pl.multiple_of(chunk * cs, cs), cs)
        f = lb + (1.0 - lb) * jax.nn.sigmoid(f_ref[rows, :])
        kk = 1.0 - f
        a_cum = _dot_hi(tri[d], jnp.log(f))
        a_end = a_cum[0:1, :] if reverse else a_cum[cs - 1:cs, :]
        mid = cs // 2 if reverse else cs // 2 - 1
        a_mid = a_cum[mid:mid + 1, :]
        q = _silu(q_ref[rows, :].astype(F32))
        v = i_ref[rows, :]
        s_in = s_ref[...]
        inter = _dot_nt((q * jnp.exp(a_cum)).astype(BF16), s_in.astype(BF16))
        sc = _dot_nt((q * jnp.exp(a_cum - a_mid)).astype(BF16), (kk * jnp.exp(a_mid - a_cum)).astype(BF16))
        sc = jnp.where(mask[d], sc, 0.0)
        h_ref[rows, :] = inter + _dot(sc.astype(BF16), v)
        s_ref[...] = s_in * jnp.exp(a_end) + _dot_tn(v, (kk * jnp.exp(a_end - a_cum)).astype(BF16))

    _bidir_loop(n_ctx // cs, lt // cs, step, 2)
    _norm_gate_epilogue(hf_ref, hb_ref, g_ref, nw_ref, y_ref, lt, True, _silu)


EVEN16_MLSTM_Q, EVEN16_MLSTM_K, EVEN16_MLSTM_V, EVEN16_MLSTM_O = 0, 2, 4, 8
EVEN16_HGRN_Q, EVEN16_HGRN_I, EVEN16_HGRN_G = 12, 16, 20
EVEN32_GATES, EVEN32_FF, EVEN32_FB = 0, 1, 5


def hgrn_mixer(p16, p32, lb, norm_w, batch, n_ctx):
    lt = p16.shape[0] // batch
    nh = HGRN_HEADS

    def blk(off):
        return pl.BlockSpec((lt, HEAD_W), lambda b, h: (b, off + h))

    def vec():
        return pl.BlockSpec((1, HEAD_W), lambda b, h: (0, h))

    return pl.pallas_call(
        functools.partial(_hgrn_kernel, n_ctx=n_ctx),
        grid=(batch, nh),
        in_specs=[blk(EVEN16_HGRN_Q), blk(EVEN16_HGRN_I), blk(EVEN16_HGRN_G), blk(EVEN32_FF), blk(EVEN32_FB), vec(), vec()],
        out_specs=pl.BlockSpec((lt, HEAD_W), lambda b, h: (b, h)),
        out_shape=jax.ShapeDtypeStruct((batch * lt, nh * HEAD_W), BF16),
        scratch_shapes=[pltpu.VMEM((lt, HEAD_W), F32), pltpu.VMEM((lt, HEAD_W), F32),
                        pltpu.VMEM((HGRN_DV, HGRN_DK), F32), pltpu.VMEM((HGRN_DV, HGRN_DK), F32)],
        compiler_params=pltpu.CompilerParams(dimension_semantics=("arbitrary", "arbitrary"), vmem_limit_bytes=MIX_VMEM),
        name="hgrn_mixer",
    )(p16, p16, p16, p32, p32, lb.reshape(1, -1).astype(F32), norm_w.reshape(1, -1).astype(F32))


def _mlstm_kernel(q_ref, k_ref, v_ref, o_ref, gt_ref, gb_ref, nw_ref, y_ref, hf_ref, hb_ref, cf_ref, cb_ref, mf_ref, mb_ref, *, n_ctx):
    cs = MLSTM_CHUNK
    lt = q_ref.shape[0]
    hd = pl.program_id(1)
    lane = lax.broadcasted_iota(jnp.int32, (1, HEAD_W), 1)
    head_mask = ((lane // MLSTM_DQK) == (hd % 2)).astype(F32)
    gcol = lax.broadcasted_iota(jnp.int32, (1, HEAD_W), 1)
    tri = (_tri(cs, False).astype(F32), _tri(cs, True).astype(F32))
    mask = (_tri(cs, False), _tri(cs, True))
    ones_v = jnp.ones((cs, HEAD_W), BF16)
    for ref in (cf_ref, cb_ref):
        ref[...] = jnp.zeros_like(ref)
    for ref in (mf_ref, mb_ref):
        ref[...] = jnp.full_like(ref, NEG_BIG)

    def pick(tile, col):
        return jnp.sum(jnp.where(gcol == col, tile, 0.0), axis=1, keepdims=True)

    def step(chunk, reverse):
        d = 1 if reverse else 0
        c_ref = cb_ref if reverse else cf_ref
        m_ref = mb_ref if reverse else mf_ref
        h_ref = hb_ref if reverse else hf_ref
        rows = pl.ds(pl.multiple_of(chunk * cs, cs), cs)
        x = gt_ref[rows, :] + gb_ref[...]
        ci = hd + (2 * MLSTM_HEADS if reverse else 0)
        ig = pick(x, ci)
        lf = pick(jax.nn.log_sigmoid(x), ci + MLSTM_HEADS)
        lf_t = jnp.broadcast_to(lf, (cs, HEAD_W))
        b = _dot_hi(tri[d], lf_t)
        g = b[0:1, :] if reverse else b[cs - 1:cs, :]
        ig_t = jnp.broadcast_to(ig, (cs, HEAD_W))
        a = g - b + ig_t
        m_loc = jnp.max(a, axis=0, keepdims=True)
        w = jnp.exp(a - m_loc)
        m_in = m_ref[...]
        r_row = jnp.transpose(ig_t - b)[0:1, 0:cs]
        log_d = jnp.where(mask[d], b[:, 0:cs] + r_row, -jnp.inf)
        inter = b + m_in
        m_out = jnp.maximum(inter, jnp.max(log_d, axis=1, keepdims=True))
        q = q_ref[rows, :]
        k = k_ref[rows, :]
        v = v_ref[rows, :]
        qm = (q.astype(F32) * (head_mask * MLSTM_DQK ** -0.5)).astype(BF16)
        s = _dot_nt(qm, k) * jnp.exp(log_d - m_out[:, 0:cs])
        e_int = jnp.exp(inter - m_out)
        vo = jnp.concatenate([v, ones_v], axis=1)
        c_in = c_ref[...]
        r = _dot(s.astype(BF16), vo) + jnp.concatenate([e_int, e_int], axis=1) * _dot(qm, c_in.astype(BF16))
        num, den = r[:, :HEAD_W], r[:, HEAD_W:]
        h_ref[rows, :] = num / jnp.maximum(jnp.abs(den), jnp.exp(-m_out))
        m_new = jnp.maximum(g + m_in, m_loc)
        sp = jnp.exp(g + m_in - m_new)
        sl = jnp.exp(m_loc - m_new)
        kw = (k.astype(F32) * head_mask * w).astype(BF16)
        c_loc = _dot_tn(kw, vo)
        c_ref[...] = jnp.concatenate([sp, sp], axis=1) * c_in + jnp.concatenate([sl, sl], axis=1) * c_loc
        m_ref[...] = m_new

    _bidir_loop(n_ctx // cs, lt // cs, step, 2)
    _norm_gate_epilogue(hf_ref, hb_ref, o_ref, nw_ref, y_ref, lt, False, jax.nn.sigmoid)


def mlstm_mixer(p16, p32, gate_b, norm_w, batch, n_ctx):
    lt = p16.shape[0] // batch
    nh = MLSTM_HEADS
    gb = jnp.zeros((1, HEAD_W), F32).at[0, :4 * nh].set(gate_b.astype(F32).reshape(-1))

    return pl.pallas_call(
        functools.partial(_mlstm_kernel, n_ctx=n_ctx),
        grid=(batch, nh),
        in_specs=[pl.BlockSpec((lt, HEAD_W), lambda b, h: (b, EVEN16_MLSTM_Q + h // 2)),
                  pl.BlockSpec((lt, HEAD_W), lambda b, h: (b, EVEN16_MLSTM_K + h // 2)),
                  pl.BlockSpec((lt, HEAD_W), lambda b, h: (b, EVEN16_MLSTM_V + h)),
                  pl.BlockSpec((lt, HEAD_W), lambda b, h: (b, EVEN16_MLSTM_O + h)),
                  pl.BlockSpec((lt, HEAD_W), lambda b, h: (b, EVEN32_GATES)),
                  pl.BlockSpec((1, HEAD_W), lambda b, h: (0, 0)),
                  pl.BlockSpec((1, HEAD_W), lambda b, h: (0, h))],
        out_specs=pl.BlockSpec((lt, HEAD_W), lambda b, h: (b, h)),
        out_shape=jax.ShapeDtypeStruct((batch * lt, nh * HEAD_W), BF16),
        scratch_shapes=[pltpu.VMEM((lt, HEAD_W), F32), pltpu.VMEM((lt, HEAD_W), F32),
                        pltpu.VMEM((HEAD_W, 2 * HEAD_W), F32), pltpu.VMEM((HEAD_W, 2 * HEAD_W), F32),
                        pltpu.VMEM((1, HEAD_W), F32), pltpu.VMEM((1, HEAD_W), F32)],
        compiler_params=pltpu.CompilerParams(dimension_semantics=("arbitrary", "arbitrary"), vmem_limit_bytes=MIX_VMEM),
        name="mlstm_mixer",
    )(p16, p16, p16, p16, p32, gb, norm_w.reshape(1, -1).astype(F32))


ROW_BLOCKS_PER_BATCH_IN = 4
ROW_BLOCKS_PER_BATCH_OUT = 8
PROJ_TN = 512


def _row_mod(mod_ref, idx, is_ctx):
    return jnp.where(is_ctx, mod_ref[0, 0, idx:idx + 1, :], mod_ref[0, 1, idx:idx + 1, :])


def _is_ctx_rows(tm, blocks_per_batch, n_ctx):
    row = (pl.program_id(0) % blocks_per_batch) * tm + lax.broadcasted_iota(jnp.int32, (tm, 1), 0)
    return row < n_ctx


def _proj_in_kernel(h_ref, mod_ref, w_ref, o_ref, u_ref, *, n_ctx, blocks_per_batch):
    tm = h_ref.shape[0]

    @pl.when(pl.program_id(1) == 0)
    def _():
        is_ctx = _is_ctx_rows(tm, blocks_per_batch, n_ctx)
        u = h_ref[...] * (1.0 + _row_mod(mod_ref, 1, is_ctx)) + _row_mod(mod_ref, 0, is_ctx)
        u_ref[...] = u.astype(BF16)

    o_ref[...] = _dot(u_ref[...], w_ref[...]).astype(o_ref.dtype)


def proj_in(h, modtab, w, out_dtype, batch, n_ctx, tn=PROJ_TN):
    t, d = h.shape
    n = w.shape[1]
    bpb = ROW_BLOCKS_PER_BATCH_IN
    tm = t // batch // bpb
    return pl.pallas_call(
        functools.partial(_proj_in_kernel, n_ctx=n_ctx, blocks_per_batch=bpb),
        grid=(t // tm, n // tn),
        in_specs=[pl.BlockSpec((tm, d), lambda i, j: (i, 0)),
                  pl.BlockSpec((1, 2, 6, d), lambda i, j: (i // bpb, 0, 0, 0)),
                  pl.BlockSpec((d, tn), lambda i, j: (0, j))],
        out_specs=pl.BlockSpec((tm, tn), lambda i, j: (i, j)),
        out_shape=jax.ShapeDtypeStruct((t, n), out_dtype),
        scratch_shapes=[pltpu.VMEM((tm, d), BF16)],
        compiler_params=pltpu.CompilerParams(dimension_semantics=("arbitrary", "arbitrary"), vmem_limit_bytes=MIX_VMEM),
        name="proj_in",
    )(h, modtab, w)


def _layer_norm_rows(z, w, b):
    mu = jnp.mean(z, axis=1, keepdims=True)
    zc = z - mu
    var = jnp.mean(zc * zc, axis=1, keepdims=True)
    return zc * lax.rsqrt(var + EPS) * w + b


def _proj_out_kernel(ya_ref, yb_ref, wa_ref, wb_ref, h_ref, mod_ref, lnw_ref, lnb_ref, h1_ref, v_ref, *, n_ctx, blocks_per_batch):
    tm = h_ref.shape[0]
    is_ctx = _is_ctx_rows(tm, blocks_per_batch, n_ctx)
    y = _dot(ya_ref[...], wa_ref[...]) + _dot(yb_ref[...], wb_ref[...])
    z = DEEPNORM_ALPHA * h_ref[...] + _row_mod(mod_ref, 2, is_ctx) * y
    h1 = _layer_norm_rows(z, lnw_ref[...], lnb_ref[...])
    h1_ref[...] = h1
    v_ref[...] = h1 * (1.0 + _row_mod(mod_ref, 4, is_ctx)) + _row_mod(mod_ref, 3, is_ctx)


def proj_out_ln(ya, yb, wa, wb, h, modtab, ln_w, ln_b, batch, n_ctx):
    t, d = h.shape
    bpb = ROW_BLOCKS_PER_BATCH_OUT
    tm = t // batch // bpb
    ka, kb = ya.shape[1], yb.shape[1]
    row = lambda i: (i, 0)
    fixed = lambda i: (0, 0)
    return pl.pallas_call(
        functools.partial(_proj_out_kernel, n_ctx=n_ctx, blocks_per_batch=bpb),
        grid=(t // tm,),
        in_specs=[pl.BlockSpec((tm, ka), row), pl.BlockSpec((tm, kb), row),
                  pl.BlockSpec((ka, d), fixed), pl.BlockSpec((kb, d), fixed),
                  pl.BlockSpec((tm, d), row),
                  pl.BlockSpec((1, 2, 6, d), lambda i: (i // bpb, 0, 0, 0)),
                  pl.BlockSpec((1, d), fixed), pl.BlockSpec((1, d), fixed)],
        out_specs=[pl.BlockSpec((tm, d), row), pl.BlockSpec((tm, d), row)],
        out_shape=[jax.ShapeDtypeStruct((t, d), F32), jax.ShapeDtypeStruct((t, d), F32)],
        compiler_params=pltpu.CompilerParams(dimension_semantics=("arbitrary",), vmem_limit_bytes=MIX_VMEM),
        name="proj_out_ln",
    )(ya, yb, wa, wb, h, modtab, ln_w.reshape(1, d).astype(F32), ln_b.reshape(1, d).astype(F32))


FFT_N2 = 128
HY_LANES = 128
HY_VMEM = 56 * 1024 * 1024


def _np_cplx_tables(n_len):
    import numpy as np
    n = 2 * n_len
    n2 = FFT_N2
    n1 = n // n2
    w = lambda m, e: np.exp(-2j * np.pi * (e % m) / m)
    k1 = np.arange(n1)
    a = np.arange(n1)
    j2 = np.arange(n2)
    ta = w(n, j2[:, None, None] * k1[None, :, None]) * w(n1, k1[None, :, None] * a[None, None, :])
    ta_stack = np.concatenate([ta.real, ta.imag], axis=1)
    tinv = np.conj(ta).transpose(0, 2, 1) / n
    tinv_stack = np.concatenate([tinv.real, -tinv.imag], axis=2)[:, :n1 // 2]
    f2 = w(n2, j2[:, None] * j2[None, :])
    sb = np.block([[f2.real, -f2.imag], [f2.imag, f2.real]])
    sbi = np.block([[f2.real, f2.imag], [-f2.imag, f2.real]])
    return ta_stack, tinv_stack, sb, sbi


def _np_dense_dft(n_len):
    import numpy as np
    n = 2 * n_len
    k = np.arange(n)
    f = np.exp(-2j * np.pi * ((k[:, None] * k[None, :]) % n) / n)
    fwd = np.concatenate([f.real, f.imag], axis=0)
    g = np.conj(f)[:n_len] / n
    inv = np.concatenate([g.real, -g.imag], axis=1)
    return fwd, inv


def _fft_stage_a(load_rows, ta_ref, b_ref, n1_in, n1):
    def body(j, carry):
        x = load_rows(j, n1_in).astype(BF16)
        a = _dot(ta_ref[j], x)
        b_ref[0, pl.ds(j, n1, stride=FFT_N2), :] = a[:n1]
        b_ref[1, pl.ds(j, n1, stride=FFT_N2), :] = a[n1:]
        return carry
    lax.fori_loop(0, FFT_N2, body, 0, unroll=2)


def _fft_stage_b_rows(b_ref, sb_ref, k1):
    rows = pl.ds(pl.multiple_of(k1 * FFT_N2, FFT_N2), FFT_N2)
    rhs = jnp.concatenate([b_ref[0, rows, :], b_ref[1, rows, :]], axis=0).astype(BF16)
    return rows, _dot(sb_ref[...], rhs)


def _hyena_spectrum_kernel(k_ref, ta_ref, sb_ref, o_ref, b_ref):
    n1 = b_ref.shape[1] // FFT_N2
    _fft_stage_a(lambda j, cnt: k_ref[0, pl.ds(j, cnt, stride=FFT_N2), :], ta_ref, b_ref, n1, n1)

    def body(k1, carry):
        rows, x = _fft_stage_b_rows(b_ref, sb_ref, k1)
        o_ref[0, 0, rows, :] = x[:FFT_N2]
        o_ref[0, 1, rows, :] = x[FFT_N2:]
        return carry
    lax.fori_loop(0, n1, body, 0)


def hyena_spectrum(kern):
    n_ord, n, width = kern.shape
    n1 = n // FFT_N2
    ta, _, sb, _ = _np_cplx_tables(n // 2)
    return pl.pallas_call(
        _hyena_spectrum_kernel,
        grid=(n_ord, width // HY_LANES),
        in_specs=[pl.BlockSpec((1, n, HY_LANES), lambda o, c: (o, 0, c)),
                  pl.BlockSpec((FFT_N2, 2 * n1, n1), lambda o, c: (0, 0, 0)),
                  pl.BlockSpec((2 * FFT_N2, 2 * FFT_N2), lambda o, c: (0, 0))],
        out_specs=pl.BlockSpec((1, 2, n, HY_LANES), lambda o, c: (o, 0, 0, c)),
        out_shape=jax.ShapeDtypeStruct((n_ord, 2, n, width), F32),
        scratch_shapes=[pltpu.VMEM((2, n, HY_LANES), F32)],
        compiler_params=pltpu.CompilerParams(dimension_semantics=("arbitrary", "arbitrary"), vmem_limit_bytes=HY_VMEM),
        name="hyena_spectrum",
    )(kern, jnp.asarray(ta, BF16), jnp.asarray(sb, BF16))


def _short_conv(x_ref, w_ref, b_ref, n_ctx):
    x = x_ref[...].astype(F32)
    lt = x.shape[0]
    row = lax.broadcasted_iota(jnp.int32, (lt, 1), 0)
    prev = jnp.where((row == 0) | (row == n_ctx), 0.0, pltpu.roll(x, 1, axis=0))
    nxt = jnp.where((row == n_ctx - 1) | (row == lt - 1), 0.0, pltpu.roll(x, lt - 1, axis=0))
    return prev * w_ref[0:1, :] + x * w_ref[1:2, :] + nxt * w_ref[2:3, :] + b_ref[...]


def _hyena_conv_kernel(z_ref, g_ref, wz_ref, bz_ref, wg_ref, bg_ref, kl_ref, kc_ref, ta_ref, ti_ref, sb_ref, sbi_ref,
                       fc_ref, gc_ref, skip_ref, nw_ref, gm_ref, y_ref, zbuf, b_ref, *, n_ctx):
    order = pl.program_id(2)
    lt = zbuf.shape[0]
    n_lat = lt - n_ctx
    n1 = b_ref.shape[1] // FFT_N2

    @pl.when(order == 0)
    def _():
        zbuf[...] = _short_conv(z_ref, wz_ref, bz_ref, n_ctx)

    _fft_stage_a(lambda j, cnt: zbuf[pl.ds(n_ctx + j, cnt, stride=FFT_N2), :], ta_ref, b_ref, n1 // 2, n1)

    def freq(k1, carry):
        rows, x = _fft_stage_b_rows(b_ref, sb_ref, k1)
        xr, xi = x[:FFT_N2], x[FFT_N2:]
        kr, ki = kl_ref[0, 0, rows, :], kl_ref[0, 1, rows, :]
        y = jnp.concatenate([xr * kr - xi * ki, xr * ki + xi * kr], axis=0).astype(BF16)
        c = _dot(sbi_ref[...], y)
        b_ref[0, rows, :] = c[:FFT_N2]
        b_ref[1, rows, :] = c[FFT_N2:]
        return carry
    lax.fori_loop(0, n1, freq, 0, unroll=2)

    def inv_a(j, carry):
        rhs = jnp.concatenate([b_ref[0, pl.ds(j, n1, stride=FFT_N2), :], b_ref[1, pl.ds(j, n1, stride=FFT_N2), :]], axis=0)
        b_ref[0, pl.ds(j, n1 // 2, stride=FFT_N2), :] = _dot(ti_ref[j], rhs.astype(BF16))
        return carry
    lax.fori_loop(0, FFT_N2, inv_a, 0, unroll=2)

    zc = zbuf[0:n_ctx, :]
    xc = _dot(fc_ref[...], zc.astype(BF16))
    nc = 2 * n_ctx
    xr, xi = xc[:nc], xc[nc:]
    kr, ki = kc_ref[0, 0], kc_ref[0, 1]
    yc = jnp.concatenate([xr * kr - xi * ki, xr * ki + xi * kr], axis=0).astype(BF16)
    conv_c = _dot(gc_ref[...], yc)

    gate = _short_conv(g_ref, wg_ref, bg_ref, n_ctx)
    skip = skip_ref[0]
    zbuf[0:n_ctx, :] = gate[0:n_ctx] * (conv_c + zc * skip)
    zbuf[n_ctx:lt, :] = gate[n_ctx:lt] * (b_ref[0, 0:n_lat, :] + zbuf[n_ctx:lt, :] * skip)

    @pl.when(order == pl.num_programs(2) - 1)
    def _():
        z = zbuf[...]
        ms = _dot_hi(z * z, gm_ref[...])
        y_ref[...] = (z * lax.rsqrt(ms + EPS) * nw_ref[...]).astype(y_ref.dtype)


def hyena_mixer(p16, conv_w, conv_b, kf_lat, kf_ctx, skip, norm_w, batch, n_ctx):
    import numpy as np
    lt = p16.shape[0] // batch
    n_lat = lt - n_ctx
    width = HYENA_WIDTH
    cb = width // HY_LANES
    n = 2 * n_lat
    n1 = n // FFT_N2
    ta, tinv, sb, sbi = _np_cplx_tables(n_lat)
    ta = ta[:, :, :n1 // 2]
    fc, gc = _np_dense_dft(n_ctx)
    fc = fc[:, :n_ctx]
    gsz = width // HYENA_GROUPS
    gidx = np.arange(HY_LANES) // gsz
    gmean = (gidx[:, None] == gidx[None, :]).astype(np.float32) / gsz
    z_blk = HYENA_ORDER * cb
    cw = conv_w.astype(F32)
    cbias = conv_b.astype(F32).reshape(1, -1)
    const2 = lambda b, c, o: (0, 0)
    const3 = lambda b, c, o: (0, 0, 0)
    return pl.pallas_call(
        functools.partial(_hyena_conv_kernel, n_ctx=n_ctx),
        grid=(batch, cb, HYENA_ORDER),
        in_specs=[pl.BlockSpec((lt, HY_LANES), lambda b, c, o: (b, z_blk + c)),
                  pl.BlockSpec((lt, HY_LANES), lambda b, c, o: (b, o * cb + c)),
                  pl.BlockSpec((HYENA_SHORT, HY_LANES), lambda b, c, o: (0, z_blk + c)),
                  pl.BlockSpec((1, HY_LANES), lambda b, c, o: (0, z_blk + c)),
                  pl.BlockSpec((HYENA_SHORT, HY_LANES), lambda b, c, o: (0, o * cb + c)),
                  pl.BlockSpec((1, HY_LANES), lambda b, c, o: (0, o * cb + c)),
                  pl.BlockSpec((1, 2, n, HY_LANES), lambda b, c, o: (o, 0, 0, c)),
                  pl.BlockSpec((1, 2, 2 * n_ctx, HY_LANES), lambda b, c, o: (o, 0, 0, c)),
                  pl.BlockSpec((FFT_N2, 2 * n1, n1 // 2), const3),
                  pl.BlockSpec((FFT_N2, n1 // 2, 2 * n1), const3),
                  pl.BlockSpec((2 * FFT_N2, 2 * FFT_N2), const2),
                  pl.BlockSpec((2 * FFT_N2, 2 * FFT_N2), const2),
                  pl.BlockSpec((4 * n_ctx, n_ctx), const2),
                  pl.BlockSpec((n_ctx, 4 * n_ctx), const2),
                  pl.BlockSpec((1, 1, HY_LANES), lambda b, c, o: (o, 0, c)),
                  pl.BlockSpec((1, HY_LANES), lambda b, c, o: (0, c)),
                  pl.BlockSpec((HY_LANES, HY_LANES), const2)],
        out_specs=pl.BlockSpec((lt, HY_LANES), lambda b, c, o: (b, c)),
        out_shape=jax.ShapeDtypeStruct((batch * lt, width), BF16),
        scratch_shapes=[pltpu.VMEM((lt, HY_LANES), F32), pltpu.VMEM((2, n, HY_LANES), F32)],
        compiler_params=pltpu.CompilerParams(dimension_semantics=("arbitrary", "arbitrary", "arbitrary"), vmem_limit_bytes=HY_VMEM),
        name="hyena_mixer",
    )(p16, p16, cw, cbias, cw, cbias, kf_lat, kf_ctx,
      jnp.asarray(ta, BF16), jnp.asarray(tinv, BF16), jnp.asarray(sb, BF16), jnp.asarray(sbi, BF16),
      jnp.asarray(fc, BF16), jnp.asarray(gc, BF16), skip.astype(F32).reshape(HYENA_ORDER, 1, width),
      norm_w.astype(F32).reshape(1, width), jnp.asarray(gmean, F32))


def hyena_spectrum_dense(kern):
    n_ord, n, width = kern.shape
    fwd, _ = _np_dense_dft(n // 2)
    spec = pallas_matmul(jnp.asarray(fwd, F32), kern.transpose(1, 0, 2).reshape(n, n_ord * width), tm=2 * n, tn=width)
    return spec.reshape(2, n, n_ord, width).transpose(2, 0, 1, 3)


def hyena_kernels(length, fw1, fb1, fw2, fb2, fw3, fb3, freq, wout):
    h = hyena_filters(length, fw1, fb1, fw2, fb2, fw3, fb3, freq, wout)
    zero = jnp.zeros((HYENA_ORDER, 1, HYENA_WIDTH), F32)
    return jnp.concatenate([h[:, 0], zero, jnp.flip(h[:, 1, 1:], axis=1)], axis=1)


def rope_tables(n_ctx, n_lat):
    half = RET_DQK // 2
    inv = ROPE_BASE ** (-jnp.arange(0, half, 2, dtype=F32) / half)
    t = jnp.arange(n_lat)
    row = (t // GRID_W).astype(F32)
    col = (t % GRID_W).astype(F32)
    a_row = row[:, None] * inv[None, :]
    a_col = col[:, None] * inv[None, :]
    cos = jnp.concatenate([jnp.cos(a_row), jnp.cos(a_row), jnp.cos(a_col), jnp.cos(a_col)], -1)
    sin = jnp.concatenate([-jnp.sin(a_row), jnp.sin(a_row), -jnp.sin(a_col), jnp.sin(a_col)], -1)
    cos = jnp.concatenate([jnp.ones((n_ctx, HEAD_W), F32), cos], 0)
    sin = jnp.concatenate([jnp.zeros((n_ctx, HEAD_W), F32), sin], 0)
    return cos, sin


def kernel(x, c, ctx, c_ctx, ada_w, ada_b, ln_w, ln_b, even_w_in, mlstm_gate_b, mlstm_norm_w, hgrn_lb, hgrn_norm_w, even_w_out, odd_w_in, hy_conv_w, hy_conv_b, hy_f_w1, hy_f_b1, hy_f_w2, hy_f_b2, hy_f_w3, hy_f_b3, hy_f_freq, hy_f_wout, hy_skip, hy_norm_w, ret_log_decay, ret_norm_w, odd_w_out, router_w, router_bias, exp_w1, exp_w3, exp_w2, sh_w1, sh_w3, sh_w2):
    batch, n_lat, d = x.shape
    n_ctx = ctx.shape[1]
    lt = n_ctx + n_lat
    t = batch * lt
    sm = jax.nn.softmax(hgrn_lb.astype(F32), axis=0)
    lower_bounds = jnp.cumsum(sm, axis=0) - sm[0]
    acts = jnp.concatenate([jax.nn.silu(c.astype(F32)), jax.nn.silu(c_ctx.astype(F32))[None]], 0)
    acts = jnp.pad(acts, ((0, 16 - (batch + 1) % 16), (0, 0)))
    cos_t, sin_t = rope_tables(n_ctx, n_lat)
    is_ctx = (jnp.arange(lt) < n_ctx)[None, :, None]
    h = jnp.concatenate([ctx, x], axis=1).reshape(t, d).astype(F32)
    for l in range(DEPTH):
        mods = pallas_matmul(acts, ada_w[l].astype(F32), tm=acts.shape[0], tn=512)[:batch + 1] + ada_b[l]
        mod_lat = mods[:batch].reshape(batch, 6, d)
        mod_ctx = jnp.broadcast_to(mods[batch].reshape(1, 6, d), (batch, 6, d))
        modtab = jnp.stack([mod_ctx, mod_lat], axis=1)
        if l % 2 == 0:
            e = l // 2
            w = even_w_in[e].astype(F32)
            c0 = sum(MLSTM_SPLITS[:4])
            c1 = c0 + MLSTM_SPLITS[4]
            c2 = c1 + HGRN_SPLITS[0] + HGRN_SPLITS[1]
            c3 = c2 + HGRN_SPLITS[2] + HGRN_SPLITS[3]
            w16 = jnp.concatenate([w[:, :c0], w[:, c1:c2], w[:, c3:]], 1).astype(BF16)
            w32 = jnp.concatenate([w[:, c0:c1], jnp.zeros((d, HEAD_W - MLSTM_SPLITS[4]), F32), w[:, c2:c3]], 1).astype(BF16)
            p16 = proj_in(h, modtab, w16, BF16, batch, n_ctx)
            p32 = proj_in(h, modtab, w32, F32, batch, n_ctx, tn=3 * HEAD_W)
            ya = mlstm_mixer(p16, p32, mlstm_gate_b[e], mlstm_norm_w[e], batch, n_ctx)
            yb = hgrn_mixer(p16, p32, lower_bounds[e], hgrn_norm_w[e], batch, n_ctx)
            w_out = even_w_out[e]
        else:
            o = l // 2
            hw = ODD_SPLITS[0]
            p16 = proj_in(h, modtab, odd_w_in[o].astype(BF16), BF16, batch, n_ctx)
            filt_p = [p.astype(F32) for p in (hy_f_w1[o], hy_f_b1[o], hy_f_w2[o], hy_f_b2[o], hy_f_w3[o], hy_f_b3[o], hy_f_freq[o], hy_f_wout[o])]
            kf_lat = hyena_spectrum(hyena_kernels(n_lat, *filt_p))
            kf_ctx = hyena_spectrum_dense(hyena_kernels(n_ctx, *filt_p))
            ya = hyena_mixer(p16, hy_conv_w[o], hy_conv_b[o], kf_lat, kf_ctx, hy_skip[o], hy_norm_w[o], batch, n_ctx)
            yb = retention_mixer(p16, hw // HEAD_W, ret_log_decay[o], ret_norm_w[o], cos_t, sin_t, batch, n_ctx)
            w_out = odd_w_out[o]
        ka = ya.shape[1]
        h1, v = proj_out_ln(ya, yb, w_out[:ka].astype(BF16), w_out[ka:].astype(BF16), h, modtab, ln_w[l, 0], ln_b[l, 0], batch, n_ctx)
        f = moe_ffn(v, router_w[l], router_bias[l], exp_w1[l], exp_w3[l], exp_w2[l], sh_w1[l], sh_w3[l], sh_w2[l])
        gate2 = jnp.where(is_ctx, mod_ctx[:, None, 5], mod_lat[:, None, 5])
        h = layer_norm(DEEPNORM_ALPHA * h1.reshape(batch, lt, d) + gate2 * f.reshape(batch, lt, d), ln_w[l, 1], ln_b[l, 1]).reshape(t, d)
    return h.reshape(batch, lt, d)[:, n_ctx:]
```

```python
import functools
import math
import jax
import jax.numpy as jnp
from jax import lax
from jax.experimental import pallas as pl
from jax.experimental.pallas import tpu as pltpu

D_MODEL = 1024
BATCH = 8
SEQ = 4096
DEPTH = 4

GRID_W = 64
CTX_LEN = 256
N_EVEN = (DEPTH + 1) // 2
N_ODD = DEPTH // 2

MLSTM_HEADS = 4
MLSTM_DQK = 64
MLSTM_DV = 128
MLSTM_CHUNK = 64
HGRN_HEADS = 4
HGRN_DK = 128
HGRN_DV = 128
HGRN_CHUNK = 32
HYENA_WIDTH = 512
HYENA_GROUPS = 8
HYENA_ORDER = 2
HYENA_EMB = 33
HYENA_FILTER_HIDDEN = 64
HYENA_SHORT = 3
HYENA_TARGET = 1e-2
HYENA_FAST_PCT = 0.3
HYENA_SLOW_PCT = 1.5
RET_HEADS = 4
RET_DQK = 128
RET_DV = 128
RET_CHUNK = 64
ROPE_BASE = 10000.0
N_EXPERTS = 64
TOP_K = 8
EXPERT_HIDDEN = 256
SHARED_HIDDEN = 256
ROUTED_SCALE = 2.5
MOE_BLOCK = 128
DEEPNORM_ALPHA = (2 * DEPTH) ** 0.25
DEEPNORM_BETA = (8 * DEPTH) ** -0.25
EPS = 1e-5
NEG_BIG = -1e30

MLSTM_SPLITS = (MLSTM_HEADS * MLSTM_DQK, MLSTM_HEADS * MLSTM_DQK, MLSTM_HEADS * MLSTM_DV, MLSTM_HEADS * MLSTM_DV, 4 * MLSTM_HEADS)
HGRN_SPLITS = (HGRN_HEADS * HGRN_DK, HGRN_HEADS * HGRN_DV, HGRN_HEADS * HGRN_DK, HGRN_HEADS * HGRN_DK, HGRN_HEADS * HGRN_DV)
EVEN_SPLITS = MLSTM_SPLITS + HGRN_SPLITS
EVEN_PROJ = sum(EVEN_SPLITS)
EVEN_MIX = MLSTM_HEADS * MLSTM_DV + HGRN_HEADS * HGRN_DV
RET_SPLITS = (RET_HEADS * RET_DQK, RET_HEADS * RET_DQK, RET_HEADS * RET_DV, RET_HEADS * RET_DV)
ODD_SPLITS = ((HYENA_ORDER + 1) * HYENA_WIDTH,) + RET_SPLITS
ODD_PROJ = sum(ODD_SPLITS)
ODD_MIX = HYENA_WIDTH + RET_HEADS * RET_DV

F32 = jnp.float32
BF16 = jnp.bfloat16


def _mm_kernel(x_ref, w_ref, o_ref):
    o_ref[...] = jnp.dot(x_ref[...].astype(BF16), w_ref[...].astype(BF16), preferred_element_type=F32)


def pallas_matmul(x, w, tm=512, tn=512):
    t, k = x.shape
    n = w.shape[1]
    n_pad = -(-n // tn) * tn
    if n_pad != n:
        w = jnp.pad(w, ((0, 0), (0, n_pad - n)))
    out = pl.pallas_call(
        _mm_kernel,
        grid=(t // tm, n_pad // tn),
        in_specs=[pl.BlockSpec((tm, k), lambda i, j: (i, 0)), pl.BlockSpec((k, tn), lambda i, j: (0, j))],
        out_specs=pl.BlockSpec((tm, tn), lambda i, j: (i, j)),
        out_shape=jax.ShapeDtypeStruct((t, n_pad), F32),
    )(x, w)
    return out[:, :n]


def mm3(x, w):
    b, l, d = x.shape
    return pallas_matmul(x.reshape(b * l, d), w).reshape(b, l, -1)


def split_cols(a, sizes):
    out, start = [], 0
    for s in sizes:
        out.append(a[..., start:start + s])
        start += s
    return out


def to_heads(a, n_heads):
    b, l, w = a.shape
    return a.reshape(b, l, n_heads, w // n_heads).transpose(0, 2, 1, 3)


def chunk(a, cs):
    b, h, l = a.shape[:3]
    return a.reshape((b, h, l // cs, cs) + a.shape[3:])


def layer_norm(x, w, b):
    xf = x.astype(F32)
    mu = jnp.mean(xf, -1, keepdims=True)
    var = jnp.mean(jnp.square(xf - mu), -1, keepdims=True)
    return ((xf - mu) * lax.rsqrt(var + EPS) * w + b).astype(x.dtype)


def head_norm(h, w, rms):
    h = h.transpose(0, 2, 1, 3)
    if not rms:
        h = h - jnp.mean(h, -1, keepdims=True)
    h = h * lax.rsqrt(jnp.mean(jnp.square(h), -1, keepdims=True) + EPS)
    b, l, nh, dh = h.shape
    return h.reshape(b, l, nh * dh) * w


def group_rms(z, w):
    b, l, wd = z.shape
    zg = z.reshape(b, l, HYENA_GROUPS, wd // HYENA_GROUPS)
    zg = zg * lax.rsqrt(jnp.mean(jnp.square(zg), -1, keepdims=True) + EPS)
    return zg.reshape(b, l, wd) * w


def run_direction(scan_fn, ctx_seq, lat_seq, const, state0, reverse):
    if reverse:
        ctx_seq = tuple(jnp.flip(a, axis=2) for a in ctx_seq)
        lat_seq = tuple(jnp.flip(a, axis=2) for a in lat_seq)
    y_ctx, state = scan_fn(ctx_seq, const, state0)
    y_lat, _ = scan_fn(lat_seq, const, state)
    if reverse:
        y_ctx = jnp.flip(y_ctx, axis=2)
        y_lat = jnp.flip(y_lat, axis=2)
    return y_lat, y_ctx


def mlstm_scan(seq, const, state):
    q, k, v, ig, lf = seq
    cs = MLSTM_CHUNK
    qc, kc, vc, igc, lfc = (chunk(a, cs) for a in seq)
    b = jnp.cumsum(lfc, axis=-1)
    g = b[..., -1]
    a = g[..., None] - b + igc
    m_loc = jnp.max(a, axis=-1)
    w = jnp.exp(a - m_loc[..., None])
    c_loc = jnp.einsum('bhcs,bhcsv,bhcsk->bhcvk', w, vc, kc)
    n_loc = jnp.einsum('bhcs,bhcsk->bhck', w, kc)

    def step(carry, inp):
        c_st, n_st, m_st = carry
        g_j, ml_j, cl_j, nl_j = inp
        m_new = jnp.maximum(g_j + m_st, ml_j)
        sp = jnp.exp(g_j + m_st - m_new)
        sl = jnp.exp(ml_j - m_new)
        new = (sp[..., None, None] * c_st + sl[..., None, None] * cl_j, sp[..., None] * n_st + sl[..., None] * nl_j, m_new)
        return new, carry

    final, (c_in, n_in, m_in) = lax.scan(step, state, tuple(jnp.moveaxis(t, 2, 0) for t in (g, m_loc, c_loc, n_loc)))
    c_in, n_in, m_in = jnp.moveaxis(c_in, 0, 2), jnp.moveaxis(n_in, 0, 2), jnp.moveaxis(m_in, 0, 2)
    tri = jnp.tril(jnp.ones((cs, cs), bool))
    log_d = jnp.where(tri, b[..., :, None] - b[..., None, :] + igc[..., None, :], -jnp.inf)
    inter = b + m_in[..., None]
    m_out = jnp.maximum(inter, jnp.max(log_d, -1))
    s = jnp.einsum('bhctk,bhcsk->bhcts', qc, kc) * jnp.exp(log_d - m_out[..., None])
    e_int = jnp.exp(inter - m_out)
    num = jnp.einsum('bhcts,bhcsv->bhctv', s, vc) + e_int[..., None] * jnp.einsum('bhctk,bhcvk->bhctv', qc, c_in)
    den = jnp.sum(s, -1) + e_int * jnp.einsum('bhctk,bhck->bhct', qc, n_in)
    h = num / jnp.maximum(jnp.abs(den), jnp.exp(-m_out))[..., None]
    return h.reshape(v.shape), final


def hgrn2_scan(seq, const, state):
    q, k, v, lf = seq
    cs = HGRN_CHUNK
    qc, kc, vc, lfc = (chunk(a, cs) for a in seq)
    a_cum = jnp.cumsum(lfc, axis=3)
    a_end = a_cum[:, :, :, -1]
    s_loc = jnp.einsum('bhcsk,bhcsv->bhckv', kc * jnp.exp(a_end[:, :, :, None] - a_cum), vc)

    def step(s_st, inp):
        a_j, sl_j = inp
        return jnp.exp(a_j)[..., None] * s_st + sl_j, s_st

    final, s_in = lax.scan(step, state, (jnp.moveaxis(a_end, 2, 0), jnp.moveaxis(s_loc, 2, 0)))
    s_in = jnp.moveaxis(s_in, 0, 2)
    inter = jnp.einsum('bhctk,bhckv->bhctv', qc * jnp.exp(a_cum), s_in)
    a_mid = a_cum[:, :, :, cs // 2 - 1:cs // 2]
    tri = jnp.tril(jnp.ones((cs, cs), bool))
    sc = jnp.einsum('bhctk,bhcsk->bhcts', qc * jnp.exp(a_cum - a_mid), kc * jnp.exp(a_mid - a_cum))
    sc = jnp.where(tri, sc, 0.0)
    out = inter + jnp.einsum('bhcts,bhcsv->bhctv', sc, vc)
    return out.reshape(v.shape), final


def retention_scan(seq, log_gamma, state):
    q, k, v = seq
    cs = RET_CHUNK
    qc, kc, vc = (chunk(a, cs) for a in seq)
    pos = jnp.arange(cs, dtype=F32)
    tri = jnp.tril(jnp.ones((cs, cs), bool))
    delta = pos[:, None] - pos[None, :]
    decay = jnp.exp(jnp.where(tri, delta[None] * log_gamma[:, None, None], -jnp.inf))
    sc = jnp.einsum('bhctk,bhcsk->bhcts', qc, kc) * decay[:, None]
    intra = jnp.einsum('bhcts,bhcsv->bhctv', sc, vc)
    q_dec = jnp.exp((pos + 1.0)[None, :] * log_gamma[:, None])
    k_dec = jnp.exp((cs - 1.0 - pos)[None, :] * log_gamma[:, None])
    s_loc = jnp.einsum('bhcsk,hs,bhcsv->bhckv', kc, k_dec, vc)
    chunk_decay = jnp.exp(cs * log_gamma)[None, :, None, None]

    def step(s_st, sl_j):
        return chunk_decay * s_st + sl_j, s_st

    final, s_in = lax.scan(step, state, jnp.moveaxis(s_loc, 2, 0))
    s_in = jnp.moveaxis(s_in, 0, 2)
    inter = jnp.einsum('bhctk,ht,bhckv->bhctv', qc, q_dec, s_in)
    return (inter + intra).reshape(v.shape), final


def rope_1d(x, pos):
    d = x.shape[-1]
    inv = ROPE_BASE ** (-jnp.arange(0, d, 2, dtype=F32) / d)
    ang = pos[:, None] * inv[None, :]
    cos = jnp.cos(ang)[None, :, None, :]
    sin = jnp.sin(ang)[None, :, None, :]
    x1, x2 = x[..., :d // 2], x[..., d // 2:]
    return jnp.concatenate([x1 * cos - x2 * sin, x1 * sin + x2 * cos], -1)


def axial_rope(x, row, col):
    half = x.shape[-1] // 2
    return jnp.concatenate([rope_1d(x[..., :half], row), rope_1d(x[..., half:], col)], -1)


def mlstm_group(cols_lat, cols_ctx, gate_b, norm_w):
    def prep(cols):
        q, k, v, o, gates = cols
        b, l, _ = q.shape
        q = to_heads(q, MLSTM_HEADS) * MLSTM_DQK ** -0.5
        k = to_heads(k, MLSTM_HEADS)
        v = to_heads(v, MLSTM_HEADS)
        gates = (gates.reshape(b, l, 4, MLSTM_HEADS) + gate_b).transpose(2, 0, 3, 1)
        fwd = (q, k, v, gates[0], jax.nn.log_sigmoid(gates[1]))
        bwd = (q, k, v, gates[2], jax.nn.log_sigmoid(gates[3]))
        return fwd, bwd, o

    fwd_l, bwd_l, o_l = prep(cols_lat)
    fwd_c, bwd_c, o_c = prep(cols_ctx)
    batch = o_l.shape[0]
    state0 = (jnp.zeros((batch, MLSTM_HEADS, MLSTM_DV, MLSTM_DQK), F32),
              jnp.zeros((batch, MLSTM_HEADS, MLSTM_DQK), F32),
              jnp.full((batch, MLSTM_HEADS), NEG_BIG, F32))
    hf_lat, hf_ctx = run_direction(mlstm_scan, fwd_c, fwd_l, None, state0, False)
    hb_lat, hb_ctx = run_direction(mlstm_scan, bwd_c, bwd_l, None, state0, True)
    y_lat = head_norm(hf_lat + hb_lat, norm_w, False) * jax.nn.sigmoid(o_l)
    y_ctx = head_norm(hf_ctx + hb_ctx, norm_w, False) * jax.nn.sigmoid(o_c)
    return y_lat, y_ctx


def hgrn_group(cols_lat, cols_ctx, lb, norm_w):
    def prep(cols):
        q, i, f_fwd, f_bwd, g = cols
        q = to_heads(jax.nn.silu(q), HGRN_HEADS)
        v = to_heads(i, HGRN_HEADS)

        def gate(fp):
            f = lb + (1.0 - lb) * jax.nn.sigmoid(fp)
            return to_heads(1.0 - f, HGRN_HEADS), to_heads(jnp.log(f), HGRN_HEADS)

        k_f, lf_f = gate(f_fwd)
        k_b, lf_b = gate(f_bwd)
        return (q, k_f, v, lf_f), (q, k_b, v, lf_b), g

    fwd_l, bwd_l, g_l = prep(cols_lat)
    fwd_c, bwd_c, g_c = prep(cols_ctx)
    state0 = jnp.zeros((g_l.shape[0], HGRN_HEADS, HGRN_DK, HGRN_DV), F32)
    yf_lat, yf_ctx = run_direction(hgrn2_scan, fwd_c, fwd_l, None, state0, False)
    yb_lat, yb_ctx = run_direction(hgrn2_scan, bwd_c, bwd_l, None, state0, True)
    y_lat = head_norm(yf_lat + yb_lat, norm_w, True) * jax.nn.silu(g_l)
    y_ctx = head_norm(yf_ctx + yb_ctx, norm_w, True) * jax.nn.silu(g_c)
    return y_lat, y_ctx


def hyena_filters(length, fw1, fb1, fw2, fb2, fw3, fb3, freq, wout):
    pos = jnp.arange(length, dtype=F32)
    t = (pos / max(length - 1, 1))[:, None]
    bands = (HYENA_EMB - 1) // 2
    w = 2.0 * math.pi * pos[:, None] / length * jnp.linspace(1e-4, bands - 1, bands)[None, :]
    z = jnp.concatenate([t, jnp.cos(w), jnp.sin(w)], -1)
    hid = jnp.sin(freq * (z @ fw1 + fb1))
    hid = jnp.sin(freq * (hid @ fw2 + fb2))
    hid = jnp.sin(freq * (hid @ fw3 + fb3))
    h = (hid @ wout).reshape(length, HYENA_ORDER, 2, HYENA_WIDTH)
    deltas = jnp.abs(jnp.linspace(math.log(HYENA_TARGET) / HYENA_SLOW_PCT, math.log(HYENA_TARGET) / HYENA_FAST_PCT, HYENA_WIDTH))
    decay = jnp.exp(-t * deltas[None, :])
    return (h * decay[:, None, None, :]).transpose(1, 2, 0, 3)


def bidir_fftconv(u, h_fwd, h_bwd, d_skip):
    length, width = h_fwd.shape
    kern = jnp.concatenate([h_fwd, jnp.zeros((1, width), F32), jnp.flip(h_bwd[1:], axis=0)], axis=0)
    spec = jnp.fft.rfft(u, n=2 * length, axis=1) * jnp.fft.rfft(kern, axis=0)[None]
    y = jnp.fft.irfft(spec, n=2 * length, axis=1)[:, :length]
    return y + u * d_skip


def hyena_group(u_lat, u_ctx, conv_w, conv_b, fw1, fb1, fw2, fb2, fw3, fb3, freq, wout, skip, norm_w):
    def run(u):
        length = u.shape[1]
        up = jnp.pad(u, ((0, 0), (1, 1), (0, 0)))
        u = up[:, :-2] * conv_w[0] + up[:, 1:-1] * conv_w[1] + up[:, 2:] * conv_w[2] + conv_b
        streams = jnp.split(u, HYENA_ORDER + 1, axis=-1)
        gates, z = streams[:-1], streams[-1]
        h = hyena_filters(length, fw1, fb1, fw2, fb2, fw3, fb3, freq, wout)
        for n in range(HYENA_ORDER):
            z = gates[n] * bidir_fftconv(z, h[n, 0], h[n, 1], skip[n])
        return group_rms(z, norm_w)

    return run(u_lat), run(u_ctx)


def retention_group(cols_lat, cols_ctx, log_decay, norm_w, row, col):
    def prep(cols, rotary):
        q, k, v, g = cols
        b, l, _ = q.shape
        q = q.reshape(b, l, RET_HEADS, RET_DQK)
        k = k.reshape(b, l, RET_HEADS, RET_DQK)
        if rotary:
            q = axial_rope(q, row, col)
            k = axial_rope(k, row, col)
        q = q.transpose(0, 2, 1, 3)
        k = k.transpose(0, 2, 1, 3) * RET_DQK ** -0.5
        return (q, k, to_heads(v, RET_HEADS)), g

    lat, g_l = prep(cols_lat, True)
    ctx, g_c = prep(cols_ctx, False)
    state0 = jnp.zeros((g_l.shape[0], RET_HEADS, RET_DQK, RET_DV), F32)
    yf_lat, yf_ctx = run_direction(retention_scan, ctx, lat, log_decay[0], state0, False)
    yb_lat, yb_ctx = run_direction(retention_scan, ctx, lat, log_decay[1], state0, True)
    y_lat = head_norm(yf_lat + yb_lat, norm_w, False) * jax.nn.silu(g_l)
    y_ctx = head_norm(yf_ctx + yb_ctx, norm_w, False) * jax.nn.silu(g_c)
    return y_lat, y_ctx


def even_mixer(u_lat, u_ctx, w_in, gate_b, mlstm_norm_w, lb, hgrn_norm_w, w_out):
    w_in = w_in.astype(F32)
    w_out = w_out.astype(F32)
    p_lat = split_cols(mm3(u_lat.astype(F32), w_in), EVEN_SPLITS)
    p_ctx = split_cols(mm3(u_ctx.astype(F32), w_in), EVEN_SPLITS)
    n_a = len(MLSTM_SPLITS)
    a_lat, a_ctx = mlstm_group(p_lat[:n_a], p_ctx[:n_a], gate_b.astype(F32), mlstm_norm_w.astype(F32))
    b_lat, b_ctx = hgrn_group(p_lat[n_a:], p_ctx[n_a:], lb, hgrn_norm_w.astype(F32))
    return mm3(jnp.concatenate([a_lat, b_lat], -1), w_out), mm3(jnp.concatenate([a_ctx, b_ctx], -1), w_out)


def odd_mixer(u_lat, u_ctx, w_in, hyena_p, log_decay, ret_norm_w, w_out, row, col):
    w_in = w_in.astype(F32)
    w_out = w_out.astype(F32)
    p_lat = split_cols(mm3(u_lat.astype(F32), w_in), ODD_SPLITS)
    p_ctx = split_cols(mm3(u_ctx.astype(F32), w_in), ODD_SPLITS)
    hy_lat, hy_ctx = hyena_group(p_lat[0], p_ctx[0], *[p.astype(F32) for p in hyena_p])
    r_lat, r_ctx = retention_group(p_lat[1:], p_ctx[1:], log_decay.astype(F32), ret_norm_w.astype(F32), row, col)
    return mm3(jnp.concatenate([hy_lat, r_lat], -1), w_out), mm3(jnp.concatenate([hy_ctx, r_ctx], -1), w_out)


ROUTE_TB = 512
DISPATCH_TB = 512
COMBINE_TB = 256
FFN_BM = 512


def _dot(a, b):
    return jnp.dot(a, b, preferred_element_type=F32)


def _route_kernel(v_ref, rw_ref, rb_ref, e_ref, rk_ref, g_ref, cnt_ref, carry_ref):
    tb, n_e = v_ref.shape[0], rw_ref.shape[1]

    @pl.when(pl.program_id(0) == 0)
    def _():
        carry_ref[...] = jnp.zeros_like(carry_ref)

    x = v_ref[...]
    w = rw_ref[...]
    xh = x.astype(BF16)
    xl = (x - xh.astype(F32)).astype(BF16)
    wh = w.astype(BF16)
    wl = (w - wh.astype(F32)).astype(BF16)
    logits = _dot(xh, wh) + (_dot(xh, wl) + _dot(xl, wh))
    scores = jax.nn.sigmoid(logits)
    work = scores + rb_ref[...]
    lane = lax.broadcasted_iota(jnp.int32, (tb, n_e), 1)
    col = lax.broadcasted_iota(jnp.int32, (tb, TOP_K), 1)
    e_out = jnp.zeros((tb, TOP_K), jnp.int32)
    g_out = jnp.zeros((tb, TOP_K), F32)
    mask = jnp.zeros((tb, n_e), F32)
    onehots = []
    for j in range(TOP_K):
        m = jnp.max(work, axis=1, keepdims=True)
        idx = jnp.min(jnp.where(work == m, lane, n_e), axis=1, keepdims=True)
        oh = lane == idx
        gj = jnp.sum(jnp.where(oh, scores, 0.0), axis=1, keepdims=True)
        e_out = jnp.where(col == j, idx, e_out)
        g_out = jnp.where(col == j, gj, g_out)
        work = jnp.where(oh, -jnp.inf, work)
        mask = mask + oh.astype(F32)
        onehots.append(oh)
    r_i = lax.broadcasted_iota(jnp.int32, (tb, tb), 0)
    c_i = lax.broadcasted_iota(jnp.int32, (tb, tb), 1)
    tri = (r_i > c_i).astype(BF16)
    rank = _dot(tri, mask.astype(BF16)) + carry_ref[...]
    rk_out = jnp.zeros((tb, TOP_K), jnp.int32)
    for j in range(TOP_K):
        rkj = jnp.sum(jnp.where(onehots[j], rank, 0.0), axis=1, keepdims=True)
        rk_out = jnp.where(col == j, rkj.astype(jnp.int32), rk_out)
    carry_ref[...] = carry_ref[...] + jnp.sum(mask, axis=0, keepdims=True)
    e_ref[...] = e_out
    rk_ref[...] = rk_out
    g_ref[...] = ROUTED_SCALE * g_out / jnp.sum(g_out, axis=1, keepdims=True)
    cnt_ref[...] = carry_ref[...]


def moe_route(v, router_w, router_bias):
    n_tok, d = v.shape
    tb = ROUTE_TB
    return pl.pallas_call(
        _route_kernel,
        grid=(n_tok // tb,),
        in_specs=[pl.BlockSpec((tb, d), lambda i: (i, 0)),
                  pl.BlockSpec((d, N_EXPERTS), lambda i: (0, 0)),
                  pl.BlockSpec((1, N_EXPERTS), lambda i: (0, 0))],
        out_specs=[pl.BlockSpec((tb, TOP_K), lambda i: (i, 0)),
                   pl.BlockSpec((tb, TOP_K), lambda i: (i, 0)),
                   pl.BlockSpec((tb, TOP_K), lambda i: (i, 0)),
                   pl.BlockSpec((1, N_EXPERTS), lambda i: (0, 0))],
        out_shape=[jax.ShapeDtypeStruct((n_tok, TOP_K), jnp.int32),
                   jax.ShapeDtypeStruct((n_tok, TOP_K), jnp.int32),
                   jax.ShapeDtypeStruct((n_tok, TOP_K), F32),
                   jax.ShapeDtypeStruct((1, N_EXPERTS), F32)],
        scratch_shapes=[pltpu.VMEM((1, N_EXPERTS), F32)],
        compiler_params=pltpu.CompilerParams(dimension_semantics=("arbitrary",)),
        name="moe_route",
    )(v, router_w, router_bias.reshape(1, N_EXPERTS))


U32 = jnp.uint32


def _pack_rows(x):
    w = x.shape[1] // 2
    lo = lax.bitcast_convert_type(x[:, :w].astype(BF16).astype(F32), U32)
    hi = lax.bitcast_convert_type(x[:, w:].astype(BF16).astype(F32), U32)
    return hi | (lo >> 16)


def _unpack_rows(u):
    lo = lax.bitcast_convert_type(u << 16, F32)
    hi = lax.bitcast_convert_type(u & jnp.uint32(0xFFFF0000), F32)
    return lo, hi


def _dispatch_kernel(zs_ref, zf_ref, nu_ref, pos_hbm, v_ref, xs_hbm, pos_smem, zbuf, pk_ref, sem_idx, sem_sc, sem_z):
    i = pl.program_id(0)
    tb = v_ref.shape[0]
    bm = zbuf.shape[0]
    n_blocks = xs_hbm.shape[0] // bm
    idx_cp = pltpu.make_async_copy(pos_hbm.at[i], pos_smem, sem_idx)
    idx_cp.start()
    pk_ref[...] = _pack_rows(v_ref[...])

    @pl.when(i == 0)
    def _():
        zbuf[...] = jnp.zeros_like(zbuf)
        for e in range(N_EXPERTS):
            @pl.when(zf_ref[e] != 0)
            def _():
                start = pl.multiple_of(zs_ref[e], bm)
                pltpu.make_async_copy(zbuf, xs_hbm.at[pl.ds(start, bm)], sem_z).start()
        for e in range(N_EXPERTS):
            @pl.when(zf_ref[e] != 0)
            def _():
                pltpu.make_async_copy(zbuf, xs_hbm.at[pl.ds(0, bm)], sem_z).wait()

        def tail(b, carry):
            cp = pltpu.make_async_copy(zbuf, xs_hbm.at[pl.ds(pl.multiple_of(b * bm, bm), bm)], sem_z)
            cp.start()
            cp.wait()
            return carry

        lax.fori_loop(nu_ref[0], n_blocks, tail, 0)

    idx_cp.wait()

    def body(t, carry):
        for k in range(TOP_K):
            p = pos_smem[0, t * TOP_K + k]
            pltpu.make_async_copy(pk_ref.at[pl.ds(t, 1)], xs_hbm.at[pl.ds(p, 1)], sem_sc).start()
        return carry

    lax.fori_loop(0, tb, body, 0)
    for k in range(TOP_K):
        pltpu.make_async_copy(pk_ref, pk_ref, sem_sc).wait()


def moe_dispatch(v, pos3, zero_start, zero_flag, n_used, n_rows):
    n_tok, d = v.shape
    tb = DISPATCH_TB
    dp = d // 2
    grid_spec = pltpu.PrefetchScalarGridSpec(
        num_scalar_prefetch=3,
        grid=(n_tok // tb,),
        in_specs=[pl.BlockSpec(memory_space=pl.ANY),
                  pl.BlockSpec((tb, d), lambda i, zs, zf, nu: (i, 0))],
        out_specs=pl.BlockSpec(memory_space=pl.ANY),
        scratch_shapes=[pltpu.SMEM((1, tb * TOP_K), jnp.int32),
                        pltpu.VMEM((FFN_BM, dp), U32),
                        pltpu.VMEM((tb, dp), U32),
                        pltpu.SemaphoreType.DMA(()),
                        pltpu.SemaphoreType.DMA(()),
                        pltpu.SemaphoreType.DMA(())],
    )
    return pl.pallas_call(
        _dispatch_kernel,
        grid_spec=grid_spec,
        out_shape=jax.ShapeDtypeStruct((n_rows, dp), U32),
        compiler_params=pltpu.CompilerParams(dimension_semantics=("arbitrary",)),
        name="moe_dispatch",
    )(zero_start, zero_flag, n_used, pos3, v)


def _ffn_kernel(be_ref, nu_ref, xs_ref, w1_ref, w3_ref, w2_ref, y_ref):
    used = pl.program_id(0) < nu_ref[0]

    @pl.when(used)
    def _():
        dp = xs_ref.shape[1]
        lo, hi = _unpack_rows(xs_ref[...])
        lo, hi = lo.astype(BF16), hi.astype(BF16)
        h1 = _dot(lo, w1_ref[0, :dp, :]) + _dot(hi, w1_ref[0, dp:, :])
        h3 = _dot(lo, w3_ref[0, :dp, :]) + _dot(hi, w3_ref[0, dp:, :])
        h = (h1 * jax.nn.sigmoid(h1) * h3).astype(BF16)
        y_ref[...] = _pack_rows(_dot(h, w2_ref[0]))

    @pl.when(jnp.logical_not(used))
    def _():
        y_ref[...] = jnp.zeros_like(y_ref)


def moe_expert_ffn(xs, block_e, n_used, w1, w3, w2):
    n_rows, dp = xs.shape
    d = 2 * dp
    bm = FFN_BM
    hid = w1.shape[2]

    def row_map(i, be, nu):
        return (jnp.minimum(i, nu[0] - 1), 0)

    def w_map(i, be, nu):
        return (be[jnp.minimum(i, nu[0] - 1)], 0, 0)

    grid_spec = pltpu.PrefetchScalarGridSpec(
        num_scalar_prefetch=2,
        grid=(n_rows // bm,),
        in_specs=[pl.BlockSpec((bm, dp), row_map),
                  pl.BlockSpec((1, d, hid), w_map),
                  pl.BlockSpec((1, d, hid), w_map),
                  pl.BlockSpec((1, hid, d), w_map)],
        out_specs=pl.BlockSpec((bm, dp), lambda i, be, nu: (i, 0)),
    )
    return pl.pallas_call(
        _ffn_kernel,
        grid_spec=grid_spec,
        out_shape=jax.ShapeDtypeStruct((n_rows, dp), U32),
        compiler_params=pltpu.CompilerParams(dimension_semantics=("arbitrary",)),
        name="moe_expert_ffn",
    )(block_e, n_used, xs, w1, w3, w2)


def _combine_kernel(pos_hbm, y_hbm, v_ref, g_ref, sw1_ref, sw3_ref, sw2_ref, h1_ref, mod_ref, lnw_ref, lnb_ref, o_ref,
                    pos_smem, ybuf, sem_idx, sem_y, *, n_ctx, blocks_per_batch):
    i = pl.program_id(0)
    n = pl.num_programs(0)
    tb = v_ref.shape[0]
    dp = ybuf.shape[1]
    slot = lax.rem(i, 2)

    def fetch_idx(step, s):
        return pltpu.make_async_copy(pos_hbm.at[step], pos_smem.at[s], sem_idx.at[s])

    def issue_gather(s):
        base = s * (TOP_K * tb)

        def body(t, carry):
            for k in range(TOP_K):
                p = pos_smem[s, 0, t * TOP_K + k]
                pltpu.make_async_copy(y_hbm.at[pl.ds(p, 1)], ybuf.at[pl.ds(base + k * tb + t, 1)], sem_y.at[s]).start()
            return carry
        lax.fori_loop(0, tb, body, 0)

    @pl.when(i == 0)
    def _():
        cp = fetch_idx(0, 0)
        cp.start()
        cp.wait()
        issue_gather(0)

    for s in range(2):
        @pl.when((i + 1 < n) & (slot == 1 - s))
        def _():
            cp = fetch_idx(i + 1, s)
            cp.start()
            cp.wait()
            issue_gather(s)

    x = v_ref[...].astype(BF16)
    h1 = _dot(x, sw1_ref[...])
    h3 = _dot(x, sw3_ref[...])
    shared = _dot((h1 * jax.nn.sigmoid(h1) * h3).astype(BF16), sw2_ref[...])
    base = pl.multiple_of(slot * (TOP_K * tb), TOP_K * tb)
    for k in range(TOP_K):
        rows = ybuf.at[pl.ds(base + k * tb, tb)]
        pltpu.make_async_copy(rows, rows, sem_y.at[slot]).wait()
    g = g_ref[...]
    acc_lo = shared[:, :dp]
    acc_hi = shared[:, dp:]
    for k in range(TOP_K):
        lo, hi = _unpack_rows(ybuf[pl.ds(base + k * tb, tb), :])
        acc_lo = acc_lo + g[:, k:k + 1] * lo
        acc_hi = acc_hi + g[:, k:k + 1] * hi
    f = jnp.concatenate([acc_lo, acc_hi], axis=1)
    is_ctx = _is_ctx_rows(tb, blocks_per_batch, n_ctx)
    z = DEEPNORM_ALPHA * h1_ref[...] + _row_mod(mod_ref, 5, is_ctx) * f
    o_ref[...] = _layer_norm_rows(z, lnw_ref[...], lnb_ref[...])


def moe_combine(pos3, y, v, gate, sw1, sw3, sw2, h1, modtab, ln_w, ln_b, batch, n_ctx):
    n_tok, d = v.shape
    tb = COMBINE_TB
    hid = sw1.shape[1]
    dp = y.shape[1]
    bpb = n_tok // batch // tb
    row = lambda i: (i, 0)
    fixed = lambda i: (0, 0)
    return pl.pallas_call(
        functools.partial(_combine_kernel, n_ctx=n_ctx, blocks_per_batch=bpb),
        grid=(n_tok // tb,),
        in_specs=[pl.BlockSpec(memory_space=pl.ANY),
                  pl.BlockSpec(memory_space=pl.ANY),
                  pl.BlockSpec((tb, d), row),
                  pl.BlockSpec((tb, TOP_K), row),
                  pl.BlockSpec((d, hid), fixed),
                  pl.BlockSpec((d, hid), fixed),
                  pl.BlockSpec((hid, d), fixed),
                  pl.BlockSpec((tb, d), row),
                  pl.BlockSpec((1, 2, 6, d), lambda i: (i // bpb, 0, 0, 0)),
                  pl.BlockSpec((1, d), fixed), pl.BlockSpec((1, d), fixed)],
        out_specs=pl.BlockSpec((tb, d), row),
        out_shape=jax.ShapeDtypeStruct((n_tok, d), F32),
        scratch_shapes=[pltpu.SMEM((2, 1, tb * TOP_K), jnp.int32),
                        pltpu.VMEM((2 * TOP_K * tb, dp), U32),
                        pltpu.SemaphoreType.DMA((2,)),
                        pltpu.SemaphoreType.DMA((2,))],
        compiler_params=pltpu.CompilerParams(dimension_semantics=("arbitrary",), vmem_limit_bytes=48 * 1024 * 1024),
        name="moe_combine",
    )(pos3, y, v, gate, sw1, sw3, sw2, h1, modtab, ln_w.reshape(1, d).astype(F32), ln_b.reshape(1, d).astype(F32))


def moe_ffn(t, router_w, router_bias, w1, w3, w2, sw1, sw3, sw2, h1, modtab, ln_w, ln_b, batch, n_ctx):
    n_tok, d = t.shape
    tf = t.astype(F32)
    bm = FFN_BM
    e_sel, rank, gate, cnt = moe_route(tf, router_w.astype(F32), router_bias.astype(F32))
    counts = cnt[0].astype(jnp.int32)
    padded = (counts + bm - 1) // bm * bm
    pend = jnp.cumsum(padded)
    pstart = pend - padded
    n_blocks = -(-(n_tok * TOP_K + N_EXPERTS * (bm - 1)) // bm)
    expert_ids = jnp.arange(N_EXPERTS, dtype=jnp.int32)
    pos = rank + jnp.sum(jnp.where(e_sel[..., None] == expert_ids, pstart, 0), -1)
    block_start = jnp.arange(n_blocks, dtype=jnp.int32) * bm
    block_e = jnp.minimum(jnp.sum((pend[None, :] <= block_start[:, None]).astype(jnp.int32), axis=1), N_EXPERTS - 1)
    n_used = (pend[-1:] // bm).astype(jnp.int32)
    zero_start = jnp.maximum(pend - bm, 0).astype(jnp.int32)
    zero_flag = (counts > 0).astype(jnp.int32)
    xs = moe_dispatch(tf, pos.reshape(n_tok // DISPATCH_TB, 1, DISPATCH_TB * TOP_K), zero_start, zero_flag, n_used, n_blocks * bm)
    y = moe_expert_ffn(xs, block_e, n_used, w1.astype(BF16), w3.astype(BF16), w2.astype(BF16))
    return moe_combine(pos.reshape(n_tok // COMBINE_TB, 1, COMBINE_TB * TOP_K), y, tf, gate,
                       sw1.astype(BF16), sw3.astype(BF16), sw2.astype(BF16), h1, modtab, ln_w, ln_b, batch, n_ctx)


HEAD_W = 128
MIX_VMEM = 48 * 1024 * 1024
NORM_ROWS = 256


def _dot_nt(a, b):
    return lax.dot_general(a, b, (((1,), (1,)), ((), ())), preferred_element_type=F32)


def _dot_tn(a, b):
    return lax.dot_general(a, b, (((0,), (0,)), ((), ())), preferred_element_type=F32)


def _dot_hi(a, b):
    return jnp.dot(a, b, preferred_element_type=F32, precision=lax.Precision.HIGHEST)


def _tri(n, upper):
    r = lax.broadcasted_iota(jnp.int32, (n, n), 0)
    c = lax.broadcasted_iota(jnp.int32, (n, n), 1)
    return (r <= c) if upper else (r >= c)


def _bidir_loop(n_ctx_chunks, n_chunks, step, unroll):
    def body(c, carry):
        cb = jnp.where(c < n_ctx_chunks, n_ctx_chunks - 1 - c, n_chunks - 1 + n_ctx_chunks - c)
        step(c, False)
        step(cb, True)
        return carry
    lax.fori_loop(0, n_chunks, body, 0, unroll=unroll)


def _norm_gate_epilogue(hf_ref, hb_ref, gate_ref, nw_ref, y_ref, n_rows, rms, gate_fn):
    def body(i, carry):
        rows = pl.ds(pl.multiple_of(i * NORM_ROWS, NORM_ROWS), NORM_ROWS)
        h = hf_ref[rows, :] + hb_ref[rows, :]
        if not rms:
            h = h - jnp.mean(h, axis=1, keepdims=True)
        h = h * lax.rsqrt(jnp.mean(h * h, axis=1, keepdims=True) + EPS)
        y_ref[rows, :] = (h * nw_ref[...] * gate_fn(gate_ref[rows, :].astype(F32))).astype(y_ref.dtype)
        return carry
    lax.fori_loop(0, n_rows // NORM_ROWS, body, 0)


def _silu(x):
    return x * jax.nn.sigmoid(x)


def _retention_kernel(ld_ref, q_ref, k_ref, v_ref, g_ref, cos_ref, sin_ref, perm_ref, nw_ref, y_ref,
                      qr_ref, kr_ref, hf_ref, hb_ref, sf_ref, sb_ref, *, n_ctx):
    cs = RET_CHUNK
    lt = q_ref.shape[0]
    hd = pl.program_id(1)

    def rope(i, carry):
        rows = pl.ds(pl.multiple_of(i * NORM_ROWS, NORM_ROWS), NORM_ROWS)
        cs_, sn_ = cos_ref[rows, :], sin_ref[rows, :]
        q = q_ref[rows, :]
        k = k_ref[rows, :]
        qr_ref[rows, :] = (q.astype(F32) * cs_ + _dot(q, perm_ref[...]) * sn_).astype(BF16)
        kr_ref[rows, :] = ((k.astype(F32) * cs_ + _dot(k, perm_ref[...]) * sn_) * RET_DQK ** -0.5).astype(BF16)
        return carry
    lax.fori_loop(0, lt // NORM_ROWS, rope, 0)

    pos_r = lax.broadcasted_iota(jnp.int32, (cs, cs), 0).astype(F32)
    pos_c = lax.broadcasted_iota(jnp.int32, (cs, cs), 1).astype(F32)
    pos_t = lax.broadcasted_iota(jnp.int32, (cs, HEAD_W), 0).astype(F32)
    consts = []
    for d in range(2):
        lg = ld_ref[d, hd]
        if d == 0:
            decay = jnp.where(pos_r >= pos_c, jnp.exp((pos_r - pos_c) * lg), 0.0)
            q_dec = jnp.exp((pos_t + 1.0) * lg)
            k_dec = jnp.exp((cs - 1.0 - pos_t) * lg)
        else:
            decay = jnp.where(pos_r <= pos_c, jnp.exp((pos_c - pos_r) * lg), 0.0)
            q_dec = jnp.exp((cs - pos_t) * lg)
            k_dec = jnp.exp(pos_t * lg)
        consts.append((decay, q_dec, k_dec, jnp.exp(cs * lg)))
    sf_ref[...] = jnp.zeros_like(sf_ref)
    sb_ref[...] = jnp.zeros_like(sb_ref)

    def step(chunk, reverse):
        decay, q_dec, k_dec, c_dec = consts[1 if reverse else 0]
        s_ref = sb_ref if reverse else sf_ref
        h_ref = hb_ref if reverse else hf_ref
        rows = pl.ds(pl.multiple_of(chunk * cs, cs), cs)
        q = qr_ref[rows, :]
        k = kr_ref[rows, :]
        v = v_ref[rows, :]
        sc = _dot_nt(q, k) * decay
        s_in = s_ref[...]
        inter = _dot((q.astype(F32) * q_dec).astype(BF16), s_in.astype(BF16))
        h_ref[rows, :] = inter + _dot(sc.astype(BF16), v)
        s_ref[...] = c_dec * s_in + _dot_tn((k.astype(F32) * k_dec).astype(BF16), v)

    _bidir_loop(n_ctx // cs, lt // cs, step, 8)
    _norm_gate_epilogue(hf_ref, hb_ref, g_ref, nw_ref, y_ref, lt, False, _silu)


def retention_mixer(p16, col0, log_decay, norm_w, cos_t, sin_t, batch, n_ctx):
    lt = p16.shape[0] // batch
    nh = RET_HEADS
    half = RET_DQK // 2
    src = jnp.arange(HEAD_W)
    swapped = jnp.where((src % half) < half // 2, src + half // 2, src - half // 2)
    perm = (src[:, None] == swapped[None, :]).astype(BF16)

    def blk(off):
        return pl.BlockSpec((lt, HEAD_W), lambda b, h, ld: (b, off + h))

    grid_spec = pltpu.PrefetchScalarGridSpec(
        num_scalar_prefetch=1,
        grid=(batch, nh),
        in_specs=[blk(col0), blk(col0 + nh), blk(col0 + 2 * nh), blk(col0 + 3 * nh),
                  pl.BlockSpec((lt, HEAD_W), lambda b, h, ld: (0, 0)),
                  pl.BlockSpec((lt, HEAD_W), lambda b, h, ld: (0, 0)),
                  pl.BlockSpec((HEAD_W, HEAD_W), lambda b, h, ld: (0, 0)),
                  pl.BlockSpec((1, HEAD_W), lambda b, h, ld: (0, h))],
        out_specs=pl.BlockSpec((lt, HEAD_W), lambda b, h, ld: (b, h)),
        scratch_shapes=[pltpu.VMEM((lt, HEAD_W), BF16), pltpu.VMEM((lt, HEAD_W), BF16),
                        pltpu.VMEM((lt, HEAD_W), F32), pltpu.VMEM((lt, HEAD_W), F32),
                        pltpu.VMEM((RET_DQK, RET_DV), F32), pltpu.VMEM((RET_DQK, RET_DV), F32)],
    )
    return pl.pallas_call(
        functools.partial(_retention_kernel, n_ctx=n_ctx),
        grid_spec=grid_spec,
        out_shape=jax.ShapeDtypeStruct((batch * lt, nh * HEAD_W), BF16),
        compiler_params=pltpu.CompilerParams(dimension_semantics=("arbitrary", "arbitrary"), vmem_limit_bytes=MIX_VMEM),
        name="retention_mixer",
    )(log_decay.astype(F32), p16, p16, p16, p16, cos_t, sin_t, perm, norm_w.reshape(1, -1).astype(F32))


def _hgrn_kernel(q_ref, i_ref, g_ref, ff_ref, fb_ref, lb_ref, nw_ref, y_ref, hf_ref, hb_ref, sf_ref, sb_ref, *, n_ctx):
    cs = HGRN_CHUNK
    lt = q_ref.shape[0]
    lb = lb_ref[...]
    sf_ref[...] = jnp.zeros_like(sf_ref)
    sb_ref[...] = jnp.zeros_like(sb_ref)
    tri = (_tri(cs, False).astype(F32), _tri(cs, True).astype(F32))
    mask = (_tri(cs, False), _tri(cs, True))

    def step(chunk, reverse):
        d = 1 if reverse else 0
        f_ref = fb_ref if reverse else ff_ref
        s_ref = sb_ref if reverse else sf_ref
        h_ref = hb_ref if reverse else hf_ref
        rows = pl.ds(pl.multiple_of(chunk * cs, cs), cs)
        f = lb + (1.0 - lb) * jax.nn.sigmoid(f_ref[rows, :])
        kk = 1.0 - f
        a_cum = _dot_hi(tri[d], jnp.log(f))
        a_end = a_cum[0:1, :] if reverse else a_cum[cs - 1:cs, :]
        mid = cs // 2 if reverse else cs // 2 - 1
        a_mid = a_cum[mid:mid + 1, :]
        q = _silu(q_ref[rows, :].astype(F32))
        v = i_ref[rows, :]
        s_in = s_ref[...]
        inter = _dot_nt((q * jnp.exp(a_cum)).astype(BF16), s_in.astype(BF16))
        sc = _dot_nt((q * jnp.exp(a_cum - a_mid)).astype(BF16), (kk * jnp.exp(a_mid - a_cum)).astype(BF16))
        sc = jnp.where(mask[d], sc, 0.0)
        h_ref[rows, :] = inter + _dot(sc.astype(BF16), v)
        s_ref[...] = s_in * jnp.exp(a_end) + _dot_tn(v, (kk * jnp.exp(a_end - a_cum)).astype(BF16))

    _bidir_loop(n_ctx // cs, lt // cs, step, 4)
    _norm_gate_epilogue(hf_ref, hb_ref, g_ref, nw_ref, y_ref, lt, True, _silu)


EVEN16_MLSTM_Q, EVEN16_MLSTM_K, EVEN16_MLSTM_V, EVEN16_MLSTM_O = 0, 2, 4, 8
EVEN16_HGRN_Q, EVEN16_HGRN_I, EVEN16_HGRN_G = 12, 16, 20
EVEN32_GATES, EVEN32_FF, EVEN32_FB = 0, 1, 5


def hgrn_mixer(p16, p32, lb, norm_w, batch, n_ctx):
    lt = p16.shape[0] // batch
    nh = HGRN_HEADS

    def blk(off):
        return pl.BlockSpec((lt, HEAD_W), lambda b, h: (b, off + h))

    def vec():
        return pl.BlockSpec((1, HEAD_W), lambda b, h: (0, h))

    return pl.pallas_call(
        functools.partial(_hgrn_kernel, n_ctx=n_ctx),
        grid=(batch, nh),
        in_specs=[blk(EVEN16_HGRN_Q), blk(EVEN16_HGRN_I), blk(EVEN16_HGRN_G), blk(EVEN32_FF), blk(EVEN32_FB), vec(), vec()],
        out_specs=pl.BlockSpec((lt, HEAD_W), lambda b, h: (b, h)),
        out_shape=jax.ShapeDtypeStruct((batch * lt, nh * HEAD_W), BF16),
        scratch_shapes=[pltpu.VMEM((lt, HEAD_W), F32), pltpu.VMEM((lt, HEAD_W), F32),
                        pltpu.VMEM((HGRN_DV, HGRN_DK), F32), pltpu.VMEM((HGRN_DV, HGRN_DK), F32)],
        compiler_params=pltpu.CompilerParams(dimension_semantics=("arbitrary", "arbitrary"), vmem_limit_bytes=MIX_VMEM),
        name="hgrn_mixer",
    )(p16, p16, p16, p32, p32, lb.reshape(1, -1).astype(F32), norm_w.reshape(1, -1).astype(F32))


def _mlstm_kernel(q_ref, k_ref, v_ref, o_ref, gt_ref, gb_ref, nw_ref, y_ref, hf_ref, hb_ref, cf_ref, cb_ref, mf_ref, mb_ref, *, n_ctx):
    cs = MLSTM_CHUNK
    lt = q_ref.shape[0]
    hd = pl.program_id(1)
    lane = lax.broadcasted_iota(jnp.int32, (1, HEAD_W), 1)
    head_mask = ((lane // MLSTM_DQK) == (hd % 2)).astype(F32)
    gcol = lax.broadcasted_iota(jnp.int32, (1, HEAD_W), 1)
    tri = (_tri(cs, False).astype(F32), _tri(cs, True).astype(F32))
    mask = (_tri(cs, False), _tri(cs, True))
    ones_v = jnp.ones((cs, HEAD_W), BF16)
    for ref in (cf_ref, cb_ref):
        ref[...] = jnp.zeros_like(ref)
    for ref in (mf_ref, mb_ref):
        ref[...] = jnp.full_like(ref, NEG_BIG)

    def pick(tile, col):
        return jnp.sum(jnp.where(gcol == col, tile, 0.0), axis=1, keepdims=True)

    def step(chunk, reverse):
        d = 1 if reverse else 0
        c_ref = cb_ref if reverse else cf_ref
        m_ref = mb_ref if reverse else mf_ref
        h_ref = hb_ref if reverse else hf_ref
        rows = pl.ds(pl.multiple_of(chunk * cs, cs), cs)
        x = gt_ref[rows, :] + gb_ref[...]
        ci = hd + (2 * MLSTM_HEADS if reverse else 0)
        ig = pick(x, ci)
        lf = pick(jax.nn.log_sigmoid(x), ci + MLSTM_HEADS)
        lf_t = jnp.broadcast_to(lf, (cs, HEAD_W))
        b = _dot_hi(tri[d], lf_t)
        g = b[0:1, :] if reverse else b[cs - 1:cs, :]
        ig_t = jnp.broadcast_to(ig, (cs, HEAD_W))
        a = g - b + ig_t
        m_loc = jnp.max(a, axis=0, keepdims=True)
        w = jnp.exp(a - m_loc)
        m_in = m_ref[...]
        r_row = jnp.transpose(ig_t - b)[0:1, 0:cs]
        log_d = jnp.where(mask[d], b[:, 0:cs] + r_row, -jnp.inf)
        inter = b + m_in
        m_out = jnp.maximum(inter, jnp.max(log_d, axis=1, keepdims=True))
        q = q_ref[rows, :]
        k = k_ref[rows, :]
        v = v_ref[rows, :]
        qm = (q.astype(F32) * (head_mask * MLSTM_DQK ** -0.5)).astype(BF16)
        s = _dot_nt(qm, k) * jnp.exp(log_d - m_out[:, 0:cs])
        e_int = jnp.exp(inter - m_out)
        vo = jnp.concatenate([v, ones_v], axis=1)
        c_in = c_ref[...]
        r = _dot(s.astype(BF16), vo) + jnp.concatenate([e_int, e_int], axis=1) * _dot(qm, c_in.astype(BF16))
        num, den = r[:, :HEAD_W], r[:, HEAD_W:]
        h_ref[rows, :] = num / jnp.maximum(jnp.abs(den), jnp.exp(-m_out))
        m_new = jnp.maximum(g + m_in, m_loc)
        sp = jnp.exp(g + m_in - m_new)
        sl = jnp.exp(m_loc - m_new)
        kw = (k.astype(F32) * head_mask * w).astype(BF16)
        c_loc = _dot_tn(kw, vo)
        c_ref[...] = jnp.concatenate([sp, sp], axis=1) * c_in + jnp.concatenate([sl, sl], axis=1) * c_loc
        m_ref[...] = m_new

    _bidir_loop(n_ctx // cs, lt // cs, step, 4)
    _norm_gate_epilogue(hf_ref, hb_ref, o_ref, nw_ref, y_ref, lt, False, jax.nn.sigmoid)


def mlstm_mixer(p16, p32, gate_b, norm_w, batch, n_ctx):
    lt = p16.shape[0] // batch
    nh = MLSTM_HEADS
    gb = jnp.zeros((1, HEAD_W), F32).at[0, :4 * nh].set(gate_b.astype(F32).reshape(-1))

    return pl.pallas_call(
        functools.partial(_mlstm_kernel, n_ctx=n_ctx),
        grid=(batch, nh),
        in_specs=[pl.BlockSpec((lt, HEAD_W), lambda b, h: (b, EVEN16_MLSTM_Q + h // 2)),
                  pl.BlockSpec((lt, HEAD_W), lambda b, h: (b, EVEN16_MLSTM_K + h // 2)),
                  pl.BlockSpec((lt, HEAD_W), lambda b, h: (b, EVEN16_MLSTM_V + h)),
                  pl.BlockSpec((lt, HEAD_W), lambda b, h: (b, EVEN16_MLSTM_O + h)),
                  pl.BlockSpec((lt, HEAD_W), lambda b, h: (b, EVEN32_GATES)),
                  pl.BlockSpec((1, HEAD_W), lambda b, h: (0, 0)),
                  pl.BlockSpec((1, HEAD_W), lambda b, h: (0, h))],
        out_specs=pl.BlockSpec((lt, HEAD_W), lambda b, h: (b, h)),
        out_shape=jax.ShapeDtypeStruct((batch * lt, nh * HEAD_W), BF16),
        scratch_shapes=[pltpu.VMEM((lt, HEAD_W), F32), pltpu.VMEM((lt, HEAD_W), F32),
                        pltpu.VMEM((HEAD_W, 2 * HEAD_W), F32), pltpu.VMEM((HEAD_W, 2 * HEAD_W), F32),
                        pltpu.VMEM((1, HEAD_W), F32), pltpu.VMEM((1, HEAD_W), F32)],
        compiler_params=pltpu.CompilerParams(dimension_semantics=("arbitrary", "arbitrary"), vmem_limit_bytes=MIX_VMEM),
        name="mlstm_mixer",
    )(p16, p16, p16, p16, p32, gb, norm_w.reshape(1, -1).astype(F32))


ROW_BLOCKS_PER_BATCH_IN = 4
ROW_BLOCKS_PER_BATCH_OUT = 8
PROJ_TN = 512


def _row_mod(mod_ref, idx, is_ctx):
    return jnp.where(is_ctx, mod_ref[0, 0, idx:idx + 1, :], mod_ref[0, 1, idx:idx + 1, :])


def _is_ctx_rows(tm, blocks_per_batch, n_ctx):
    row = (pl.program_id(0) % blocks_per_batch) * tm + lax.broadcasted_iota(jnp.int32, (tm, 1), 0)
    return row < n_ctx


def _proj_in_kernel(h_ref, mod_ref, w_ref, o_ref, u_ref, *, n_ctx, blocks_per_batch):
    tm = h_ref.shape[0]

    @pl.when(pl.program_id(1) == 0)
    def _():
        is_ctx = _is_ctx_rows(tm, blocks_per_batch, n_ctx)
        u = h_ref[...] * (1.0 + _row_mod(mod_ref, 1, is_ctx)) + _row_mod(mod_ref, 0, is_ctx)
        u_ref[...] = u.astype(BF16)

    o_ref[...] = _dot(u_ref[...], w_ref[...]).astype(o_ref.dtype)


def proj_in(h, modtab, w, out_dtype, batch, n_ctx, tn=PROJ_TN):
    t, d = h.shape
    n = w.shape[1]
    bpb = ROW_BLOCKS_PER_BATCH_IN
    tm = t // batch // bpb
    return pl.pallas_call(
        functools.partial(_proj_in_kernel, n_ctx=n_ctx, blocks_per_batch=bpb),
        grid=(t // tm, n // tn),
        in_specs=[pl.BlockSpec((tm, d), lambda i, j: (i, 0)),
                  pl.BlockSpec((1, 2, 6, d), lambda i, j: (i // bpb, 0, 0, 0)),
                  pl.BlockSpec((d, tn), lambda i, j: (0, j))],
        out_specs=pl.BlockSpec((tm, tn), lambda i, j: (i, j)),
        out_shape=jax.ShapeDtypeStruct((t, n), out_dtype),
        scratch_shapes=[pltpu.VMEM((tm, d), BF16)],
        compiler_params=pltpu.CompilerParams(dimension_semantics=("arbitrary", "arbitrary"), vmem_limit_bytes=MIX_VMEM),
        name="proj_in",
    )(h, modtab, w)


def _layer_norm_rows(z, w, b):
    mu = jnp.mean(z, axis=1, keepdims=True)
    zc = z - mu
    var = jnp.mean(zc * zc, axis=1, keepdims=True)
    return zc * lax.rsqrt(var + EPS) * w + b


def _proj_out_kernel(ya_ref, yb_ref, wa_ref, wb_ref, h_ref, mod_ref, lnw_ref, lnb_ref, h1_ref, v_ref, *, n_ctx, blocks_per_batch):
    tm = h_ref.shape[0]
    is_ctx = _is_ctx_rows(tm, blocks_per_batch, n_ctx)
    y = _dot(ya_ref[...], wa_ref[...]) + _dot(yb_ref[...], wb_ref[...])
    z = DEEPNORM_ALPHA * h_ref[...] + _row_mod(mod_ref, 2, is_ctx) * y
    h1 = _layer_norm_rows(z, lnw_ref[...], lnb_ref[...])
    h1_ref[...] = h1
    v_ref[...] = h1 * (1.0 + _row_mod(mod_ref, 4, is_ctx)) + _row_mod(mod_ref, 3, is_ctx)


def proj_out_ln(ya, yb, wa, wb, h, modtab, ln_w, ln_b, batch, n_ctx):
    t, d = h.shape
    bpb = ROW_BLOCKS_PER_BATCH_OUT
    tm = t // batch // bpb
    ka, kb = ya.shape[1], yb.shape[1]
    row = lambda i: (i, 0)
    fixed = lambda i: (0, 0)
    return pl.pallas_call(
        functools.partial(_proj_out_kernel, n_ctx=n_ctx, blocks_per_batch=bpb),
        grid=(t // tm,),
        in_specs=[pl.BlockSpec((tm, ka), row), pl.BlockSpec((tm, kb), row),
                  pl.BlockSpec((ka, d), fixed), pl.BlockSpec((kb, d), fixed),
                  pl.BlockSpec((tm, d), row),
                  pl.BlockSpec((1, 2, 6, d), lambda i: (i // bpb, 0, 0, 0)),
                  pl.BlockSpec((1, d), fixed), pl.BlockSpec((1, d), fixed)],
        out_specs=[pl.BlockSpec((tm, d), row), pl.BlockSpec((tm, d), row)],
        out_shape=[jax.ShapeDtypeStruct((t, d), F32), jax.ShapeDtypeStruct((t, d), F32)],
        compiler_params=pltpu.CompilerParams(dimension_semantics=("arbitrary",), vmem_limit_bytes=MIX_VMEM),
        name="proj_out_ln",
    )(ya, yb, wa, wb, h, modtab, ln_w.reshape(1, d).astype(F32), ln_b.reshape(1, d).astype(F32))


FFT_N2 = 128
HY_LANES = 128
HY_VMEM = 56 * 1024 * 1024


def _np_cplx_tables(n_len):
    import numpy as np
    n = 2 * n_len
    n2 = FFT_N2
    n1 = n // n2
    w = lambda m, e: np.exp(-2j * np.pi * (e % m) / m)
    k1 = np.arange(n1)
    a = np.arange(n1)
    j2 = np.arange(n2)
    ta = w(n, j2[:, None, None] * k1[None, :, None]) * w(n1, k1[None, :, None] * a[None, None, :])
    ta_stack = np.concatenate([ta.real, ta.imag], axis=1)
    tinv = np.conj(ta).transpose(0, 2, 1) / n
    tinv_stack = np.concatenate([tinv.real, -tinv.imag], axis=2)[:, :n1 // 2]
    f2 = w(n2, j2[:, None] * j2[None, :])
    sb = np.block([[f2.real, -f2.imag], [f2.imag, f2.real]])
    sbi = np.block([[f2.real, f2.imag], [-f2.imag, f2.real]])
    return ta_stack, tinv_stack, sb, sbi


def _np_dense_dft(n_len):
    import numpy as np
    n = 2 * n_len
    k = np.arange(n)
    f = np.exp(-2j * np.pi * ((k[:, None] * k[None, :]) % n) / n)
    fwd = np.concatenate([f.real, f.imag], axis=0)
    g = np.conj(f)[:n_len] / n
    inv = np.concatenate([g.real, -g.imag], axis=1)
    return fwd, inv


def _fft_stage_a(load_rows, ta_ref, b_ref, n1_in, n1):
    def body(j, carry):
        x = load_rows(j, n1_in).astype(BF16)
        a = _dot(ta_ref[j], x)
        b_ref[0, pl.ds(j, n1, stride=FFT_N2), :] = a[:n1]
        b_ref[1, pl.ds(j, n1, stride=FFT_N2), :] = a[n1:]
        return carry
    lax.fori_loop(0, FFT_N2, body, 0, unroll=8)


def _fft_stage_b_rows(b_ref, sb_ref, k1):
    rows = pl.ds(pl.multiple_of(k1 * FFT_N2, FFT_N2), FFT_N2)
    rhs = jnp.concatenate([b_ref[0, rows, :], b_ref[1, rows, :]], axis=0).astype(BF16)
    return rows, _dot(sb_ref[...], rhs)


def _hyena_spectrum_kernel(k_ref, ta_ref, sb_ref, o_ref, b_ref):
    n1 = b_ref.shape[1] // FFT_N2
    _fft_stage_a(lambda j, cnt: k_ref[0, pl.ds(j, cnt, stride=FFT_N2), :], ta_ref, b_ref, n1, n1)

    def body(k1, carry):
        rows, x = _fft_stage_b_rows(b_ref, sb_ref, k1)
        o_ref[0, 0, rows, :] = x[:FFT_N2]
        o_ref[0, 1, rows, :] = x[FFT_N2:]
        return carry
    lax.fori_loop(0, n1, body, 0)


def hyena_spectrum(kern):
    n_ord, n, width = kern.shape
    n1 = n // FFT_N2
    ta, _, sb, _ = _np_cplx_tables(n // 2)
    return pl.pallas_call(
        _hyena_spectrum_kernel,
        grid=(n_ord, width // HY_LANES),
        in_specs=[pl.BlockSpec((1, n, HY_LANES), lambda o, c: (o, 0, c)),
                  pl.BlockSpec((FFT_N2, 2 * n1, n1), lambda o, c: (0, 0, 0)),
                  pl.BlockSpec((2 * FFT_N2, 2 * FFT_N2), lambda o, c: (0, 0))],
        out_specs=pl.BlockSpec((1, 2, n, HY_LANES), lambda o, c: (o, 0, 0, c)),
        out_shape=jax.ShapeDtypeStruct((n_ord, 2, n, width), F32),
        scratch_shapes=[pltpu.VMEM((2, n, HY_LANES), F32)],
        compiler_params=pltpu.CompilerParams(dimension_semantics=("arbitrary", "arbitrary"), vmem_limit_bytes=HY_VMEM),
        name="hyena_spectrum",
    )(kern, jnp.asarray(ta, BF16), jnp.asarray(sb, BF16))


def _short_conv(x_ref, w_ref, b_ref, n_ctx):
    x = x_ref[...].astype(F32)
    lt = x.shape[0]
    row = lax.broadcasted_iota(jnp.int32, (lt, 1), 0)
    prev = jnp.where((row == 0) | (row == n_ctx), 0.0, pltpu.roll(x, 1, axis=0))
    nxt = jnp.where((row == n_ctx - 1) | (row == lt - 1), 0.0, pltpu.roll(x, lt - 1, axis=0))
    return prev * w_ref[0:1, :] + x * w_ref[1:2, :] + nxt * w_ref[2:3, :] + b_ref[...]


def _hyena_conv_kernel(z_ref, g_ref, wz_ref, bz_ref, wg_ref, bg_ref, kl_ref, kc_ref, ta_ref, ti_ref, sb_ref, sbi_ref,
                       fc_ref, gc_ref, skip_ref, nw_ref, gm_ref, y_ref, zbuf, b_ref, *, n_ctx):
    order = pl.program_id(2)
    lt = zbuf.shape[0]
    n_lat = lt - n_ctx
    n1 = b_ref.shape[1] // FFT_N2

    @pl.when(order == 0)
    def _():
        zbuf[...] = _short_conv(z_ref, wz_ref, bz_ref, n_ctx)

    _fft_stage_a(lambda j, cnt: zbuf[pl.ds(n_ctx + j, cnt, stride=FFT_N2), :], ta_ref, b_ref, n1 // 2, n1)

    def freq(k1, carry):
        rows, x = _fft_stage_b_rows(b_ref, sb_ref, k1)
        xr, xi = x[:FFT_N2], x[FFT_N2:]
        kr, ki = kl_ref[0, 0, rows, :], kl_ref[0, 1, rows, :]
        y = jnp.concatenate([xr * kr - xi * ki, xr * ki + xi * kr], axis=0).astype(BF16)
        c = _dot(sbi_ref[...], y)
        b_ref[0, rows, :] = c[:FFT_N2]
        b_ref[1, rows, :] = c[FFT_N2:]
        return carry
    lax.fori_loop(0, n1, freq, 0, unroll=8)

    def inv_a(j, carry):
        rhs = jnp.concatenate([b_ref[0, pl.ds(j, n1, stride=FFT_N2), :], b_ref[1, pl.ds(j, n1, stride=FFT_N2), :]], axis=0)
        b_ref[0, pl.ds(j, n1 // 2, stride=FFT_N2), :] = _dot(ti_ref[j], rhs.astype(BF16))
        return carry
    lax.fori_loop(0, FFT_N2, inv_a, 0, unroll=8)

    zc = zbuf[0:n_ctx, :]
    xc = _dot(fc_ref[...], zc.astype(BF16))
    nc = 2 * n_ctx
    xr, xi = xc[:nc], xc[nc:]
    kr, ki = kc_ref[0, 0], kc_ref[0, 1]
    yc = jnp.concatenate([xr * kr - xi * ki, xr * ki + xi * kr], axis=0).astype(BF16)
    conv_c = _dot(gc_ref[...], yc)

    gate = _short_conv(g_ref, wg_ref, bg_ref, n_ctx)
    skip = skip_ref[0]
    zbuf[0:n_ctx, :] = gate[0:n_ctx] * (conv_c + zc * skip)
    zbuf[n_ctx:lt, :] = gate[n_ctx:lt] * (b_ref[0, 0:n_lat, :] + zbuf[n_ctx:lt, :] * skip)

    @pl.when(order == pl.num_programs(2) - 1)
    def _():
        z = zbuf[...]
        ms = _dot_hi(z * z, gm_ref[...])
        y_ref[...] = (z * lax.rsqrt(ms + EPS) * nw_ref[...]).astype(y_ref.dtype)


def hyena_mixer(p16, conv_w, conv_b, kf_lat, kf_ctx, skip, norm_w, batch, n_ctx):
    import numpy as np
    lt = p16.shape[0] // batch
    n_lat = lt - n_ctx
    width = HYENA_WIDTH
    cb = width // HY_LANES
    n = 2 * n_lat
    n1 = n // FFT_N2
    ta, tinv, sb, sbi = _np_cplx_tables(n_lat)
    ta = ta[:, :, :n1 // 2]
    fc, gc = _np_dense_dft(n_ctx)
    fc = fc[:, :n_ctx]
    gsz = width // HYENA_GROUPS
    gidx = np.arange(HY_LANES) // gsz
    gmean = (gidx[:, None] == gidx[None, :]).astype(np.float32) / gsz
    z_blk = HYENA_ORDER * cb
    cw = conv_w.astype(F32)
    cbias = conv_b.astype(F32).reshape(1, -1)
    const2 = lambda b, c, o: (0, 0)
    const3 = lambda b, c, o: (0, 0, 0)
    return pl.pallas_call(
        functools.partial(_hyena_conv_kernel, n_ctx=n_ctx),
        grid=(batch, cb, HYENA_ORDER),
        in_specs=[pl.BlockSpec((lt, HY_LANES), lambda b, c, o: (b, z_blk + c)),
                  pl.BlockSpec((lt, HY_LANES), lambda b, c, o: (b, o * cb + c)),
                  pl.BlockSpec((HYENA_SHORT, HY_LANES), lambda b, c, o: (0, z_blk + c)),
                  pl.BlockSpec((1, HY_LANES), lambda b, c, o: (0, z_blk + c)),
                  pl.BlockSpec((HYENA_SHORT, HY_LANES), lambda b, c, o: (0, o * cb + c)),
                  pl.BlockSpec((1, HY_LANES), lambda b, c, o: (0, o * cb + c)),
                  pl.BlockSpec((1, 2, n, HY_LANES), lambda b, c, o: (o, 0, 0, c)),
                  pl.BlockSpec((1, 2, 2 * n_ctx, HY_LANES), lambda b, c, o: (o, 0, 0, c)),
                  pl.BlockSpec((FFT_N2, 2 * n1, n1 // 2), const3),
                  pl.BlockSpec((FFT_N2, n1 // 2, 2 * n1), const3),
                  pl.BlockSpec((2 * FFT_N2, 2 * FFT_N2), const2),
                  pl.BlockSpec((2 * FFT_N2, 2 * FFT_N2), const2),
                  pl.BlockSpec((4 * n_ctx, n_ctx), const2),
                  pl.BlockSpec((n_ctx, 4 * n_ctx), const2),
                  pl.BlockSpec((1, 1, HY_LANES), lambda b, c, o: (o, 0, c)),
                  pl.BlockSpec((1, HY_LANES), lambda b, c, o: (0, c)),
                  pl.BlockSpec((HY_LANES, HY_LANES), const2)],
        out_specs=pl.BlockSpec((lt, HY_LANES), lambda b, c, o: (b, c)),
        out_shape=jax.ShapeDtypeStruct((batch * lt, width), BF16),
        scratch_shapes=[pltpu.VMEM((lt, HY_LANES), F32), pltpu.VMEM((2, n, HY_LANES), F32)],
        compiler_params=pltpu.CompilerParams(dimension_semantics=("arbitrary", "arbitrary", "arbitrary"), vmem_limit_bytes=HY_VMEM),
        name="hyena_mixer",
    )(p16, p16, cw, cbias, cw, cbias, kf_lat, kf_ctx,
      jnp.asarray(ta, BF16), jnp.asarray(tinv, BF16), jnp.asarray(sb, BF16), jnp.asarray(sbi, BF16),
      jnp.asarray(fc, BF16), jnp.asarray(gc, BF16), skip.astype(F32).reshape(HYENA_ORDER, 1, width),
      norm_w.astype(F32).reshape(1, width), jnp.asarray(gmean, F32))


def hyena_spectrum_dense(kern):
    n_ord, n, width = kern.shape
    fwd, _ = _np_dense_dft(n // 2)
    spec = pallas_matmul(jnp.asarray(fwd, F32), kern.transpose(1, 0, 2).reshape(n, n_ord * width), tm=2 * n, tn=width)
    return spec.reshape(2, n, n_ord, width).transpose(2, 0, 1, 3)


def hyena_kernels(length, fw1, fb1, fw2, fb2, fw3, fb3, freq, wout):
    h = hyena_filters(length, fw1, fb1, fw2, fb2, fw3, fb3, freq, wout)
    zero = jnp.zeros((HYENA_ORDER, 1, HYENA_WIDTH), F32)
    return jnp.concatenate([h[:, 0], zero, jnp.flip(h[:, 1, 1:], axis=1)], axis=1)


def rope_tables(n_ctx, n_lat):
    half = RET_DQK // 2
    inv = ROPE_BASE ** (-jnp.arange(0, half, 2, dtype=F32) / half)
    t = jnp.arange(n_lat)
    row = (t // GRID_W).astype(F32)
    col = (t % GRID_W).astype(F32)
    a_row = row[:, None] * inv[None, :]
    a_col = col[:, None] * inv[None, :]
    cos = jnp.concatenate([jnp.cos(a_row), jnp.cos(a_row), jnp.cos(a_col), jnp.cos(a_col)], -1)
    sin = jnp.concatenate([-jnp.sin(a_row), jnp.sin(a_row), -jnp.sin(a_col), jnp.sin(a_col)], -1)
    cos = jnp.concatenate([jnp.ones((n_ctx, HEAD_W), F32), cos], 0)
    sin = jnp.concatenate([jnp.zeros((n_ctx, HEAD_W), F32), sin], 0)
    return cos, sin


def kernel(x, c, ctx, c_ctx, ada_w, ada_b, ln_w, ln_b, even_w_in, mlstm_gate_b, mlstm_norm_w, hgrn_lb, hgrn_norm_w, even_w_out, odd_w_in, hy_conv_w, hy_conv_b, hy_f_w1, hy_f_b1, hy_f_w2, hy_f_b2, hy_f_w3, hy_f_b3, hy_f_freq, hy_f_wout, hy_skip, hy_norm_w, ret_log_decay, ret_norm_w, odd_w_out, router_w, router_bias, exp_w1, exp_w3, exp_w2, sh_w1, sh_w3, sh_w2):
    batch, n_lat, d = x.shape
    n_ctx = ctx.shape[1]
    lt = n_ctx + n_lat
    t = batch * lt
    sm = jax.nn.softmax(hgrn_lb.astype(F32), axis=0)
    lower_bounds = jnp.cumsum(sm, axis=0) - sm[0]
    acts = jnp.concatenate([jax.nn.silu(c.astype(F32)), jax.nn.silu(c_ctx.astype(F32))[None]], 0)
    acts = jnp.pad(acts, ((0, 16 - (batch + 1) % 16), (0, 0)))
    cos_t, sin_t = rope_tables(n_ctx, n_lat)
    is_ctx = (jnp.arange(lt) < n_ctx)[None, :, None]
    h = jnp.concatenate([ctx, x], axis=1).reshape(t, d).astype(F32)
    for l in range(DEPTH):
        mods = pallas_matmul(acts, ada_w[l].astype(F32), tm=acts.shape[0], tn=512)[:batch + 1] + ada_b[l]
        mod_lat = mods[:batch].reshape(batch, 6, d)
        mod_ctx = jnp.broadcast_to(mods[batch].reshape(1, 6, d), (batch, 6, d))
        modtab = jnp.stack([mod_ctx, mod_lat], axis=1)
        if l % 2 == 0:
            e = l // 2
            w = even_w_in[e].astype(F32)
            c0 = sum(MLSTM_SPLITS[:4])
            c1 = c0 + MLSTM_SPLITS[4]
            c2 = c1 + HGRN_SPLITS[0] + HGRN_SPLITS[1]
            c3 = c2 + HGRN_SPLITS[2] + HGRN_SPLITS[3]
            w16 = jnp.concatenate([w[:, :c0], w[:, c1:c2], w[:, c3:]], 1).astype(BF16)
            w32 = jnp.concatenate([w[:, c0:c1], jnp.zeros((d, HEAD_W - MLSTM_SPLITS[4]), F32), w[:, c2:c3]], 1).astype(BF16)
            p16 = proj_in(h, modtab, w16, BF16, batch, n_ctx)
            p32 = proj_in(h, modtab, w32, F32, batch, n_ctx, tn=3 * HEAD_W)
            ya = mlstm_mixer(p16, p32, mlstm_gate_b[e], mlstm_norm_w[e], batch, n_ctx)
            yb = hgrn_mixer(p16, p32, lower_bounds[e], hgrn_norm_w[e], batch, n_ctx)
            w_out = even_w_out[e]
        else:
            o = l // 2
            hw = ODD_SPLITS[0]
            p16 = proj_in(h, modtab, odd_w_in[o].astype(BF16), BF16, batch, n_ctx)
            filt_p = [p.astype(F32) for p in (hy_f_w1[o], hy_f_b1[o], hy_f_w2[o], hy_f_b2[o], hy_f_w3[o], hy_f_b3[o], hy_f_freq[o], hy_f_wout[o])]
            kf_lat = hyena_spectrum(hyena_kernels(n_lat, *filt_p))
            kf_ctx = hyena_spectrum_dense(hyena_kernels(n_ctx, *filt_p))
            ya = hyena_mixer(p16, hy_conv_w[o], hy_conv_b[o], kf_lat, kf_ctx, hy_skip[o], hy_norm_w[o], batch, n_ctx)
            yb = retention_mixer(p16, hw // HEAD_W, ret_log_decay[o], ret_norm_w[o], cos_t, sin_t, batch, n_ctx)
            w_out = odd_w_out[o]
        ka = ya.shape[1]
        h1, v = proj_out_ln(ya, yb, w_out[:ka].astype(BF16), w_out[ka:].astype(BF16), h, modtab, ln_w[l, 0], ln_b[l, 0], batch, n_ctx)
        h = moe_ffn(v, router_w[l], router_bias[l], exp_w1[l], exp_w3[l], exp_w2[l], sh_w1[l], sh_w3[l], sh_w2[l],
                    h1, modtab, ln_w[l, 1], ln_b[l, 1], batch, n_ctx)
    return h.reshape(batch, lt, d)[:, n_ctx:]
```

```python
import functools
import math
import jax
import jax.numpy as jnp
from jax import lax
from jax.experimental import pallas as pl
from jax.experimental.pallas import tpu as pltpu

D_MODEL = 1024
BATCH = 8
SEQ = 4096
DEPTH = 4

GRID_W = 64
CTX_LEN = 256
N_EVEN = (DEPTH + 1) // 2
N_ODD = DEPTH // 2

MLSTM_HEADS = 4
MLSTM_DQK = 64
MLSTM_DV = 128
MLSTM_CHUNK = 64
HGRN_HEADS = 4
HGRN_DK = 128
HGRN_DV = 128
HGRN_CHUNK = 32
HYENA_WIDTH = 512
HYENA_GROUPS = 8
HYENA_ORDER = 2
HYENA_EMB = 33
HYENA_FILTER_HIDDEN = 64
HYENA_SHORT = 3
HYENA_TARGET = 1e-2
HYENA_FAST_PCT = 0.3
HYENA_SLOW_PCT = 1.5
RET_HEADS = 4
RET_DQK = 128
RET_DV = 128
RET_CHUNK = 64
ROPE_BASE = 10000.0
N_EXPERTS = 64
TOP_K = 8
EXPERT_HIDDEN = 256
SHARED_HIDDEN = 256
ROUTED_SCALE = 2.5
MOE_BLOCK = 128
DEEPNORM_ALPHA = (2 * DEPTH) ** 0.25
DEEPNORM_BETA = (8 * DEPTH) ** -0.25
EPS = 1e-5
NEG_BIG = -1e30

MLSTM_SPLITS = (MLSTM_HEADS * MLSTM_DQK, MLSTM_HEADS * MLSTM_DQK, MLSTM_HEADS * MLSTM_DV, MLSTM_HEADS * MLSTM_DV, 4 * MLSTM_HEADS)
HGRN_SPLITS = (HGRN_HEADS * HGRN_DK, HGRN_HEADS * HGRN_DV, HGRN_HEADS * HGRN_DK, HGRN_HEADS * HGRN_DK, HGRN_HEADS * HGRN_DV)
EVEN_SPLITS = MLSTM_SPLITS + HGRN_SPLITS
EVEN_PROJ = sum(EVEN_SPLITS)
EVEN_MIX = MLSTM_HEADS * MLSTM_DV + HGRN_HEADS * HGRN_DV
RET_SPLITS = (RET_HEADS * RET_DQK, RET_HEADS * RET_DQK, RET_HEADS * RET_DV, RET_HEADS * RET_DV)
ODD_SPLITS = ((HYENA_ORDER + 1) * HYENA_WIDTH,) + RET_SPLITS
ODD_PROJ = sum(ODD_SPLITS)
ODD_MIX = HYENA_WIDTH + RET_HEADS * RET_DV

F32 = jnp.float32
BF16 = jnp.bfloat16


def _mm_kernel(x_ref, w_ref, o_ref):
    o_ref[...] = jnp.dot(x_ref[...].astype(BF16), w_ref[...].astype(BF16), preferred_element_type=F32)


def pallas_matmul(x, w, tm=512, tn=512):
    t, k = x.shape
    n = w.shape[1]
    n_pad = -(-n // tn) * tn
    if n_pad != n:
        w = jnp.pad(w, ((0, 0), (0, n_pad - n)))
    out = pl.pallas_call(
        _mm_kernel,
        grid=(t // tm, n_pad // tn),
        in_specs=[pl.BlockSpec((tm, k), lambda i, j: (i, 0)), pl.BlockSpec((k, tn), lambda i, j: (0, j))],
        out_specs=pl.BlockSpec((tm, tn), lambda i, j: (i, j)),
        out_shape=jax.ShapeDtypeStruct((t, n_pad), F32),
    )(x, w)
    return out[:, :n]


def mm3(x, w):
    b, l, d = x.shape
    return pallas_matmul(x.reshape(b * l, d), w).reshape(b, l, -1)


def split_cols(a, sizes):
    out, start = [], 0
    for s in sizes:
        out.append(a[..., start:start + s])
        start += s
    return out


def to_heads(a, n_heads):
    b, l, w = a.shape
    return a.reshape(b, l, n_heads, w // n_heads).transpose(0, 2, 1, 3)


def chunk(a, cs):
    b, h, l = a.shape[:3]
    return a.reshape((b, h, l // cs, cs) + a.shape[3:])


def layer_norm(x, w, b):
    xf = x.astype(F32)
    mu = jnp.mean(xf, -1, keepdims=True)
    var = jnp.mean(jnp.square(xf - mu), -1, keepdims=True)
    return ((xf - mu) * lax.rsqrt(var + EPS) * w + b).astype(x.dtype)


def head_norm(h, w, rms):
    h = h.transpose(0, 2, 1, 3)
    if not rms:
        h = h - jnp.mean(h, -1, keepdims=True)
    h = h * lax.rsqrt(jnp.mean(jnp.square(h), -1, keepdims=True) + EPS)
    b, l, nh, dh = h.shape
    return h.reshape(b, l, nh * dh) * w


def group_rms(z, w):
    b, l, wd = z.shape
    zg = z.reshape(b, l, HYENA_GROUPS, wd // HYENA_GROUPS)
    zg = zg * lax.rsqrt(jnp.mean(jnp.square(zg), -1, keepdims=True) + EPS)
    return zg.reshape(b, l, wd) * w


def run_direction(scan_fn, ctx_seq, lat_seq, const, state0, reverse):
    if reverse:
        ctx_seq = tuple(jnp.flip(a, axis=2) for a in ctx_seq)
        lat_seq = tuple(jnp.flip(a, axis=2) for a in lat_seq)
    y_ctx, state = scan_fn(ctx_seq, const, state0)
    y_lat, _ = scan_fn(lat_seq, const, state)
    if reverse:
        y_ctx = jnp.flip(y_ctx, axis=2)
        y_lat = jnp.flip(y_lat, axis=2)
    return y_lat, y_ctx


def mlstm_scan(seq, const, state):
    q, k, v, ig, lf = seq
    cs = MLSTM_CHUNK
    qc, kc, vc, igc, lfc = (chunk(a, cs) for a in seq)
    b = jnp.cumsum(lfc, axis=-1)
    g = b[..., -1]
    a = g[..., None] - b + igc
    m_loc = jnp.max(a, axis=-1)
    w = jnp.exp(a - m_loc[..., None])
    c_loc = jnp.einsum('bhcs,bhcsv,bhcsk->bhcvk', w, vc, kc)
    n_loc = jnp.einsum('bhcs,bhcsk->bhck', w, kc)

    def step(carry, inp):
        c_st, n_st, m_st = carry
        g_j, ml_j, cl_j, nl_j = inp
        m_new = jnp.maximum(g_j + m_st, ml_j)
        sp = jnp.exp(g_j + m_st - m_new)
        sl = jnp.exp(ml_j - m_new)
        new = (sp[..., None, None] * c_st + sl[..., None, None] * cl_j, sp[..., None] * n_st + sl[..., None] * nl_j, m_new)
        return new, carry

    final, (c_in, n_in, m_in) = lax.scan(step, state, tuple(jnp.moveaxis(t, 2, 0) for t in (g, m_loc, c_loc, n_loc)))
    c_in, n_in, m_in = jnp.moveaxis(c_in, 0, 2), jnp.moveaxis(n_in, 0, 2), jnp.moveaxis(m_in, 0, 2)
    tri = jnp.tril(jnp.ones((cs, cs), bool))
    log_d = jnp.where(tri, b[..., :, None] - b[..., None, :] + igc[..., None, :], -jnp.inf)
    inter = b + m_in[..., None]
    m_out = jnp.maximum(inter, jnp.max(log_d, -1))
    s = jnp.einsum('bhctk,bhcsk->bhcts', qc, kc) * jnp.exp(log_d - m_out[..., None])
    e_int = jnp.exp(inter - m_out)
    num = jnp.einsum('bhcts,bhcsv->bhctv', s, vc) + e_int[..., None] * jnp.einsum('bhctk,bhcvk->bhctv', qc, c_in)
    den = jnp.sum(s, -1) + e_int * jnp.einsum('bhctk,bhck->bhct', qc, n_in)
    h = num / jnp.maximum(jnp.abs(den), jnp.exp(-m_out))[..., None]
    return h.reshape(v.shape), final


def hgrn2_scan(seq, const, state):
    q, k, v, lf = seq
    cs = HGRN_CHUNK
    qc, kc, vc, lfc = (chunk(a, cs) for a in seq)
    a_cum = jnp.cumsum(lfc, axis=3)
    a_end = a_cum[:, :, :, -1]
    s_loc = jnp.einsum('bhcsk,bhcsv->bhckv', kc * jnp.exp(a_end[:, :, :, None] - a_cum), vc)

    def step(s_st, inp):
        a_j, sl_j = inp
        return jnp.exp(a_j)[..., None] * s_st + sl_j, s_st

    final, s_in = lax.scan(step, state, (jnp.moveaxis(a_end, 2, 0), jnp.moveaxis(s_loc, 2, 0)))
    s_in = jnp.moveaxis(s_in, 0, 2)
    inter = jnp.einsum('bhctk,bhckv->bhctv', qc * jnp.exp(a_cum), s_in)
    a_mid = a_cum[:, :, :, cs // 2 - 1:cs // 2]
    tri = jnp.tril(jnp.ones((cs, cs), bool))
    sc = jnp.einsum('bhctk,bhcsk->bhcts', qc * jnp.exp(a_cum - a_mid), kc * jnp.exp(a_mid - a_cum))
    sc = jnp.where(tri, sc, 0.0)
    out = inter + jnp.einsum('bhcts,bhcsv->bhctv', sc, vc)
    return out.reshape(v.shape), final


def retention_scan(seq, log_gamma, state):
    q, k, v = seq
    cs = RET_CHUNK
    qc, kc, vc = (chunk(a, cs) for a in seq)
    pos = jnp.arange(cs, dtype=F32)
    tri = jnp.tril(jnp.ones((cs, cs), bool))
    delta = pos[:, None] - pos[None, :]
    decay = jnp.exp(jnp.where(tri, delta[None] * log_gamma[:, None, None], -jnp.inf))
    sc = jnp.einsum('bhctk,bhcsk->bhcts', qc, kc) * decay[:, None]
    intra = jnp.einsum('bhcts,bhcsv->bhctv', sc, vc)
    q_dec = jnp.exp((pos + 1.0)[None, :] * log_gamma[:, None])
    k_dec = jnp.exp((cs - 1.0 - pos)[None, :] * log_gamma[:, None])
    s_loc = jnp.einsum('bhcsk,hs,bhcsv->bhckv', kc, k_dec, vc)
    chunk_decay = jnp.exp(cs * log_gamma)[None, :, None, None]

    def step(s_st, sl_j):
        return chunk_decay * s_st + sl_j, s_st

    final, s_in = lax.scan(step, state, jnp.moveaxis(s_loc, 2, 0))
    s_in = jnp.moveaxis(s_in, 0, 2)
    inter = jnp.einsum('bhctk,ht,bhckv->bhctv', qc, q_dec, s_in)
    return (inter + intra).reshape(v.shape), final


def rope_1d(x, pos):
    d = x.shape[-1]
    inv = ROPE_BASE ** (-jnp.arange(0, d, 2, dtype=F32) / d)
    ang = pos[:, None] * inv[None, :]
    cos = jnp.cos(ang)[None, :, None, :]
    sin = jnp.sin(ang)[None, :, None, :]
    x1, x2 = x[..., :d // 2], x[..., d // 2:]
    return jnp.concatenate([x1 * cos - x2 * sin, x1 * sin + x2 * cos], -1)


def axial_rope(x, row, col):
    half = x.shape[-1] // 2
    return jnp.concatenate([rope_1d(x[..., :half], row), rope_1d(x[..., half:], col)], -1)


def mlstm_group(cols_lat, cols_ctx, gate_b, norm_w):
    def prep(cols):
        q, k, v, o, gates = cols
        b, l, _ = q.shape
        q = to_heads(q, MLSTM_HEADS) * MLSTM_DQK ** -0.5
        k = to_heads(k, MLSTM_HEADS)
        v = to_heads(v, MLSTM_HEADS)
        gates = (gates.reshape(b, l, 4, MLSTM_HEADS) + gate_b).transpose(2, 0, 3, 1)
        fwd = (q, k, v, gates[0], jax.nn.log_sigmoid(gates[1]))
        bwd = (q, k, v, gates[2], jax.nn.log_sigmoid(gates[3]))
        return fwd, bwd, o

    fwd_l, bwd_l, o_l = prep(cols_lat)
    fwd_c, bwd_c, o_c = prep(cols_ctx)
    batch = o_l.shape[0]
    state0 = (jnp.zeros((batch, MLSTM_HEADS, MLSTM_DV, MLSTM_DQK), F32),
              jnp.zeros((batch, MLSTM_HEADS, MLSTM_DQK), F32),
              jnp.full((batch, MLSTM_HEADS), NEG_BIG, F32))
    hf_lat, hf_ctx = run_direction(mlstm_scan, fwd_c, fwd_l, None, state0, False)
    hb_lat, hb_ctx = run_direction(mlstm_scan, bwd_c, bwd_l, None, state0, True)
    y_lat = head_norm(hf_lat + hb_lat, norm_w, False) * jax.nn.sigmoid(o_l)
    y_ctx = head_norm(hf_ctx + hb_ctx, norm_w, False) * jax.nn.sigmoid(o_c)
    return y_lat, y_ctx


def hgrn_group(cols_lat, cols_ctx, lb, norm_w):
    def prep(cols):
        q, i, f_fwd, f_bwd, g = cols
        q = to_heads(jax.nn.silu(q), HGRN_HEADS)
        v = to_heads(i, HGRN_HEADS)

        def gate(fp):
            f = lb + (1.0 - lb) * jax.nn.sigmoid(fp)
            return to_heads(1.0 - f, HGRN_HEADS), to_heads(jnp.log(f), HGRN_HEADS)

        k_f, lf_f = gate(f_fwd)
        k_b, lf_b = gate(f_bwd)
        return (q, k_f, v, lf_f), (q, k_b, v, lf_b), g

    fwd_l, bwd_l, g_l = prep(cols_lat)
    fwd_c, bwd_c, g_c = prep(cols_ctx)
    state0 = jnp.zeros((g_l.shape[0], HGRN_HEADS, HGRN_DK, HGRN_DV), F32)
    yf_lat, yf_ctx = run_direction(hgrn2_scan, fwd_c, fwd_l, None, state0, False)
    yb_lat, yb_ctx = run_direction(hgrn2_scan, bwd_c, bwd_l, None, state0, True)
    y_lat = head_norm(yf_lat + yb_lat, norm_w, True) * jax.nn.silu(g_l)
    y_ctx = head_norm(yf_ctx + yb_ctx, norm_w, True) * jax.nn.silu(g_c)
    return y_lat, y_ctx


def hyena_filters(length, fw1, fb1, fw2, fb2, fw3, fb3, freq, wout):
    pos = jnp.arange(length, dtype=F32)
    t = (pos / max(length - 1, 1))[:, None]
    bands = (HYENA_EMB - 1) // 2
    w = 2.0 * math.pi * pos[:, None] / length * jnp.linspace(1e-4, bands - 1, bands)[None, :]
    z = jnp.concatenate([t, jnp.cos(w), jnp.sin(w)], -1)
    hid = jnp.sin(freq * (z @ fw1 + fb1))
    hid = jnp.sin(freq * (hid @ fw2 + fb2))
    hid = jnp.sin(freq * (hid @ fw3 + fb3))
    h = (hid @ wout).reshape(length, HYENA_ORDER, 2, HYENA_WIDTH)
    deltas = jnp.abs(jnp.linspace(math.log(HYENA_TARGET) / HYENA_SLOW_PCT, math.log(HYENA_TARGET) / HYENA_FAST_PCT, HYENA_WIDTH))
    decay = jnp.exp(-t * deltas[None, :])
    return (h * decay[:, None, None, :]).transpose(1, 2, 0, 3)


def bidir_fftconv(u, h_fwd, h_bwd, d_skip):
    length, width = h_fwd.shape
    kern = jnp.concatenate([h_fwd, jnp.zeros((1, width), F32), jnp.flip(h_bwd[1:], axis=0)], axis=0)
    spec = jnp.fft.rfft(u, n=2 * length, axis=1) * jnp.fft.rfft(kern, axis=0)[None]
    y = jnp.fft.irfft(spec, n=2 * length, axis=1)[:, :length]
    return y + u * d_skip


def hyena_group(u_lat, u_ctx, conv_w, conv_b, fw1, fb1, fw2, fb2, fw3, fb3, freq, wout, skip, norm_w):
    def run(u):
        length = u.shape[1]
        up = jnp.pad(u, ((0, 0), (1, 1), (0, 0)))
        u = up[:, :-2] * conv_w[0] + up[:, 1:-1] * conv_w[1] + up[:, 2:] * conv_w[2] + conv_b
        streams = jnp.split(u, HYENA_ORDER + 1, axis=-1)
        gates, z = streams[:-1], streams[-1]
        h = hyena_filters(length, fw1, fb1, fw2, fb2, fw3, fb3, freq, wout)
        for n in range(HYENA_ORDER):
            z = gates[n] * bidir_fftconv(z, h[n, 0], h[n, 1], skip[n])
        return group_rms(z, norm_w)

    return run(u_lat), run(u_ctx)


def retention_group(cols_lat, cols_ctx, log_decay, norm_w, row, col):
    def prep(cols, rotary):
        q, k, v, g = cols
        b, l, _ = q.shape
        q = q.reshape(b, l, RET_HEADS, RET_DQK)
        k = k.reshape(b, l, RET_HEADS, RET_DQK)
        if rotary:
            q = axial_rope(q, row, col)
            k = axial_rope(k, row, col)
        q = q.transpose(0, 2, 1, 3)
        k = k.transpose(0, 2, 1, 3) * RET_DQK ** -0.5
        return (q, k, to_heads(v, RET_HEADS)), g

    lat, g_l = prep(cols_lat, True)
    ctx, g_c = prep(cols_ctx, False)
    state0 = jnp.zeros((g_l.shape[0], RET_HEADS, RET_DQK, RET_DV), F32)
    yf_lat, yf_ctx = run_direction(retention_scan, ctx, lat, log_decay[0], state0, False)
    yb_lat, yb_ctx = run_direction(retention_scan, ctx, lat, log_decay[1], state0, True)
    y_lat = head_norm(yf_lat + yb_lat, norm_w, False) * jax.nn.silu(g_l)
    y_ctx = head_norm(yf_ctx + yb_ctx, norm_w, False) * jax.nn.silu(g_c)
    return y_lat, y_ctx


def even_mixer(u_lat, u_ctx, w_in, gate_b, mlstm_norm_w, lb, hgrn_norm_w, w_out):
    w_in = w_in.astype(F32)
    w_out = w_out.astype(F32)
    p_lat = split_cols(mm3(u_lat.astype(F32), w_in), EVEN_SPLITS)
    p_ctx = split_cols(mm3(u_ctx.astype(F32), w_in), EVEN_SPLITS)
    n_a = len(MLSTM_SPLITS)
    a_lat, a_ctx = mlstm_group(p_lat[:n_a], p_ctx[:n_a], gate_b.astype(F32), mlstm_norm_w.astype(F32))
    b_lat, b_ctx = hgrn_group(p_lat[n_a:], p_ctx[n_a:], lb, hgrn_norm_w.astype(F32))
    return mm3(jnp.concatenate([a_lat, b_lat], -1), w_out), mm3(jnp.concatenate([a_ctx, b_ctx], -1), w_out)


def odd_mixer(u_lat, u_ctx, w_in, hyena_p, log_decay, ret_norm_w, w_out, row, col):
    w_in = w_in.astype(F32)
    w_out = w_out.astype(F32)
    p_lat = split_cols(mm3(u_lat.astype(F32), w_in), ODD_SPLITS)
    p_ctx = split_cols(mm3(u_ctx.astype(F32), w_in), ODD_SPLITS)
    hy_lat, hy_ctx = hyena_group(p_lat[0], p_ctx[0], *[p.astype(F32) for p in hyena_p])
    r_lat, r_ctx = retention_group(p_lat[1:], p_ctx[1:], log_decay.astype(F32), ret_norm_w.astype(F32), row, col)
    return mm3(jnp.concatenate([hy_lat, r_lat], -1), w_out), mm3(jnp.concatenate([hy_ctx, r_ctx], -1), w_out)


ROUTE_TB = 512
DISPATCH_TB = 512
COMBINE_TB = 256
FFN_BM = 512


def _dot(a, b):
    return jnp.dot(a, b, preferred_element_type=F32)


def _route_kernel(v_ref, rw_ref, rb_ref, e_ref, rk_ref, g_ref, cnt_ref, carry_ref):
    tb, n_e = v_ref.shape[0], rw_ref.shape[1]

    @pl.when(pl.program_id(0) == 0)
    def _():
        carry_ref[...] = jnp.zeros_like(carry_ref)

    x = v_ref[...]
    w = rw_ref[...]
    xh = x.astype(BF16)
    xl = (x - xh.astype(F32)).astype(BF16)
    wh = w.astype(BF16)
    wl = (w - wh.astype(F32)).astype(BF16)
    logits = _dot(xh, wh) + (_dot(xh, wl) + _dot(xl, wh))
    scores = jax.nn.sigmoid(logits)
    work = scores + rb_ref[...]
    lane = lax.broadcasted_iota(jnp.int32, (tb, n_e), 1)
    col = lax.broadcasted_iota(jnp.int32, (tb, TOP_K), 1)
    e_out = jnp.zeros((tb, TOP_K), jnp.int32)
    g_out = jnp.zeros((tb, TOP_K), F32)
    mask = jnp.zeros((tb, n_e), F32)
    onehots = []
    for j in range(TOP_K):
        m = jnp.max(work, axis=1, keepdims=True)
        idx = jnp.min(jnp.where(work == m, lane, n_e), axis=1, keepdims=True)
        oh = lane == idx
        gj = jnp.sum(jnp.where(oh, scores, 0.0), axis=1, keepdims=True)
        e_out = jnp.where(col == j, idx, e_out)
        g_out = jnp.where(col == j, gj, g_out)
        work = jnp.where(oh, -jnp.inf, work)
        mask = mask + oh.astype(F32)
        onehots.append(oh)
    r_i = lax.broadcasted_iota(jnp.int32, (tb, tb), 0)
    c_i = lax.broadcasted_iota(jnp.int32, (tb, tb), 1)
    tri = (r_i > c_i).astype(BF16)
    rank = _dot(tri, mask.astype(BF16)) + carry_ref[...]
    rk_out = jnp.zeros((tb, TOP_K), jnp.int32)
    for j in range(TOP_K):
        rkj = jnp.sum(jnp.where(onehots[j], rank, 0.0), axis=1, keepdims=True)
        rk_out = jnp.where(col == j, rkj.astype(jnp.int32), rk_out)
    carry_ref[...] = carry_ref[...] + jnp.sum(mask, axis=0, keepdims=True)
    e_ref[...] = e_out
    rk_ref[...] = rk_out
    g_ref[...] = ROUTED_SCALE * g_out / jnp.sum(g_out, axis=1, keepdims=True)
    cnt_ref[...] = carry_ref[...]


def moe_route(v, router_w, router_bias):
    n_tok, d = v.shape
    tb = ROUTE_TB
    return pl.pallas_call(
        _route_kernel,
        grid=(n_tok // tb,),
        in_specs=[pl.BlockSpec((tb, d), lambda i: (i, 0)),
                  pl.BlockSpec((d, N_EXPERTS), lambda i: (0, 0)),
                  pl.BlockSpec((1, N_EXPERTS), lambda i: (0, 0))],
        out_specs=[pl.BlockSpec((tb, TOP_K), lambda i: (i, 0)),
                   pl.BlockSpec((tb, TOP_K), lambda i: (i, 0)),
                   pl.BlockSpec((tb, TOP_K), lambda i: (i, 0)),
                   pl.BlockSpec((1, N_EXPERTS), lambda i: (0, 0))],
        out_shape=[jax.ShapeDtypeStruct((n_tok, TOP_K), jnp.int32),
                   jax.ShapeDtypeStruct((n_tok, TOP_K), jnp.int32),
                   jax.ShapeDtypeStruct((n_tok, TOP_K), F32),
                   jax.ShapeDtypeStruct((1, N_EXPERTS), F32)],
        scratch_shapes=[pltpu.VMEM((1, N_EXPERTS), F32)],
        compiler_params=pltpu.CompilerParams(dimension_semantics=("arbitrary",)),
        name="moe_route",
    )(v, router_w, router_bias.reshape(1, N_EXPERTS))


U32 = jnp.uint32


def _pack_rows(x):
    w = x.shape[1] // 2
    lo = lax.bitcast_convert_type(x[:, :w].astype(BF16).astype(F32), U32)
    hi = lax.bitcast_convert_type(x[:, w:].astype(BF16).astype(F32), U32)
    return hi | (lo >> 16)


def _unpack_rows(u):
    lo = lax.bitcast_convert_type(u << 16, F32)
    hi = lax.bitcast_convert_type(u & jnp.uint32(0xFFFF0000), F32)
    return lo, hi


def _dispatch_kernel(zs_ref, zf_ref, nu_ref, pos_hbm, v_ref, xs_hbm, pos_smem, zbuf, pk_ref, sem_idx, sem_sc, sem_z):
    i = pl.program_id(0)
    tb = v_ref.shape[0]
    bm = zbuf.shape[0]
    n_blocks = xs_hbm.shape[0] // bm
    idx_cp = pltpu.make_async_copy(pos_hbm.at[i], pos_smem, sem_idx)
    idx_cp.start()
    pk_ref[...] = _pack_rows(v_ref[...])

    @pl.when(i == 0)
    def _():
        zbuf[...] = jnp.zeros_like(zbuf)
        for e in range(N_EXPERTS):
            @pl.when(zf_ref[e] != 0)
            def _():
                start = pl.multiple_of(zs_ref[e], bm)
                pltpu.make_async_copy(zbuf, xs_hbm.at[pl.ds(start, bm)], sem_z).start()
        for e in range(N_EXPERTS):
            @pl.when(zf_ref[e] != 0)
            def _():
                pltpu.make_async_copy(zbuf, xs_hbm.at[pl.ds(0, bm)], sem_z).wait()

        def tail(b, carry):
            cp = pltpu.make_async_copy(zbuf, xs_hbm.at[pl.ds(pl.multiple_of(b * bm, bm), bm)], sem_z)
            cp.start()
            cp.wait()
            return carry

        lax.fori_loop(nu_ref[0], n_blocks, tail, 0)

    idx_cp.wait()

    def body(t8, carry):
        row0 = pl.multiple_of(t8 * 8, 8)
        for r in range(8):
            for k in range(TOP_K):
                p = pos_smem[0, (row0 + r) * TOP_K + k]
                pltpu.make_async_copy(pk_ref.at[pl.ds(row0 + r, 1)], xs_hbm.at[pl.ds(p, 1)], sem_sc).start()
        return carry

    lax.fori_loop(0, tb // 8, body, 0)
    for k in range(TOP_K):
        pltpu.make_async_copy(pk_ref, pk_ref, sem_sc).wait()


def moe_dispatch(v, pos3, zero_start, zero_flag, n_used, n_rows):
    n_tok, d = v.shape
    tb = DISPATCH_TB
    dp = d // 2
    grid_spec = pltpu.PrefetchScalarGridSpec(
        num_scalar_prefetch=3,
        grid=(n_tok // tb,),
        in_specs=[pl.BlockSpec(memory_space=pl.ANY),
                  pl.BlockSpec((tb, d), lambda i, zs, zf, nu: (i, 0))],
        out_specs=pl.BlockSpec(memory_space=pl.ANY),
        scratch_shapes=[pltpu.SMEM((1, tb * TOP_K), jnp.int32),
                        pltpu.VMEM((FFN_BM, dp), U32),
                        pltpu.VMEM((tb, dp), U32),
                        pltpu.SemaphoreType.DMA(()),
                        pltpu.SemaphoreType.DMA(()),
                        pltpu.SemaphoreType.DMA(())],
    )
    return pl.pallas_call(
        _dispatch_kernel,
        grid_spec=grid_spec,
        out_shape=jax.ShapeDtypeStruct((n_rows, dp), U32),
        compiler_params=pltpu.CompilerParams(dimension_semantics=("arbitrary",)),
        name="moe_dispatch",
    )(zero_start, zero_flag, n_used, pos3, v)


def _ffn_kernel(be_ref, nu_ref, xs_ref, w1_ref, w3_ref, w2_ref, y_ref):
    used = pl.program_id(0) < nu_ref[0]

    @pl.when(used)
    def _():
        dp = xs_ref.shape[1]
        lo, hi = _unpack_rows(xs_ref[...])
        lo, hi = lo.astype(BF16), hi.astype(BF16)
        h1 = _dot(lo, w1_ref[0, :dp, :]) + _dot(hi, w1_ref[0, dp:, :])
        h3 = _dot(lo, w3_ref[0, :dp, :]) + _dot(hi, w3_ref[0, dp:, :])
        h = (h1 * jax.nn.sigmoid(h1) * h3).astype(BF16)
        y_ref[...] = _pack_rows(_dot(h, w2_ref[0]))

    @pl.when(jnp.logical_not(used))
    def _():
        y_ref[...] = jnp.zeros_like(y_ref)


def moe_expert_ffn(xs, block_e, n_used, w1, w3, w2):
    n_rows, dp = xs.shape
    d = 2 * dp
    bm = FFN_BM
    hid = w1.shape[2]

    def row_map(i, be, nu):
        return (jnp.minimum(i, nu[0] - 1), 0)

    def w_map(i, be, nu):
        return (be[jnp.minimum(i, nu[0] - 1)], 0, 0)

    grid_spec = pltpu.PrefetchScalarGridSpec(
        num_scalar_prefetch=2,
        grid=(n_rows // bm,),
        in_specs=[pl.BlockSpec((bm, dp), row_map),
                  pl.BlockSpec((1, d, hid), w_map),
                  pl.BlockSpec((1, d, hid), w_map),
                  pl.BlockSpec((1, hid, d), w_map)],
        out_specs=pl.BlockSpec((bm, dp), lambda i, be, nu: (i, 0)),
    )
    return pl.pallas_call(
        _ffn_kernel,
        grid_spec=grid_spec,
        out_shape=jax.ShapeDtypeStruct((n_rows, dp), U32),
        compiler_params=pltpu.CompilerParams(dimension_semantics=("arbitrary",)),
        name="moe_expert_ffn",
    )(block_e, n_used, xs, w1, w3, w2)


def _combine_kernel(pos_hbm, y_hbm, v_ref, g_ref, sw1_ref, sw3_ref, sw2_ref, h1_ref, mod_ref, lnw_ref, lnb_ref, o_ref,
                    pos_smem, ybuf, sem_idx, sem_y, *, n_ctx, blocks_per_batch):
    i = pl.program_id(0)
    n = pl.num_programs(0)
    tb = v_ref.shape[0]
    dp = ybuf.shape[1]
    slot = lax.rem(i, 2)

    def fetch_idx(step, s):
        return pltpu.make_async_copy(pos_hbm.at[step], pos_smem.at[s], sem_idx.at[s])

    def issue_gather(s):
        base = s * (TOP_K * tb)

        def body(t8, carry):
            row0 = pl.multiple_of(t8 * 8, 8)
            for r in range(8):
                for k in range(TOP_K):
                    p = pos_smem[s, 0, (row0 + r) * TOP_K + k]
                    pltpu.make_async_copy(y_hbm.at[pl.ds(p, 1)], ybuf.at[pl.ds(base + k * tb + row0 + r, 1)], sem_y.at[s]).start()
            return carry
        lax.fori_loop(0, tb // 8, body, 0)

    @pl.when(i == 0)
    def _():
        cp = fetch_idx(0, 0)
        cp.start()
        cp.wait()
        issue_gather(0)

    for s in range(2):
        @pl.when((i + 1 < n) & (slot == 1 - s))
        def _():
            cp = fetch_idx(i + 1, s)
            cp.start()
            cp.wait()
            issue_gather(s)

    x = v_ref[...].astype(BF16)
    h1 = _dot(x, sw1_ref[...])
    h3 = _dot(x, sw3_ref[...])
    shared = _dot((h1 * jax.nn.sigmoid(h1) * h3).astype(BF16), sw2_ref[...])
    base = pl.multiple_of(slot * (TOP_K * tb), TOP_K * tb)
    for k in range(TOP_K):
        rows = ybuf.at[pl.ds(base + k * tb, tb)]
        pltpu.make_async_copy(rows, rows, sem_y.at[slot]).wait()
    g = g_ref[...]
    acc_lo = shared[:, :dp]
    acc_hi = shared[:, dp:]
    for k in range(TOP_K):
        lo, hi = _unpack_rows(ybuf[pl.ds(base + k * tb, tb), :])
        acc_lo = acc_lo + g[:, k:k + 1] * lo
        acc_hi = acc_hi + g[:, k:k + 1] * hi
    f = jnp.concatenate([acc_lo, acc_hi], axis=1)
    is_ctx = _is_ctx_rows(tb, blocks_per_batch, n_ctx)
    z = DEEPNORM_ALPHA * h1_ref[...] + _row_mod(mod_ref, 5, is_ctx) * f
    o_ref[...] = _layer_norm_rows(z, lnw_ref[...], lnb_ref[...])


def moe_combine(pos3, y, v, gate, sw1, sw3, sw2, h1, modtab, ln_w, ln_b, batch, n_ctx):
    n_tok, d = v.shape
    tb = COMBINE_TB
    hid = sw1.shape[1]
    dp = y.shape[1]
    bpb = n_tok // batch // tb
    row = lambda i: (i, 0)
    fixed = lambda i: (0, 0)
    return pl.pallas_call(
        functools.partial(_combine_kernel, n_ctx=n_ctx, blocks_per_batch=bpb),
        grid=(n_tok // tb,),
        in_specs=[pl.BlockSpec(memory_space=pl.ANY),
                  pl.BlockSpec(memory_space=pl.ANY),
                  pl.BlockSpec((tb, d), row),
                  pl.BlockSpec((tb, TOP_K), row),
                  pl.BlockSpec((d, hid), fixed),
                  pl.BlockSpec((d, hid), fixed),
                  pl.BlockSpec((hid, d), fixed),
                  pl.BlockSpec((tb, d), row),
                  pl.BlockSpec((1, 2, 6, d), lambda i: (i // bpb, 0, 0, 0)),
                  pl.BlockSpec((1, d), fixed), pl.BlockSpec((1, d), fixed)],
        out_specs=pl.BlockSpec((tb, d), row),
        out_shape=jax.ShapeDtypeStruct((n_tok, d), F32),
        scratch_shapes=[pltpu.SMEM((2, 1, tb * TOP_K), jnp.int32),
                        pltpu.VMEM((2 * TOP_K * tb, dp), U32),
                        pltpu.SemaphoreType.DMA((2,)),
                        pltpu.SemaphoreType.DMA((2,))],
        compiler_params=pltpu.CompilerParams(dimension_semantics=("arbitrary",), vmem_limit_bytes=48 * 1024 * 1024),
        name="moe_combine",
    )(pos3, y, v, gate, sw1, sw3, sw2, h1, modtab, ln_w.reshape(1, d).astype(F32), ln_b.reshape(1, d).astype(F32))


def moe_ffn(t, router_w, router_bias, w1, w3, w2, sw1, sw3, sw2, h1, modtab, ln_w, ln_b, batch, n_ctx):
    n_tok, d = t.shape
    tf = t.astype(F32)
    bm = FFN_BM
    e_sel, rank, gate, cnt = moe_route(tf, router_w.astype(F32), router_bias.astype(F32))
    counts = cnt[0].astype(jnp.int32)
    padded = (counts + bm - 1) // bm * bm
    pend = jnp.cumsum(padded)
    pstart = pend - padded
    n_blocks = -(-(n_tok * TOP_K + N_EXPERTS * (bm - 1)) // bm)
    expert_ids = jnp.arange(N_EXPERTS, dtype=jnp.int32)
    pos = rank + jnp.sum(jnp.where(e_sel[..., None] == expert_ids, pstart, 0), -1)
    block_start = jnp.arange(n_blocks, dtype=jnp.int32) * bm
    block_e = jnp.minimum(jnp.sum((pend[None, :] <= block_start[:, None]).astype(jnp.int32), axis=1), N_EXPERTS - 1)
    n_used = (pend[-1:] // bm).astype(jnp.int32)
    zero_start = jnp.maximum(pend - bm, 0).astype(jnp.int32)
    zero_flag = (counts > 0).astype(jnp.int32)
    xs = moe_dispatch(tf, pos.reshape(n_tok // DISPATCH_TB, 1, DISPATCH_TB * TOP_K), zero_start, zero_flag, n_used, n_blocks * bm)
    y = moe_expert_ffn(xs, block_e, n_used, w1.astype(BF16), w3.astype(BF16), w2.astype(BF16))
    return moe_combine(pos.reshape(n_tok // COMBINE_TB, 1, COMBINE_TB * TOP_K), y, tf, gate,
                       sw1.astype(BF16), sw3.astype(BF16), sw2.astype(BF16), h1, modtab, ln_w, ln_b, batch, n_ctx)


HEAD_W = 128
MIX_VMEM = 48 * 1024 * 1024
NORM_ROWS = 256


def _dot_nt(a, b):
    return lax.dot_general(a, b, (((1,), (1,)), ((), ())), preferred_element_type=F32)


def _dot_tn(a, b):
    return lax.dot_general(a, b, (((0,), (0,)), ((), ())), preferred_element_type=F32)


def _dot_hi(a, b):
    bh = b.astype(BF16)
    bl = (b - bh.astype(F32)).astype(BF16)
    a16 = a.astype(BF16)
    return _dot(a16, bh) + _dot(a16, bl)


def _tri(n, upper):
    r = lax.broadcasted_iota(jnp.int32, (n, n), 0)
    c = lax.broadcasted_iota(jnp.int32, (n, n), 1)
    return (r <= c) if upper else (r >= c)


def _bidir_loop(n_ctx_chunks, n_chunks, step, unroll):
    def body(c, carry):
        cb = jnp.where(c < n_ctx_chunks, n_ctx_chunks - 1 - c, n_chunks - 1 + n_ctx_chunks - c)
        for ref, idx, val in step(c, False) + step(cb, True):
            ref[idx] = val
        return carry
    lax.fori_loop(0, n_chunks, body, 0, unroll=unroll)


def _norm_gate_epilogue(hf_ref, hb_ref, gate_ref, nw_ref, y_ref, n_rows, rms, gate_fn):
    def body(i, carry):
        rows = pl.ds(pl.multiple_of(i * NORM_ROWS, NORM_ROWS), NORM_ROWS)
        h = hf_ref[rows, :] + hb_ref[rows, :]
        if not rms:
            h = h - jnp.mean(h, axis=1, keepdims=True)
        h = h * lax.rsqrt(jnp.mean(h * h, axis=1, keepdims=True) + EPS)
        y_ref[rows, :] = (h * nw_ref[...] * gate_fn(gate_ref[rows, :].astype(F32))).astype(y_ref.dtype)
        return carry
    lax.fori_loop(0, n_rows // NORM_ROWS, body, 0)


def _silu(x):
    return x * jax.nn.sigmoid(x)


def _retention_kernel(ld_ref, q_ref, k_ref, v_ref, g_ref, cos_ref, sin_ref, perm_ref, nw_ref, y_ref,
                      qr_ref, kr_ref, hf_ref, hb_ref, sf_ref, sb_ref, *, n_ctx):
    cs = RET_CHUNK
    lt = q_ref.shape[0]
    hd = pl.program_id(1)

    def rope(i, carry):
        rows = pl.ds(pl.multiple_of(i * NORM_ROWS, NORM_ROWS), NORM_ROWS)
        cs_, sn_ = cos_ref[rows, :], sin_ref[rows, :]
        q = q_ref[rows, :]
        k = k_ref[rows, :]
        qr_ref[rows, :] = (q.astype(F32) * cs_ + _dot(q, perm_ref[...]) * sn_).astype(BF16)
        kr_ref[rows, :] = ((k.astype(F32) * cs_ + _dot(k, perm_ref[...]) * sn_) * RET_DQK ** -0.5).astype(BF16)
        return carry
    lax.fori_loop(0, lt // NORM_ROWS, rope, 0)

    pos_r = lax.broadcasted_iota(jnp.int32, (cs, cs), 0).astype(F32)
    pos_c = lax.broadcasted_iota(jnp.int32, (cs, cs), 1).astype(F32)
    pos_t = lax.broadcasted_iota(jnp.int32, (cs, HEAD_W), 0).astype(F32)
    consts = []
    for d in range(2):
        lg = ld_ref[d, hd]
        if d == 0:
            decay = jnp.where(pos_r >= pos_c, jnp.exp((pos_r - pos_c) * lg), 0.0)
            q_dec = jnp.exp((pos_t + 1.0) * lg)
            k_dec = jnp.exp((cs - 1.0 - pos_t) * lg)
        else:
            decay = jnp.where(pos_r <= pos_c, jnp.exp((pos_c - pos_r) * lg), 0.0)
            q_dec = jnp.exp((cs - pos_t) * lg)
            k_dec = jnp.exp(pos_t * lg)
        consts.append((decay, q_dec, k_dec, jnp.exp(cs * lg)))
    sf_ref[...] = jnp.zeros_like(sf_ref)
    sb_ref[...] = jnp.zeros_like(sb_ref)

    def step(chunk, reverse):
        decay, q_dec, k_dec, c_dec = consts[1 if reverse else 0]
        s_ref = sb_ref if reverse else sf_ref
        h_ref = hb_ref if reverse else hf_ref
        rows = pl.ds(pl.multiple_of(chunk * cs, cs), cs)
        q = qr_ref[rows, :]
        k = kr_ref[rows, :]
        v = v_ref[rows, :]
        sc = _dot_nt(q, k) * decay
        s_in = s_ref[...]
        inter = _dot((q.astype(F32) * q_dec).astype(BF16), s_in.astype(BF16))
        return [(h_ref, (rows, slice(None)), inter + _dot(sc.astype(BF16), v)),
                (s_ref, (slice(None), slice(None)), c_dec * s_in + _dot_tn((k.astype(F32) * k_dec).astype(BF16), v))]

    _bidir_loop(n_ctx // cs, lt // cs, step, 8)
    _norm_gate_epilogue(hf_ref, hb_ref, g_ref, nw_ref, y_ref, lt, False, _silu)


def retention_mixer(p16, col0, log_decay, norm_w, cos_t, sin_t, batch, n_ctx):
    lt = p16.shape[0] // batch
    nh = RET_HEADS
    half = RET_DQK // 2
    src = jnp.arange(HEAD_W)
    swapped = jnp.where((src % half) < half // 2, src + half // 2, src - half // 2)
    perm = (src[:, None] == swapped[None, :]).astype(BF16)

    def blk(off):
        return pl.BlockSpec((lt, HEAD_W), lambda b, h, ld: (b, off + h))

    grid_spec = pltpu.PrefetchScalarGridSpec(
        num_scalar_prefetch=1,
        grid=(batch, nh),
        in_specs=[blk(col0), blk(col0 + nh), blk(col0 + 2 * nh), blk(col0 + 3 * nh),
                  pl.BlockSpec((lt, HEAD_W), lambda b, h, ld: (0, 0)),
                  pl.BlockSpec((lt, HEAD_W), lambda b, h, ld: (0, 0)),
                  pl.BlockSpec((HEAD_W, HEAD_W), lambda b, h, ld: (0, 0)),
                  pl.BlockSpec((1, HEAD_W), lambda b, h, ld: (0, h))],
        out_specs=pl.BlockSpec((lt, HEAD_W), lambda b, h, ld: (b, h)),
        scratch_shapes=[pltpu.VMEM((lt, HEAD_W), BF16), pltpu.VMEM((lt, HEAD_W), BF16),
                        pltpu.VMEM((lt, HEAD_W), F32), pltpu.VMEM((lt, HEAD_W), F32),
                        pltpu.VMEM((RET_DQK, RET_DV), F32), pltpu.VMEM((RET_DQK, RET_DV), F32)],
    )
    return pl.pallas_call(
        functools.partial(_retention_kernel, n_ctx=n_ctx),
        grid_spec=grid_spec,
        out_shape=jax.ShapeDtypeStruct((batch * lt, nh * HEAD_W), BF16),
        compiler_params=pltpu.CompilerParams(dimension_semantics=("arbitrary", "arbitrary"), vmem_limit_bytes=MIX_VMEM),
        name="retention_mixer",
    )(log_decay.astype(F32), p16, p16, p16, p16, cos_t, sin_t, perm, norm_w.reshape(1, -1).astype(F32))


def _hgrn_kernel(q_ref, i_ref, g_ref, ff_ref, fb_ref, lb_ref, nw_ref, y_ref, hf_ref, hb_ref, sf_ref, sb_ref, *, n_ctx):
    cs = HGRN_CHUNK
    lt = q_ref.shape[0]
    lb = lb_ref[...]
    sf_ref[...] = jnp.zeros_like(sf_ref)
    sb_ref[...] = jnp.zeros_like(sb_ref)
    tri = (_tri(cs, False).astype(F32), _tri(cs, True).astype(F32))
    mask = (_tri(cs, False), _tri(cs, True))

    def step(chunk, reverse):
        d = 1 if reverse else 0
        f_ref = fb_ref if reverse else ff_ref
        s_ref = sb_ref if reverse else sf_ref
        h_ref = hb_ref if reverse else hf_ref
        rows = pl.ds(pl.multiple_of(chunk * cs, cs), cs)
        f = lb + (1.0 - lb) * jax.nn.sigmoid(f_ref[rows, :])
        kk = 1.0 - f
        a_cum = _dot_hi(tri[d], jnp.log(f))
        a_end = a_cum[0:1, :] if reverse else a_cum[cs - 1:cs, :]
        mid = cs // 2 if reverse else cs // 2 - 1
        a_mid = a_cum[mid:mid + 1, :]
        q = _silu(q_ref[rows, :].astype(F32))
        v = i_ref[rows, :]
        s_in = s_ref[...]
        inter = _dot_nt((q * jnp.exp(a_cum)).astype(BF16), s_in.astype(BF16))
        sc = _dot_nt((q * jnp.exp(a_cum - a_mid)).astype(BF16), (kk * jnp.exp(a_mid - a_cum)).astype(BF16))
        sc = jnp.where(mask[d], sc, 0.0)
        return [(h_ref, (rows, slice(None)), inter + _dot(sc.astype(BF16), v)),
                (s_ref, (slice(None), slice(None)), s_in * jnp.exp(a_end) + _dot_tn(v, (kk * jnp.exp(a_end - a_cum)).astype(BF16)))]

    _bidir_loop(n_ctx // cs, lt // cs, step, 4)
    _norm_gate_epilogue(hf_ref, hb_ref, g_ref, nw_ref, y_ref, lt, True, _silu)


EVEN16_MLSTM_Q, EVEN16_MLSTM_K, EVEN16_MLSTM_V, EVEN16_MLSTM_O = 0, 2, 4, 8
EVEN16_HGRN_Q, EVEN16_HGRN_I, EVEN16_HGRN_G = 12, 16, 20
EVEN32_GATES, EVEN32_FF, EVEN32_FB = 0, 1, 5


def hgrn_mixer(p16, p32, lb, norm_w, batch, n_ctx):
    lt = p16.shape[0] // batch
    nh = HGRN_HEADS

    def blk(off):
        return pl.BlockSpec((lt, HEAD_W), lambda b, h: (b, off + h))

    def vec():
        return pl.BlockSpec((1, HEAD_W), lambda b, h: (0, h))

    return pl.pallas_call(
        functools.partial(_hgrn_kernel, n_ctx=n_ctx),
        grid=(batch, nh),
        in_specs=[blk(EVEN16_HGRN_Q), blk(EVEN16_HGRN_I), blk(EVEN16_HGRN_G), blk(EVEN32_FF), blk(EVEN32_FB), vec(), vec()],
        out_specs=pl.BlockSpec((lt, HEAD_W), lambda b, h: (b, h)),
        out_shape=jax.ShapeDtypeStruct((batch * lt, nh * HEAD_W), BF16),
        scratch_shapes=[pltpu.VMEM((lt, HEAD_W), F32), pltpu.VMEM((lt, HEAD_W), F32),
                        pltpu.VMEM((HGRN_DV, HGRN_DK), F32), pltpu.VMEM((HGRN_DV, HGRN_DK), F32)],
        compiler_params=pltpu.CompilerParams(dimension_semantics=("arbitrary", "arbitrary"), vmem_limit_bytes=MIX_VMEM),
        name="hgrn_mixer",
    )(p16, p16, p16, p32, p32, lb.reshape(1, -1).astype(F32), norm_w.reshape(1, -1).astype(F32))


def _mlstm_kernel(q_ref, k_ref, v_ref, o_ref, gt_ref, gb_ref, nw_ref, y_ref, hf_ref, hb_ref, cf_ref, cb_ref, mf_ref, mb_ref, *, n_ctx):
    cs = MLSTM_CHUNK
    lt = q_ref.shape[0]
    hd = pl.program_id(1)
    lane = lax.broadcasted_iota(jnp.int32, (1, HEAD_W), 1)
    head_mask = ((lane // MLSTM_DQK) == (hd % 2)).astype(F32)
    gcol = lax.broadcasted_iota(jnp.int32, (1, HEAD_W), 1)
    tri = (_tri(cs, False).astype(F32), _tri(cs, True).astype(F32))
    mask = (_tri(cs, False), _tri(cs, True))
    ones_v = jnp.ones((cs, HEAD_W), BF16)
    for ref in (cf_ref, cb_ref):
        ref[...] = jnp.zeros_like(ref)
    for ref in (mf_ref, mb_ref):
        ref[...] = jnp.full_like(ref, NEG_BIG)

    def pick(tile, col):
        return jnp.sum(jnp.where(gcol == col, tile, 0.0), axis=1, keepdims=True)

    def step(chunk, reverse):
        d = 1 if reverse else 0
        c_ref = cb_ref if reverse else cf_ref
        m_ref = mb_ref if reverse else mf_ref
        h_ref = hb_ref if reverse else hf_ref
        rows = pl.ds(pl.multiple_of(chunk * cs, cs), cs)
        x = gt_ref[rows, :] + gb_ref[...]
        ci = hd + (2 * MLSTM_HEADS if reverse else 0)
        ig = pick(x, ci)
        lf = pick(jax.nn.log_sigmoid(x), ci + MLSTM_HEADS)
        lf_t = jnp.broadcast_to(lf, (cs, HEAD_W))
        b = _dot_hi(tri[d], lf_t)
        g = b[0:1, :] if reverse else b[cs - 1:cs, :]
        ig_t = jnp.broadcast_to(ig, (cs, HEAD_W))
        a = g - b + ig_t
        m_loc = jnp.max(a, axis=0, keepdims=True)
        w = jnp.exp(a - m_loc)
        m_in = m_ref[...]
        r_row = jnp.transpose(ig_t - b)[0:1, 0:cs]
        log_d = jnp.where(mask[d], b[:, 0:cs] + r_row, -jnp.inf)
        inter = b + m_in
        m_out = jnp.maximum(inter, jnp.max(log_d, axis=1, keepdims=True))
        q = q_ref[rows, :]
        k = k_ref[rows, :]
        v = v_ref[rows, :]
        qm = (q.astype(F32) * (head_mask * MLSTM_DQK ** -0.5)).astype(BF16)
        s = _dot_nt(qm, k) * jnp.exp(log_d - m_out[:, 0:cs])
        e_int = jnp.exp(inter - m_out)
        vo = jnp.concatenate([v, ones_v], axis=1)
        c_in = c_ref[...]
        r = _dot(s.astype(BF16), vo) + jnp.concatenate([e_int, e_int], axis=1) * _dot(qm, c_in.astype(BF16))
        num, den = r[:, :HEAD_W], r[:, HEAD_W:]
        h_out = num / jnp.maximum(jnp.abs(den), jnp.exp(-m_out))
        m_new = jnp.maximum(g + m_in, m_loc)
        sp = jnp.exp(g + m_in - m_new)
        sl = jnp.exp(m_loc - m_new)
        kw = (k.astype(F32) * head_mask * w).astype(BF16)
        c_loc = _dot_tn(kw, vo)
        c_new = jnp.concatenate([sp, sp], axis=1) * c_in + jnp.concatenate([sl, sl], axis=1) * c_loc
        full = (slice(None), slice(None))
        return [(h_ref, (rows, slice(None)), h_out), (c_ref, full, c_new), (m_ref, full, m_new)]

    _bidir_loop(n_ctx // cs, lt // cs, step, 4)
    _norm_gate_epilogue(hf_ref, hb_ref, o_ref, nw_ref, y_ref, lt, False, jax.nn.sigmoid)


def mlstm_mixer(p16, p32, gate_b, norm_w, batch, n_ctx):
    lt = p16.shape[0] // batch
    nh = MLSTM_HEADS
    gb = jnp.zeros((1, HEAD_W), F32).at[0, :4 * nh].set(gate_b.astype(F32).reshape(-1))

    return pl.pallas_call(
        functools.partial(_mlstm_kernel, n_ctx=n_ctx),
        grid=(batch, nh),
        in_specs=[pl.BlockSpec((lt, HEAD_W), lambda b, h: (b, EVEN16_MLSTM_Q + h // 2)),
                  pl.BlockSpec((lt, HEAD_W), lambda b, h: (b, EVEN16_MLSTM_K + h // 2)),
                  pl.BlockSpec((lt, HEAD_W), lambda b, h: (b, EVEN16_MLSTM_V + h)),
                  pl.BlockSpec((lt, HEAD_W), lambda b, h: (b, EVEN16_MLSTM_O + h)),
                  pl.BlockSpec((lt, HEAD_W), lambda b, h: (b, EVEN32_GATES)),
                  pl.BlockSpec((1, HEAD_W), lambda b, h: (0, 0)),
                  pl.BlockSpec((1, HEAD_W), lambda b, h: (0, h))],
        out_specs=pl.BlockSpec((lt, HEAD_W), lambda b, h: (b, h)),
        out_shape=jax.ShapeDtypeStruct((batch * lt, nh * HEAD_W), BF16),
        scratch_shapes=[pltpu.VMEM((lt, HEAD_W), F32), pltpu.VMEM((lt, HEAD_W), F32),
                        pltpu.VMEM((HEAD_W, 2 * HEAD_W), F32), pltpu.VMEM((HEAD_W, 2 * HEAD_W), F32),
                        pltpu.VMEM((1, HEAD_W), F32), pltpu.VMEM((1, HEAD_W), F32)],
        compiler_params=pltpu.CompilerParams(dimension_semantics=("arbitrary", "arbitrary"), vmem_limit_bytes=MIX_VMEM),
        name="mlstm_mixer",
    )(p16, p16, p16, p16, p32, gb, norm_w.reshape(1, -1).astype(F32))


ROW_BLOCKS_PER_BATCH_IN = 4
ROW_BLOCKS_PER_BATCH_OUT = 8
PROJ_TN = 512


def _row_mod(mod_ref, idx, is_ctx):
    return jnp.where(is_ctx, mod_ref[0, 0, idx:idx + 1, :], mod_ref[0, 1, idx:idx + 1, :])


def _is_ctx_rows(tm, blocks_per_batch, n_ctx):
    row = (pl.program_id(0) % blocks_per_batch) * tm + lax.broadcasted_iota(jnp.int32, (tm, 1), 0)
    return row < n_ctx


def _proj_in_kernel(h_ref, mod_ref, w_ref, o_ref, u_ref, *, n_ctx, blocks_per_batch):
    tm = h_ref.shape[0]

    @pl.when(pl.program_id(1) == 0)
    def _():
        is_ctx = _is_ctx_rows(tm, blocks_per_batch, n_ctx)
        u = h_ref[...] * (1.0 + _row_mod(mod_ref, 1, is_ctx)) + _row_mod(mod_ref, 0, is_ctx)
        u_ref[...] = u.astype(BF16)

    o_ref[...] = _dot(u_ref[...], w_ref[...]).astype(o_ref.dtype)


def proj_in(h, modtab, w, out_dtype, batch, n_ctx, tn=PROJ_TN):
    t, d = h.shape
    n = w.shape[1]
    bpb = ROW_BLOCKS_PER_BATCH_IN
    tm = t // batch // bpb
    return pl.pallas_call(
        functools.partial(_proj_in_kernel, n_ctx=n_ctx, blocks_per_batch=bpb),
        grid=(t // tm, n // tn),
        in_specs=[pl.BlockSpec((tm, d), lambda i, j: (i, 0)),
                  pl.BlockSpec((1, 2, 6, d), lambda i, j: (i // bpb, 0, 0, 0)),
                  pl.BlockSpec((d, tn), lambda i, j: (0, j))],
        out_specs=pl.BlockSpec((tm, tn), lambda i, j: (i, j)),
        out_shape=jax.ShapeDtypeStruct((t, n), out_dtype),
        scratch_shapes=[pltpu.VMEM((tm, d), BF16)],
        compiler_params=pltpu.CompilerParams(dimension_semantics=("arbitrary", "arbitrary"), vmem_limit_bytes=MIX_VMEM),
        name="proj_in",
    )(h, modtab, w)


def _layer_norm_rows(z, w, b):
    mu = jnp.mean(z, axis=1, keepdims=True)
    zc = z - mu
    var = jnp.mean(zc * zc, axis=1, keepdims=True)
    return zc * lax.rsqrt(var + EPS) * w + b


def _proj_out_kernel(ya_ref, yb_ref, wa_ref, wb_ref, h_ref, mod_ref, lnw_ref, lnb_ref, h1_ref, v_ref, *, n_ctx, blocks_per_batch):
    tm = h_ref.shape[0]
    is_ctx = _is_ctx_rows(tm, blocks_per_batch, n_ctx)
    y = _dot(ya_ref[...], wa_ref[...]) + _dot(yb_ref[...], wb_ref[...])
    z = DEEPNORM_ALPHA * h_ref[...] + _row_mod(mod_ref, 2, is_ctx) * y
    h1 = _layer_norm_rows(z, lnw_ref[...], lnb_ref[...])
    h1_ref[...] = h1
    v_ref[...] = h1 * (1.0 + _row_mod(mod_ref, 4, is_ctx)) + _row_mod(mod_ref, 3, is_ctx)


def proj_out_ln(ya, yb, wa, wb, h, modtab, ln_w, ln_b, batch, n_ctx):
    t, d = h.shape
    bpb = ROW_BLOCKS_PER_BATCH_OUT
    tm = t // batch // bpb
    ka, kb = ya.shape[1], yb.shape[1]
    row = lambda i: (i, 0)
    fixed = lambda i: (0, 0)
    return pl.pallas_call(
        functools.partial(_proj_out_kernel, n_ctx=n_ctx, blocks_per_batch=bpb),
        grid=(t // tm,),
        in_specs=[pl.BlockSpec((tm, ka), row), pl.BlockSpec((tm, kb), row),
                  pl.BlockSpec((ka, d), fixed), pl.BlockSpec((kb, d), fixed),
                  pl.BlockSpec((tm, d), row),
                  pl.BlockSpec((1, 2, 6, d), lambda i: (i // bpb, 0, 0, 0)),
                  pl.BlockSpec((1, d), fixed), pl.BlockSpec((1, d), fixed)],
        out_specs=[pl.BlockSpec((tm, d), row), pl.BlockSpec((tm, d), row)],
        out_shape=[jax.ShapeDtypeStruct((t, d), F32), jax.ShapeDtypeStruct((t, d), F32)],
        compiler_params=pltpu.CompilerParams(dimension_semantics=("arbitrary",), vmem_limit_bytes=MIX_VMEM),
        name="proj_out_ln",
    )(ya, yb, wa, wb, h, modtab, ln_w.reshape(1, d).astype(F32), ln_b.reshape(1, d).astype(F32))


FFT_N2 = 128
HY_LANES = 128
HY_VMEM = 56 * 1024 * 1024


def _np_cplx_tables(n_len):
    import numpy as np
    n = 2 * n_len
    n2 = FFT_N2
    n1 = n // n2
    w = lambda m, e: np.exp(-2j * np.pi * (e % m) / m)
    k1 = np.arange(n1)
    a = np.arange(n1)
    j2 = np.arange(n2)
    ta = w(n, j2[:, None, None] * k1[None, :, None]) * w(n1, k1[None, :, None] * a[None, None, :])
    ta_stack = np.concatenate([ta.real, ta.imag], axis=1)
    tinv = np.conj(ta).transpose(0, 2, 1) / n
    tinv_stack = np.concatenate([tinv.real, -tinv.imag], axis=2)[:, :n1 // 2]
    f2 = w(n2, j2[:, None] * j2[None, :])
    sb = np.block([[f2.real, -f2.imag], [f2.imag, f2.real]])
    sbi = np.block([[f2.real, f2.imag], [-f2.imag, f2.real]])
    return ta_stack, tinv_stack, sb, sbi


def _np_dense_dft(n_len):
    import numpy as np
    n = 2 * n_len
    k = np.arange(n)
    f = np.exp(-2j * np.pi * ((k[:, None] * k[None, :]) % n) / n)
    fwd = np.concatenate([f.real, f.imag], axis=0)
    g = np.conj(f)[:n_len] / n
    inv = np.concatenate([g.real, -g.imag], axis=1)
    return fwd, inv


def _fft_stage_a(load_rows, ta_ref, b_ref, n1_in, n1):
    def body(j, carry):
        x = load_rows(j, n1_in).astype(BF16)
        a = _dot(ta_ref[j], x)
        b_ref[0, pl.ds(j, n1, stride=FFT_N2), :] = a[:n1]
        b_ref[1, pl.ds(j, n1, stride=FFT_N2), :] = a[n1:]
        return carry
    lax.fori_loop(0, FFT_N2, body, 0, unroll=8)


def _fft_stage_b_rows(b_ref, sb_ref, k1):
    rows = pl.ds(pl.multiple_of(k1 * FFT_N2, FFT_N2), FFT_N2)
    rhs = jnp.concatenate([b_ref[0, rows, :], b_ref[1, rows, :]], axis=0).astype(BF16)
    return rows, _dot(sb_ref[...], rhs)


def _hyena_spectrum_kernel(k_ref, ta_ref, sb_ref, o_ref, b_ref):
    n1 = b_ref.shape[1] // FFT_N2
    _fft_stage_a(lambda j, cnt: k_ref[0, pl.ds(j, cnt, stride=FFT_N2), :], ta_ref, b_ref, n1, n1)

    def body(k1, carry):
        rows, x = _fft_stage_b_rows(b_ref, sb_ref, k1)
        o_ref[0, 0, rows, :] = x[:FFT_N2]
        o_ref[0, 1, rows, :] = x[FFT_N2:]
        return carry
    lax.fori_loop(0, n1, body, 0)


def hyena_spectrum(kern):
    n_ord, n, width = kern.shape
    n1 = n // FFT_N2
    ta, _, sb, _ = _np_cplx_tables(n // 2)
    return pl.pallas_call(
        _hyena_spectrum_kernel,
        grid=(n_ord, width // HY_LANES),
        in_specs=[pl.BlockSpec((1, n, HY_LANES), lambda o, c: (o, 0, c)),
                  pl.BlockSpec((FFT_N2, 2 * n1, n1), lambda o, c: (0, 0, 0)),
                  pl.BlockSpec((2 * FFT_N2, 2 * FFT_N2), lambda o, c: (0, 0))],
        out_specs=pl.BlockSpec((1, 2, n, HY_LANES), lambda o, c: (o, 0, 0, c)),
        out_shape=jax.ShapeDtypeStruct((n_ord, 2, n, width), F32),
        scratch_shapes=[pltpu.VMEM((2, n, HY_LANES), F32)],
        compiler_params=pltpu.CompilerParams(dimension_semantics=("arbitrary", "arbitrary"), vmem_limit_bytes=HY_VMEM),
        name="hyena_spectrum",
    )(kern, jnp.asarray(ta, BF16), jnp.asarray(sb, BF16))


def _short_conv(x_ref, w_ref, b_ref, n_ctx):
    x = x_ref[...].astype(F32)
    lt = x.shape[0]
    row = lax.broadcasted_iota(jnp.int32, (lt, 1), 0)
    prev = jnp.where((row == 0) | (row == n_ctx), 0.0, pltpu.roll(x, 1, axis=0))
    nxt = jnp.where((row == n_ctx - 1) | (row == lt - 1), 0.0, pltpu.roll(x, lt - 1, axis=0))
    return prev * w_ref[0:1, :] + x * w_ref[1:2, :] + nxt * w_ref[2:3, :] + b_ref[...]


def _hyena_conv_kernel(z_ref, g_ref, wz_ref, bz_ref, wg_ref, bg_ref, kl_ref, kc_ref, ta_ref, ti_ref, sb_ref, sbi_ref,
                       fc_ref, gc_ref, skip_ref, nw_ref, gm_ref, y_ref, zbuf, b_ref, *, n_ctx):
    order = pl.program_id(2)
    lt = zbuf.shape[0]
    n_lat = lt - n_ctx
    n1 = b_ref.shape[1] // FFT_N2

    @pl.when(order == 0)
    def _():
        zbuf[...] = _short_conv(z_ref, wz_ref, bz_ref, n_ctx)

    _fft_stage_a(lambda j, cnt: zbuf[pl.ds(n_ctx + j, cnt, stride=FFT_N2), :], ta_ref, b_ref, n1 // 2, n1)

    def freq(k1, carry):
        rows, x = _fft_stage_b_rows(b_ref, sb_ref, k1)
        xr, xi = x[:FFT_N2], x[FFT_N2:]
        kr, ki = kl_ref[0, 0, rows, :], kl_ref[0, 1, rows, :]
        y = jnp.concatenate([xr * kr - xi * ki, xr * ki + xi * kr], axis=0).astype(BF16)
        c = _dot(sbi_ref[...], y)
        b_ref[0, rows, :] = c[:FFT_N2]
        b_ref[1, rows, :] = c[FFT_N2:]
        return carry
    lax.fori_loop(0, n1, freq, 0, unroll=8)

    def inv_a(j, carry):
        rhs = jnp.concatenate([b_ref[0, pl.ds(j, n1, stride=FFT_N2), :], b_ref[1, pl.ds(j, n1, stride=FFT_N2), :]], axis=0)
        b_ref[0, pl.ds(j, n1 // 2, stride=FFT_N2), :] = _dot(ti_ref[j], rhs.astype(BF16))
        return carry
    lax.fori_loop(0, FFT_N2, inv_a, 0, unroll=8)

    zc = zbuf[0:n_ctx, :]
    xc = _dot(fc_ref[...], zc.astype(BF16))
    nc = 2 * n_ctx
    xr, xi = xc[:nc], xc[nc:]
    kr, ki = kc_ref[0, 0], kc_ref[0, 1]
    yc = jnp.concatenate([xr * kr - xi * ki, xr * ki + xi * kr], axis=0).astype(BF16)
    conv_c = _dot(gc_ref[...], yc)

    gate = _short_conv(g_ref, wg_ref, bg_ref, n_ctx)
    skip = skip_ref[0]
    zbuf[0:n_ctx, :] = gate[0:n_ctx] * (conv_c + zc * skip)
    zbuf[n_ctx:lt, :] = gate[n_ctx:lt] * (b_ref[0, 0:n_lat, :] + zbuf[n_ctx:lt, :] * skip)

    @pl.when(order == pl.num_programs(2) - 1)
    def _():
        z = zbuf[...]
        sq = z * z
        sq_hi = sq.astype(BF16)
        sq_lo = (sq - sq_hi.astype(F32)).astype(BF16)
        gm = gm_ref[...].astype(BF16)
        ms = _dot(sq_hi, gm) + _dot(sq_lo, gm)
        y_ref[...] = (z * lax.rsqrt(ms + EPS) * nw_ref[...]).astype(y_ref.dtype)


def hyena_mixer(p16, conv_w, conv_b, kf_lat, kf_ctx, skip, norm_w, batch, n_ctx):
    import numpy as np
    lt = p16.shape[0] // batch
    n_lat = lt - n_ctx
    width = HYENA_WIDTH
    cb = width // HY_LANES
    n = 2 * n_lat
    n1 = n // FFT_N2
    ta, tinv, sb, sbi = _np_cplx_tables(n_lat)
    ta = ta[:, :, :n1 // 2]
    fc, gc = _np_dense_dft(n_ctx)
    fc = fc[:, :n_ctx]
    gsz = width // HYENA_GROUPS
    gidx = np.arange(HY_LANES) // gsz
    gmean = (gidx[:, None] == gidx[None, :]).astype(np.float32) / gsz
    z_blk = HYENA_ORDER * cb
    cw = conv_w.astype(F32)
    cbias = conv_b.astype(F32).reshape(1, -1)
    const2 = lambda b, c, o: (0, 0)
    const3 = lambda b, c, o: (0, 0, 0)
    return pl.pallas_call(
        functools.partial(_hyena_conv_kernel, n_ctx=n_ctx),
        grid=(batch, cb, HYENA_ORDER),
        in_specs=[pl.BlockSpec((lt, HY_LANES), lambda b, c, o: (b, z_blk + c)),
                  pl.BlockSpec((lt, HY_LANES), lambda b, c, o: (b, o * cb + c)),
                  pl.BlockSpec((HYENA_SHORT, HY_LANES), lambda b, c, o: (0, z_blk + c)),
                  pl.BlockSpec((1, HY_LANES), lambda b, c, o: (0, z_blk + c)),
                  pl.BlockSpec((HYENA_SHORT, HY_LANES), lambda b, c, o: (0, o * cb + c)),
                  pl.BlockSpec((1, HY_LANES), lambda b, c, o: (0, o * cb + c)),
                  pl.BlockSpec((1, 2, n, HY_LANES), lambda b, c, o: (o, 0, 0, c)),
                  pl.BlockSpec((1, 2, 2 * n_ctx, HY_LANES), lambda b, c, o: (o, 0, 0, c)),
                  pl.BlockSpec((FFT_N2, 2 * n1, n1 // 2), const3),
                  pl.BlockSpec((FFT_N2, n1 // 2, 2 * n1), const3),
                  pl.BlockSpec((2 * FFT_N2, 2 * FFT_N2), const2),
                  pl.BlockSpec((2 * FFT_N2, 2 * FFT_N2), const2),
                  pl.BlockSpec((4 * n_ctx, n_ctx), const2),
                  pl.BlockSpec((n_ctx, 4 * n_ctx), const2),
                  pl.BlockSpec((1, 1, HY_LANES), lambda b, c, o: (o, 0, c)),
                  pl.BlockSpec((1, HY_LANES), lambda b, c, o: (0, c)),
                  pl.BlockSpec((HY_LANES, HY_LANES), const2)],
        out_specs=pl.BlockSpec((lt, HY_LANES), lambda b, c, o: (b, c)),
        out_shape=jax.ShapeDtypeStruct((batch * lt, width), BF16),
        scratch_shapes=[pltpu.VMEM((lt, HY_LANES), F32), pltpu.VMEM((2, n, HY_LANES), F32)],
        compiler_params=pltpu.CompilerParams(dimension_semantics=("arbitrary", "arbitrary", "arbitrary"), vmem_limit_bytes=HY_VMEM),
        name="hyena_mixer",
    )(p16, p16, cw, cbias, cw, cbias, kf_lat, kf_ctx,
      jnp.asarray(ta, BF16), jnp.asarray(tinv, BF16), jnp.asarray(sb, BF16), jnp.asarray(sbi, BF16),
      jnp.asarray(fc, BF16), jnp.asarray(gc, BF16), skip.astype(F32).reshape(HYENA_ORDER, 1, width),
      norm_w.astype(F32).reshape(1, width), jnp.asarray(gmean, F32))


def hyena_spectrum_dense(kern):
    n_ord, n, width = kern.shape
    fwd, _ = _np_dense_dft(n // 2)
    spec = pallas_matmul(jnp.asarray(fwd, F32), kern.transpose(1, 0, 2).reshape(n, n_ord * width), tm=2 * n, tn=width)
    return spec.reshape(2, n, n_ord, width).transpose(2, 0, 1, 3)


def hyena_kernels(length, fw1, fb1, fw2, fb2, fw3, fb3, freq, wout):
    h = hyena_filters(length, fw1, fb1, fw2, fb2, fw3, fb3, freq, wout)
    zero = jnp.zeros((HYENA_ORDER, 1, HYENA_WIDTH), F32)
    return jnp.concatenate([h[:, 0], zero, jnp.flip(h[:, 1, 1:], axis=1)], axis=1)


def rope_tables(n_ctx, n_lat):
    half = RET_DQK // 2
    inv = ROPE_BASE ** (-jnp.arange(0, half, 2, dtype=F32) / half)
    t = jnp.arange(n_lat)
    row = (t // GRID_W).astype(F32)
    col = (t % GRID_W).astype(F32)
    a_row = row[:, None] * inv[None, :]
    a_col = col[:, None] * inv[None, :]
    cos = jnp.concatenate([jnp.cos(a_row), jnp.cos(a_row), jnp.cos(a_col), jnp.cos(a_col)], -1)
    sin = jnp.concatenate([-jnp.sin(a_row), jnp.sin(a_row), -jnp.sin(a_col), jnp.sin(a_col)], -1)
    cos = jnp.concatenate([jnp.ones((n_ctx, HEAD_W), F32), cos], 0)
    sin = jnp.concatenate([jnp.zeros((n_ctx, HEAD_W), F32), sin], 0)
    return cos, sin


def kernel(x, c, ctx, c_ctx, ada_w, ada_b, ln_w, ln_b, even_w_in, mlstm_gate_b, mlstm_norm_w, hgrn_lb, hgrn_norm_w, even_w_out, odd_w_in, hy_conv_w, hy_conv_b, hy_f_w1, hy_f_b1, hy_f_w2, hy_f_b2, hy_f_w3, hy_f_b3, hy_f_freq, hy_f_wout, hy_skip, hy_norm_w, ret_log_decay, ret_norm_w, odd_w_out, router_w, router_bias, exp_w1, exp_w3, exp_w2, sh_w1, sh_w3, sh_w2):
    batch, n_lat, d = x.shape
    n_ctx = ctx.shape[1]
    lt = n_ctx + n_lat
    t = batch * lt
    sm = jax.nn.softmax(hgrn_lb.astype(F32), axis=0)
    lower_bounds = jnp.cumsum(sm, axis=0) - sm[0]
    acts = jnp.concatenate([jax.nn.silu(c.astype(F32)), jax.nn.silu(c_ctx.astype(F32))[None]], 0)
    acts = jnp.pad(acts, ((0, 16 - (batch + 1) % 16), (0, 0)))
    cos_t, sin_t = rope_tables(n_ctx, n_lat)
    is_ctx = (jnp.arange(lt) < n_ctx)[None, :, None]
    h = jnp.concatenate([ctx, x], axis=1).reshape(t, d).astype(F32)
    for l in range(DEPTH):
        mods = pallas_matmul(acts, ada_w[l].astype(F32), tm=acts.shape[0], tn=512)[:batch + 1] + ada_b[l]
        mod_lat = mods[:batch].reshape(batch, 6, d)
        mod_ctx = jnp.broadcast_to(mods[batch].reshape(1, 6, d), (batch, 6, d))
        modtab = jnp.stack([mod_ctx, mod_lat], axis=1)
        if l % 2 == 0:
            e = l // 2
            w = even_w_in[e].astype(F32)
            c0 = sum(MLSTM_SPLITS[:4])
            c1 = c0 + MLSTM_SPLITS[4]
            c2 = c1 + HGRN_SPLITS[0] + HGRN_SPLITS[1]
            c3 = c2 + HGRN_SPLITS[2] + HGRN_SPLITS[3]
            w16 = jnp.concatenate([w[:, :c0], w[:, c1:c2], w[:, c3:]], 1).astype(BF16)
            w32 = jnp.concatenate([w[:, c0:c1], jnp.zeros((d, HEAD_W - MLSTM_SPLITS[4]), F32), w[:, c2:c3]], 1).astype(BF16)
            p16 = proj_in(h, modtab, w16, BF16, batch, n_ctx, tn=w16.shape[1] // 2)
            p32 = proj_in(h, modtab, w32, F32, batch, n_ctx, tn=w32.shape[1])
            ya = mlstm_mixer(p16, p32, mlstm_gate_b[e], mlstm_norm_w[e], batch, n_ctx)
            yb = hgrn_mixer(p16, p32, lower_bounds[e], hgrn_norm_w[e], batch, n_ctx)
            w_out = even_w_out[e]
        else:
            o = l // 2
            hw = ODD_SPLITS[0]
            p16 = proj_in(h, modtab, odd_w_in[o].astype(BF16), BF16, batch, n_ctx, tn=ODD_PROJ // 2)
            filt_p = [p.astype(F32) for p in (hy_f_w1[o], hy_f_b1[o], hy_f_w2[o], hy_f_b2[o], hy_f_w3[o], hy_f_b3[o], hy_f_freq[o], hy_f_wout[o])]
            kf_lat = hyena_spectrum(hyena_kernels(n_lat, *filt_p))
            kf_ctx = hyena_spectrum_dense(hyena_kernels(n_ctx, *filt_p))
            ya = hyena_mixer(p16, hy_conv_w[o], hy_conv_b[o], kf_lat, kf_ctx, hy_skip[o], hy_norm_w[o], batch, n_ctx)
            yb = retention_mixer(p16, hw // HEAD_W, ret_log_decay[o], ret_norm_w[o], cos_t, sin_t, batch, n_ctx)
            w_out = odd_w_out[o]
        ka = ya.shape[1]
        h1, v = proj_out_ln(ya, yb, w_out[:ka].astype(BF16), w_out[ka:].astype(BF16), h, modtab, ln_w[l, 0], ln_b[l, 0], batch, n_ctx)
        h = moe_ffn(v, router_w[l], router_bias[l], exp_w1[l], exp_w3[l], exp_w2[l], sh_w1[l], sh_w3[l], sh_w2[l],
                    h1, modtab, ln_w[l, 1], ln_b[l, 1], batch, n_ctx)
    return h.reshape(batch, lt, d)[:, n_ctx:]
```

```python
import functools
import math
import jax
import jax.numpy as jnp
from jax import lax
from jax.experimental import pallas as pl
from jax.experimental.pallas import tpu as pltpu

D_MODEL = 1024
BATCH = 8
SEQ = 4096
DEPTH = 4

GRID_W = 64
CTX_LEN = 256
N_EVEN = (DEPTH + 1) // 2
N_ODD = DEPTH // 2

MLSTM_HEADS = 4
MLSTM_DQK = 64
MLSTM_DV = 128
MLSTM_CHUNK = 64
HGRN_HEADS = 4
HGRN_DK = 128
HGRN_DV = 128
HGRN_CHUNK = 32
HYENA_WIDTH = 512
HYENA_GROUPS = 8
HYENA_ORDER = 2
HYENA_EMB = 33
HYENA_FILTER_HIDDEN = 64
HYENA_SHORT = 3
HYENA_TARGET = 1e-2
HYENA_FAST_PCT = 0.3
HYENA_SLOW_PCT = 1.5
RET_HEADS = 4
RET_DQK = 128
RET_DV = 128
RET_CHUNK = 64
ROPE_BASE = 10000.0
N_EXPERTS = 64
TOP_K = 8
EXPERT_HIDDEN = 256
SHARED_HIDDEN = 256
ROUTED_SCALE = 2.5
MOE_BLOCK = 128
DEEPNORM_ALPHA = (2 * DEPTH) ** 0.25
DEEPNORM_BETA = (8 * DEPTH) ** -0.25
EPS = 1e-5
NEG_BIG = -1e30

MLSTM_SPLITS = (MLSTM_HEADS * MLSTM_DQK, MLSTM_HEADS * MLSTM_DQK, MLSTM_HEADS * MLSTM_DV, MLSTM_HEADS * MLSTM_DV, 4 * MLSTM_HEADS)
HGRN_SPLITS = (HGRN_HEADS * HGRN_DK, HGRN_HEADS * HGRN_DV, HGRN_HEADS * HGRN_DK, HGRN_HEADS * HGRN_DK, HGRN_HEADS * HGRN_DV)
EVEN_SPLITS = MLSTM_SPLITS + HGRN_SPLITS
EVEN_PROJ = sum(EVEN_SPLITS)
EVEN_MIX = MLSTM_HEADS * MLSTM_DV + HGRN_HEADS * HGRN_DV
RET_SPLITS = (RET_HEADS * RET_DQK, RET_HEADS * RET_DQK, RET_HEADS * RET_DV, RET_HEADS * RET_DV)
ODD_SPLITS = ((HYENA_ORDER + 1) * HYENA_WIDTH,) + RET_SPLITS
ODD_PROJ = sum(ODD_SPLITS)
ODD_MIX = HYENA_WIDTH + RET_HEADS * RET_DV

F32 = jnp.float32
BF16 = jnp.bfloat16


def _mm_kernel(x_ref, w_ref, o_ref):
    o_ref[...] = jnp.dot(x_ref[...].astype(BF16), w_ref[...].astype(BF16), preferred_element_type=F32)


def pallas_matmul(x, w, tm=512, tn=512):
    t, k = x.shape
    n = w.shape[1]
    n_pad = -(-n // tn) * tn
    if n_pad != n:
        w = jnp.pad(w, ((0, 0), (0, n_pad - n)))
    out = pl.pallas_call(
        _mm_kernel,
        grid=(t // tm, n_pad // tn),
        in_specs=[pl.BlockSpec((tm, k), lambda i, j: (i, 0)), pl.BlockSpec((k, tn), lambda i, j: (0, j))],
        out_specs=pl.BlockSpec((tm, tn), lambda i, j: (i, j)),
        out_shape=jax.ShapeDtypeStruct((t, n_pad), F32),
    )(x, w)
    return out[:, :n]


def hyena_filters(length, fw1, fb1, fw2, fb2, fw3, fb3, freq, wout):
    pos = jnp.arange(length, dtype=F32)
    t = (pos / max(length - 1, 1))[:, None]
    bands = (HYENA_EMB - 1) // 2
    w = 2.0 * math.pi * pos[:, None] / length * jnp.linspace(1e-4, bands - 1, bands)[None, :]
    z = jnp.concatenate([t, jnp.cos(w), jnp.sin(w)], -1)
    hid = jnp.sin(freq * (z @ fw1 + fb1))
    hid = jnp.sin(freq * (hid @ fw2 + fb2))
    hid = jnp.sin(freq * (hid @ fw3 + fb3))
    h = (hid @ wout).reshape(length, HYENA_ORDER, 2, HYENA_WIDTH)
    deltas = jnp.abs(jnp.linspace(math.log(HYENA_TARGET) / HYENA_SLOW_PCT, math.log(HYENA_TARGET) / HYENA_FAST_PCT, HYENA_WIDTH))
    decay = jnp.exp(-t * deltas[None, :])
    return (h * decay[:, None, None, :]).transpose(1, 2, 0, 3)


ROUTE_TB = 512
DISPATCH_TB = 512
COMBINE_TB = 256
FFN_BM = 512


def _dot(a, b):
    return jnp.dot(a, b, preferred_element_type=F32)


def _route_kernel(v_ref, rw_ref, rb_ref, e_ref, rk_ref, g_ref, cnt_ref, carry_ref):
    tb, n_e = v_ref.shape[0], rw_ref.shape[1]

    @pl.when(pl.program_id(0) == 0)
    def _():
        carry_ref[...] = jnp.zeros_like(carry_ref)

    x = v_ref[...]
    w = rw_ref[...]
    xh = x.astype(BF16)
    xl = (x - xh.astype(F32)).astype(BF16)
    wh = w.astype(BF16)
    wl = (w - wh.astype(F32)).astype(BF16)
    logits = _dot(xh, wh) + (_dot(xh, wl) + _dot(xl, wh))
    scores = jax.nn.sigmoid(logits)
    work = scores + rb_ref[...]
    lane = lax.broadcasted_iota(jnp.int32, (tb, n_e), 1)
    col = lax.broadcasted_iota(jnp.int32, (tb, TOP_K), 1)
    e_out = jnp.zeros((tb, TOP_K), jnp.int32)
    g_out = jnp.zeros((tb, TOP_K), F32)
    mask = jnp.zeros((tb, n_e), F32)
    onehots = []
    for j in range(TOP_K):
        m = jnp.max(work, axis=1, keepdims=True)
        idx = jnp.min(jnp.where(work == m, lane, n_e), axis=1, keepdims=True)
        oh = lane == idx
        gj = jnp.sum(jnp.where(oh, scores, 0.0), axis=1, keepdims=True)
        e_out = jnp.where(col == j, idx, e_out)
        g_out = jnp.where(col == j, gj, g_out)
        work = jnp.where(oh, -jnp.inf, work)
        mask = mask + oh.astype(F32)
        onehots.append(oh)
    r_i = lax.broadcasted_iota(jnp.int32, (tb, tb), 0)
    c_i = lax.broadcasted_iota(jnp.int32, (tb, tb), 1)
    tri = (r_i > c_i).astype(BF16)
    rank = _dot(tri, mask.astype(BF16)) + carry_ref[...]
    rk_out = jnp.zeros((tb, TOP_K), jnp.int32)
    for j in range(TOP_K):
        rkj = jnp.sum(jnp.where(onehots[j], rank, 0.0), axis=1, keepdims=True)
        rk_out = jnp.where(col == j, rkj.astype(jnp.int32), rk_out)
    carry_ref[...] = carry_ref[...] + jnp.sum(mask, axis=0, keepdims=True)
    e_ref[...] = e_out
    rk_ref[...] = rk_out
    g_ref[...] = ROUTED_SCALE * g_out / jnp.sum(g_out, axis=1, keepdims=True)
    cnt_ref[...] = carry_ref[...]


def moe_route(v, router_w, router_bias):
    n_tok, d = v.shape
    tb = ROUTE_TB
    return pl.pallas_call(
        _route_kernel,
        grid=(n_tok // tb,),
        in_specs=[pl.BlockSpec((tb, d), lambda i: (i, 0)),
                  pl.BlockSpec((d, N_EXPERTS), lambda i: (0, 0)),
                  pl.BlockSpec((1, N_EXPERTS), lambda i: (0, 0))],
        out_specs=[pl.BlockSpec((tb, TOP_K), lambda i: (i, 0)),
                   pl.BlockSpec((tb, TOP_K), lambda i: (i, 0)),
                   pl.BlockSpec((tb, TOP_K), lambda i: (i, 0)),
                   pl.BlockSpec((1, N_EXPERTS), lambda i: (0, 0))],
        out_shape=[jax.ShapeDtypeStruct((n_tok, TOP_K), jnp.int32),
                   jax.ShapeDtypeStruct((n_tok, TOP_K), jnp.int32),
                   jax.ShapeDtypeStruct((n_tok, TOP_K), F32),
                   jax.ShapeDtypeStruct((1, N_EXPERTS), F32)],
        scratch_shapes=[pltpu.VMEM((1, N_EXPERTS), F32)],
        compiler_params=pltpu.CompilerParams(dimension_semantics=("arbitrary",)),
        name="moe_route",
    )(v, router_w, router_bias.reshape(1, N_EXPERTS))


U32 = jnp.uint32


def _pack_rows(x):
    w = x.shape[1] // 2
    lo = lax.bitcast_convert_type(x[:, :w].astype(BF16).astype(F32), U32)
    hi = lax.bitcast_convert_type(x[:, w:].astype(BF16).astype(F32), U32)
    return hi | (lo >> 16)


def _unpack_rows(u):
    lo = lax.bitcast_convert_type(u << 16, F32)
    hi = lax.bitcast_convert_type(u & jnp.uint32(0xFFFF0000), F32)
    return lo, hi


def _dispatch_kernel(zs_ref, zf_ref, nu_ref, pos_hbm, v_ref, xs_hbm, pos_smem, zbuf, pk_ref, sem_idx, sem_sc, sem_z):
    i = pl.program_id(0)
    tb = v_ref.shape[0]
    bm = zbuf.shape[0]
    n_blocks = xs_hbm.shape[0] // bm
    idx_cp = pltpu.make_async_copy(pos_hbm.at[i], pos_smem, sem_idx)
    idx_cp.start()
    pk_ref[...] = _pack_rows(v_ref[...])

    @pl.when(i == 0)
    def _():
        zbuf[...] = jnp.zeros_like(zbuf)
        for e in range(N_EXPERTS):
            @pl.when(zf_ref[e] != 0)
            def _():
                start = pl.multiple_of(zs_ref[e], bm)
                pltpu.make_async_copy(zbuf, xs_hbm.at[pl.ds(start, bm)], sem_z).start()
        for e in range(N_EXPERTS):
            @pl.when(zf_ref[e] != 0)
            def _():
                pltpu.make_async_copy(zbuf, xs_hbm.at[pl.ds(0, bm)], sem_z).wait()

        def tail(b, carry):
            cp = pltpu.make_async_copy(zbuf, xs_hbm.at[pl.ds(pl.multiple_of(b * bm, bm), bm)], sem_z)
            cp.start()
            cp.wait()
            return carry

        lax.fori_loop(nu_ref[0], n_blocks, tail, 0)

    idx_cp.wait()

    def body(t8, carry):
        row0 = pl.multiple_of(t8 * 8, 8)
        for r in range(8):
            for k in range(TOP_K):
                p = pos_smem[0, (row0 + r) * TOP_K + k]
                pltpu.make_async_copy(pk_ref.at[pl.ds(row0 + r, 1)], xs_hbm.at[pl.ds(p, 1)], sem_sc).start()
        return carry

    lax.fori_loop(0, tb // 8, body, 0)
    for k in range(TOP_K):
        pltpu.make_async_copy(pk_ref, pk_ref, sem_sc).wait()


def moe_dispatch(v, pos3, zero_start, zero_flag, n_used, n_rows):
    n_tok, d = v.shape
    tb = DISPATCH_TB
    dp = d // 2
    grid_spec = pltpu.PrefetchScalarGridSpec(
        num_scalar_prefetch=3,
        grid=(n_tok // tb,),
        in_specs=[pl.BlockSpec(memory_space=pl.ANY),
                  pl.BlockSpec((tb, d), lambda i, zs, zf, nu: (i, 0))],
        out_specs=pl.BlockSpec(memory_space=pl.ANY),
        scratch_shapes=[pltpu.SMEM((1, tb * TOP_K), jnp.int32),
                        pltpu.VMEM((FFN_BM, dp), U32),
                        pltpu.VMEM((tb, dp), U32),
                        pltpu.SemaphoreType.DMA(()),
                        pltpu.SemaphoreType.DMA(()),
                        pltpu.SemaphoreType.DMA(())],
    )
    return pl.pallas_call(
        _dispatch_kernel,
        grid_spec=grid_spec,
        out_shape=jax.ShapeDtypeStruct((n_rows, dp), U32),
        compiler_params=pltpu.CompilerParams(dimension_semantics=("arbitrary",)),
        name="moe_dispatch",
    )(zero_start, zero_flag, n_used, pos3, v)


def _ffn_kernel(be_ref, nu_ref, xs_ref, w1_ref, w3_ref, w2_ref, y_ref):
    used = pl.program_id(0) < nu_ref[0]

    @pl.when(used)
    def _():
        dp = xs_ref.shape[1]
        lo, hi = _unpack_rows(xs_ref[...])
        lo, hi = lo.astype(BF16), hi.astype(BF16)
        h1 = _dot(lo, w1_ref[0, :dp, :]) + _dot(hi, w1_ref[0, dp:, :])
        h3 = _dot(lo, w3_ref[0, :dp, :]) + _dot(hi, w3_ref[0, dp:, :])
        h = (h1 * jax.nn.sigmoid(h1) * h3).astype(BF16)
        y_ref[...] = _pack_rows(_dot(h, w2_ref[0]))

    @pl.when(jnp.logical_not(used))
    def _():
        y_ref[...] = jnp.zeros_like(y_ref)


def moe_expert_ffn(xs, block_e, n_used, w1, w3, w2):
    n_rows, dp = xs.shape
    d = 2 * dp
    bm = FFN_BM
    hid = w1.shape[2]

    def row_map(i, be, nu):
        return (jnp.minimum(i, nu[0] - 1), 0)

    def w_map(i, be, nu):
        return (be[jnp.minimum(i, nu[0] - 1)], 0, 0)

    grid_spec = pltpu.PrefetchScalarGridSpec(
        num_scalar_prefetch=2,
        grid=(n_rows // bm,),
        in_specs=[pl.BlockSpec((bm, dp), row_map),
                  pl.BlockSpec((1, d, hid), w_map),
                  pl.BlockSpec((1, d, hid), w_map),
                  pl.BlockSpec((1, hid, d), w_map)],
        out_specs=pl.BlockSpec((bm, dp), lambda i, be, nu: (i, 0)),
    )
    return pl.pallas_call(
        _ffn_kernel,
        grid_spec=grid_spec,
        out_shape=jax.ShapeDtypeStruct((n_rows, dp), U32),
        compiler_params=pltpu.CompilerParams(dimension_semantics=("arbitrary",)),
        name="moe_expert_ffn",
    )(block_e, n_used, xs, w1, w3, w2)


def _combine_kernel(pos_hbm, y_hbm, v_ref, g_ref, sw1_ref, sw3_ref, sw2_ref, h1_ref, mod_ref, lnw_ref, lnb_ref, o_ref,
                    pos_smem, ybuf, sem_idx, sem_y, *, n_ctx, blocks_per_batch):
    i = pl.program_id(0)
    n = pl.num_programs(0)
    tb = v_ref.shape[0]
    dp = ybuf.shape[1]
    slot = lax.rem(i, 2)

    def fetch_idx(step, s):
        return pltpu.make_async_copy(pos_hbm.at[step], pos_smem.at[s], sem_idx.at[s])

    def issue_gather(s):
        base = s * (TOP_K * tb)

        def body(t8, carry):
            row0 = pl.multiple_of(t8 * 8, 8)
            for r in range(8):
                for k in range(TOP_K):
                    p = pos_smem[s, 0, (row0 + r) * TOP_K + k]
                    pltpu.make_async_copy(y_hbm.at[pl.ds(p, 1)], ybuf.at[pl.ds(base + k * tb + row0 + r, 1)], sem_y.at[s]).start()
            return carry
        lax.fori_loop(0, tb // 8, body, 0)

    @pl.when(i == 0)
    def _():
        cp = fetch_idx(0, 0)
        cp.start()
        cp.wait()
        issue_gather(0)

    for s in range(2):
        @pl.when((i + 1 < n) & (slot == 1 - s))
        def _():
            cp = fetch_idx(i + 1, s)
            cp.start()
            cp.wait()
            issue_gather(s)

    x = v_ref[...].astype(BF16)
    h1 = _dot(x, sw1_ref[...])
    h3 = _dot(x, sw3_ref[...])
    shared = _dot((h1 * jax.nn.sigmoid(h1) * h3).astype(BF16), sw2_ref[...])
    base = pl.multiple_of(slot * (TOP_K * tb), TOP_K * tb)
    for k in range(TOP_K):
        rows = ybuf.at[pl.ds(base + k * tb, tb)]
        pltpu.make_async_copy(rows, rows, sem_y.at[slot]).wait()
    g = g_ref[...]
    acc_lo = shared[:, :dp]
    acc_hi = shared[:, dp:]
    for k in range(TOP_K):
        lo, hi = _unpack_rows(ybuf[pl.ds(base + k * tb, tb), :])
        acc_lo = acc_lo + g[:, k:k + 1] * lo
        acc_hi = acc_hi + g[:, k:k + 1] * hi
    f = jnp.concatenate([acc_lo, acc_hi], axis=1)
    is_ctx = _is_ctx_rows(tb, blocks_per_batch, n_ctx)
    z = DEEPNORM_ALPHA * h1_ref[...] + _row_mod(mod_ref, 5, is_ctx) * f
    o_ref[...] = _layer_norm_rows(z, lnw_ref[...], lnb_ref[...])


def moe_combine(pos3, y, v, gate, sw1, sw3, sw2, h1, modtab, ln_w, ln_b, batch, n_ctx):
    n_tok, d = v.shape
    tb = COMBINE_TB
    hid = sw1.shape[1]
    dp = y.shape[1]
    bpb = n_tok // batch // tb
    row = lambda i: (i, 0)
    fixed = lambda i: (0, 0)
    return pl.pallas_call(
        functools.partial(_combine_kernel, n_ctx=n_ctx, blocks_per_batch=bpb),
        grid=(n_tok // tb,),
        in_specs=[pl.BlockSpec(memory_space=pl.ANY),
                  pl.BlockSpec(memory_space=pl.ANY),
                  pl.BlockSpec((tb, d), row),
                  pl.BlockSpec((tb, TOP_K), row),
                  pl.BlockSpec((d, hid), fixed),
                  pl.BlockSpec((d, hid), fixed),
                  pl.BlockSpec((hid, d), fixed),
                  pl.BlockSpec((tb, d), row),
                  pl.BlockSpec((1, 2, 6, d), lambda i: (i // bpb, 0, 0, 0)),
                  pl.BlockSpec((1, d), fixed), pl.BlockSpec((1, d), fixed)],
        out_specs=pl.BlockSpec((tb, d), row),
        out_shape=jax.ShapeDtypeStruct((n_tok, d), F32),
        scratch_shapes=[pltpu.SMEM((2, 1, tb * TOP_K), jnp.int32),
                        pltpu.VMEM((2 * TOP_K * tb, dp), U32),
                        pltpu.SemaphoreType.DMA((2,)),
                        pltpu.SemaphoreType.DMA((2,))],
        compiler_params=pltpu.CompilerParams(dimension_semantics=("arbitrary",), vmem_limit_bytes=48 * 1024 * 1024),
        name="moe_combine",
    )(pos3, y, v, gate, sw1, sw3, sw2, h1, modtab, ln_w.reshape(1, d).astype(F32), ln_b.reshape(1, d).astype(F32))


def moe_ffn(t, router_w, router_bias, w1, w3, w2, sw1, sw3, sw2, h1, modtab, ln_w, ln_b, batch, n_ctx):
    n_tok, d = t.shape
    tf = t.astype(F32)
    bm = FFN_BM
    e_sel, rank, gate, cnt = moe_route(tf, router_w.astype(F32), router_bias.astype(F32))
    counts = cnt[0].astype(jnp.int32)
    padded = (counts + bm - 1) // bm * bm
    pend = jnp.cumsum(padded)
    pstart = pend - padded
    n_blocks = -(-(n_tok * TOP_K + N_EXPERTS * (bm - 1)) // bm)
    expert_ids = jnp.arange(N_EXPERTS, dtype=jnp.int32)
    pos = rank + jnp.sum(jnp.where(e_sel[..., None] == expert_ids, pstart, 0), -1)
    block_start = jnp.arange(n_blocks, dtype=jnp.int32) * bm
    block_e = jnp.minimum(jnp.sum((pend[None, :] <= block_start[:, None]).astype(jnp.int32), axis=1), N_EXPERTS - 1)
    n_used = (pend[-1:] // bm).astype(jnp.int32)
    zero_start = jnp.maximum(pend - bm, 0).astype(jnp.int32)
    zero_flag = (counts > 0).astype(jnp.int32)
    xs = moe_dispatch(tf, pos.reshape(n_tok // DISPATCH_TB, 1, DISPATCH_TB * TOP_K), zero_start, zero_flag, n_used, n_blocks * bm)
    y = moe_expert_ffn(xs, block_e, n_used, w1.astype(BF16), w3.astype(BF16), w2.astype(BF16))
    return moe_combine(pos.reshape(n_tok // COMBINE_TB, 1, COMBINE_TB * TOP_K), y, tf, gate,
                       sw1.astype(BF16), sw3.astype(BF16), sw2.astype(BF16), h1, modtab, ln_w, ln_b, batch, n_ctx)


HEAD_W = 128
MIX_VMEM = 48 * 1024 * 1024
NORM_ROWS = 256


def _dot_nt(a, b):
    return lax.dot_general(a, b, (((1,), (1,)), ((), ())), preferred_element_type=F32)


def _dot_tn(a, b):
    return lax.dot_general(a, b, (((0,), (0,)), ((), ())), preferred_element_type=F32)


def _dot_hi(a, b):
    bh = b.astype(BF16)
    bl = (b - bh.astype(F32)).astype(BF16)
    a16 = a.astype(BF16)
    return _dot(a16, bh) + _dot(a16, bl)


def _tri(n, upper):
    r = lax.broadcasted_iota(jnp.int32, (n, n), 0)
    c = lax.broadcasted_iota(jnp.int32, (n, n), 1)
    return (r <= c) if upper else (r >= c)


def _bidir_loop(n_ctx_chunks, n_chunks, step, unroll):
    def body(c, carry):
        cb = jnp.where(c < n_ctx_chunks, n_ctx_chunks - 1 - c, n_chunks - 1 + n_ctx_chunks - c)
        live, stores = [step(c, False), step(cb, True)], []
        while live:
            waiting = []
            for gen in live:
                if isinstance(gen, list):
                    stores += gen
                    continue
                try:
                    next(gen)
                    waiting.append(gen)
                except StopIteration as done:
                    stores += done.value
            live = waiting
        for ref, idx, val in stores:
            ref[idx] = val
        return carry
    lax.fori_loop(0, n_chunks, body, 0, unroll=unroll)


def _norm_gate_epilogue(hf_ref, hb_ref, gate_ref, nw_ref, y_ref, n_rows, rms, gate_fn):
    def body(i, carry):
        rows = pl.ds(pl.multiple_of(i * NORM_ROWS, NORM_ROWS), NORM_ROWS)
        h = hf_ref[rows, :] + hb_ref[rows, :]
        if not rms:
            h = h - jnp.mean(h, axis=1, keepdims=True)
        h = h * lax.rsqrt(jnp.mean(h * h, axis=1, keepdims=True) + EPS)
        y_ref[rows, :] = (h * nw_ref[...] * gate_fn(gate_ref[rows, :].astype(F32))).astype(y_ref.dtype)
        return carry
    lax.fori_loop(0, n_rows // NORM_ROWS, body, 0)


def _silu(x):
    return x * jax.nn.sigmoid(x)


def _retention_kernel(ld_ref, q_ref, k_ref, v_ref, g_ref, cos_ref, sin_ref, perm_ref, nw_ref, y_ref,
                      qr_ref, kr_ref, hf_ref, hb_ref, sf_ref, sb_ref, *, n_ctx):
    cs = RET_CHUNK
    lt = q_ref.shape[0]
    hd = pl.program_id(1)

    def rope(i, carry):
        rows = pl.ds(pl.multiple_of(i * NORM_ROWS, NORM_ROWS), NORM_ROWS)
        cs_, sn_ = cos_ref[rows, :], sin_ref[rows, :]
        q = q_ref[rows, :]
        k = k_ref[rows, :]
        qr_ref[rows, :] = (q.astype(F32) * cs_ + _dot(q, perm_ref[...]) * sn_).astype(BF16)
        kr_ref[rows, :] = ((k.astype(F32) * cs_ + _dot(k, perm_ref[...]) * sn_) * RET_DQK ** -0.5).astype(BF16)
        return carry
    lax.fori_loop(0, lt // NORM_ROWS, rope, 0)

    pos_r = lax.broadcasted_iota(jnp.int32, (cs, cs), 0).astype(F32)
    pos_c = lax.broadcasted_iota(jnp.int32, (cs, cs), 1).astype(F32)
    pos_t = lax.broadcasted_iota(jnp.int32, (cs, HEAD_W), 0).astype(F32)
    consts = []
    for d in range(2):
        lg = ld_ref[d, hd]
        if d == 0:
            decay = jnp.where(pos_r >= pos_c, jnp.exp((pos_r - pos_c) * lg), 0.0)
            q_dec = jnp.exp((pos_t + 1.0) * lg)
            k_dec = jnp.exp((cs - 1.0 - pos_t) * lg)
        else:
            decay = jnp.where(pos_r <= pos_c, jnp.exp((pos_c - pos_r) * lg), 0.0)
            q_dec = jnp.exp((cs - pos_t) * lg)
            k_dec = jnp.exp(pos_t * lg)
        consts.append((decay, q_dec, k_dec, jnp.exp(cs * lg)))
    sf_ref[...] = jnp.zeros_like(sf_ref)
    sb_ref[...] = jnp.zeros_like(sb_ref)

    def step(chunk, reverse):
        decay, q_dec, k_dec, c_dec = consts[1 if reverse else 0]
        s_ref = sb_ref if reverse else sf_ref
        h_ref = hb_ref if reverse else hf_ref
        rows = pl.ds(pl.multiple_of(chunk * cs, cs), cs)
        q = qr_ref[rows, :]
        k = kr_ref[rows, :]
        v = v_ref[rows, :]
        sc = _dot_nt(q, k) * decay
        s_in = s_ref[...]
        inter = _dot((q.astype(F32) * q_dec).astype(BF16), s_in.astype(BF16))
        return [(h_ref, (rows, slice(None)), inter + _dot(sc.astype(BF16), v)),
                (s_ref, (slice(None), slice(None)), c_dec * s_in + _dot_tn((k.astype(F32) * k_dec).astype(BF16), v))]

    _bidir_loop(n_ctx // cs, lt // cs, step, 8)
    _norm_gate_epilogue(hf_ref, hb_ref, g_ref, nw_ref, y_ref, lt, False, _silu)


def retention_mixer(p16, col0, log_decay, norm_w, cos_t, sin_t, batch, n_ctx):
    lt = p16.shape[0] // batch
    nh = RET_HEADS
    half = RET_DQK // 2
    src = jnp.arange(HEAD_W)
    swapped = jnp.where((src % half) < half // 2, src + half // 2, src - half // 2)
    perm = (src[:, None] == swapped[None, :]).astype(BF16)

    def blk(off):
        return pl.BlockSpec((lt, HEAD_W), lambda b, h, ld: (b, off + h))

    grid_spec = pltpu.PrefetchScalarGridSpec(
        num_scalar_prefetch=1,
        grid=(batch, nh),
        in_specs=[blk(col0), blk(col0 + nh), blk(col0 + 2 * nh), blk(col0 + 3 * nh),
                  pl.BlockSpec((lt, HEAD_W), lambda b, h, ld: (0, 0)),
                  pl.BlockSpec((lt, HEAD_W), lambda b, h, ld: (0, 0)),
                  pl.BlockSpec((HEAD_W, HEAD_W), lambda b, h, ld: (0, 0)),
                  pl.BlockSpec((1, HEAD_W), lambda b, h, ld: (0, h))],
        out_specs=pl.BlockSpec((lt, HEAD_W), lambda b, h, ld: (b, h)),
        scratch_shapes=[pltpu.VMEM((lt, HEAD_W), BF16), pltpu.VMEM((lt, HEAD_W), BF16),
                        pltpu.VMEM((lt, HEAD_W), F32), pltpu.VMEM((lt, HEAD_W), F32),
                        pltpu.VMEM((RET_DQK, RET_DV), F32), pltpu.VMEM((RET_DQK, RET_DV), F32)],
    )
    return pl.pallas_call(
        functools.partial(_retention_kernel, n_ctx=n_ctx),
        grid_spec=grid_spec,
        out_shape=jax.ShapeDtypeStruct((batch * lt, nh * HEAD_W), BF16),
        compiler_params=pltpu.CompilerParams(dimension_semantics=("arbitrary", "arbitrary"), vmem_limit_bytes=MIX_VMEM),
        name="retention_mixer",
    )(log_decay.astype(F32), p16, p16, p16, p16, cos_t, sin_t, perm, norm_w.reshape(1, -1).astype(F32))


def _hgrn_kernel(q_ref, i_ref, g_ref, ff_ref, fb_ref, lb_ref, nw_ref, y_ref, hf_ref, hb_ref, sf_ref, sb_ref, *, n_ctx):
    cs = HGRN_CHUNK
    lt = q_ref.shape[0]
    lb = lb_ref[...]
    sf_ref[...] = jnp.zeros_like(sf_ref)
    sb_ref[...] = jnp.zeros_like(sb_ref)
    tri = (_tri(cs, False).astype(F32), _tri(cs, True).astype(F32))
    mask = (_tri(cs, False), _tri(cs, True))

    def step(chunk, reverse):
        d = 1 if reverse else 0
        f_ref = fb_ref if reverse else ff_ref
        s_ref = sb_ref if reverse else sf_ref
        h_ref = hb_ref if reverse else hf_ref
        rows = pl.ds(pl.multiple_of(chunk * cs, cs), cs)
        f = lb + (1.0 - lb) * jax.nn.sigmoid(f_ref[rows, :])
        kk = 1.0 - f
        a_cum = _dot_hi(tri[d], jnp.log(f))
        q = _silu(q_ref[rows, :].astype(F32))
        v = i_ref[rows, :]
        s_in = s_ref[...]
        yield
        a_end = a_cum[0:1, :] if reverse else a_cum[cs - 1:cs, :]
        mid = cs // 2 if reverse else cs // 2 - 1
        a_mid = a_cum[mid:mid + 1, :]
        inter = _dot_nt((q * jnp.exp(a_cum)).astype(BF16), s_in.astype(BF16))
        sc = _dot_nt((q * jnp.exp(a_cum - a_mid)).astype(BF16), (kk * jnp.exp(a_mid - a_cum)).astype(BF16))
        yield
        sc = jnp.where(mask[d], sc, 0.0)
        return [(h_ref, (rows, slice(None)), inter + _dot(sc.astype(BF16), v)),
                (s_ref, (slice(None), slice(None)), s_in * jnp.exp(a_end) + _dot_tn(v, (kk * jnp.exp(a_end - a_cum)).astype(BF16)))]

    _bidir_loop(n_ctx // cs, lt // cs, step, 4)
    _norm_gate_epilogue(hf_ref, hb_ref, g_ref, nw_ref, y_ref, lt, True, _silu)


EVEN16_MLSTM_Q, EVEN16_MLSTM_K, EVEN16_MLSTM_V, EVEN16_MLSTM_O = 0, 2, 4, 8
EVEN16_HGRN_Q, EVEN16_HGRN_I, EVEN16_HGRN_G = 12, 16, 20
EVEN32_GATES, EVEN32_FF, EVEN32_FB = 0, 1, 5


def hgrn_mixer(p16, p32, lb, norm_w, batch, n_ctx):
    lt = p16.shape[0] // batch
    nh = HGRN_HEADS

    def blk(off):
        return pl.BlockSpec((lt, HEAD_W), lambda b, h: (b, off + h))

    def vec():
        return pl.BlockSpec((1, HEAD_W), lambda b, h: (0, h))

    return pl.pallas_call(
        functools.partial(_hgrn_kernel, n_ctx=n_ctx),
        grid=(batch, nh),
        in_specs=[blk(EVEN16_HGRN_Q), blk(EVEN16_HGRN_I), blk(EVEN16_HGRN_G), blk(EVEN32_FF), blk(EVEN32_FB), vec(), vec()],
        out_specs=pl.BlockSpec((lt, HEAD_W), lambda b, h: (b, h)),
        out_shape=jax.ShapeDtypeStruct((batch * lt, nh * HEAD_W), BF16),
        scratch_shapes=[pltpu.VMEM((lt, HEAD_W), F32), pltpu.VMEM((lt, HEAD_W), F32),
                        pltpu.VMEM((HGRN_DV, HGRN_DK), F32), pltpu.VMEM((HGRN_DV, HGRN_DK), F32)],
        compiler_params=pltpu.CompilerParams(dimension_semantics=("arbitrary", "arbitrary"), vmem_limit_bytes=MIX_VMEM),
        name="hgrn_mixer",
    )(p16, p16, p16, p32, p32, lb.reshape(1, -1).astype(F32), norm_w.reshape(1, -1).astype(F32))


def _mlstm_kernel(q_ref, k_ref, v_ref, o_ref, gt_ref, gb_ref, nw_ref, y_ref, hf_ref, hb_ref, cf_ref, cb_ref, mf_ref, mb_ref, *, n_ctx):
    cs = MLSTM_CHUNK
    lt = q_ref.shape[0]
    hd = pl.program_id(1)
    lane = lax.broadcasted_iota(jnp.int32, (1, HEAD_W), 1)
    head_mask = ((lane // MLSTM_DQK) == (hd % 2)).astype(F32)
    gcol = lax.broadcasted_iota(jnp.int32, (1, HEAD_W), 1)
    tri = (_tri(cs, False).astype(F32), _tri(cs, True).astype(F32))
    mask = (_tri(cs, False), _tri(cs, True))
    ones_v = jnp.ones((cs, HEAD_W), BF16)
    for ref in (cf_ref, cb_ref):
        ref[...] = jnp.zeros_like(ref)
    for ref in (mf_ref, mb_ref):
        ref[...] = jnp.full_like(ref, NEG_BIG)

    def pick(tile, col):
        return jnp.sum(jnp.where(gcol == col, tile, 0.0), axis=1, keepdims=True)

    def step(chunk, reverse):
        d = 1 if reverse else 0
        c_ref = cb_ref if reverse else cf_ref
        m_ref = mb_ref if reverse else mf_ref
        h_ref = hb_ref if reverse else hf_ref
        rows = pl.ds(pl.multiple_of(chunk * cs, cs), cs)
        x = gt_ref[rows, :] + gb_ref[...]
        ci = hd + (2 * MLSTM_HEADS if reverse else 0)
        ig = pick(x, ci)
        lf = pick(jax.nn.log_sigmoid(x), ci + MLSTM_HEADS)
        lf_t = jnp.broadcast_to(lf, (cs, HEAD_W))
        b = _dot_hi(tri[d], lf_t)
        q = q_ref[rows, :]
        k = k_ref[rows, :]
        v = v_ref[rows, :]
        qm = (q.astype(F32) * (head_mask * MLSTM_DQK ** -0.5)).astype(BF16)
        s_raw = _dot_nt(qm, k)
        c_in = c_ref[...]
        r_state = _dot(qm, c_in.astype(BF16))
        yield
        g = b[0:1, :] if reverse else b[cs - 1:cs, :]
        ig_t = jnp.broadcast_to(ig, (cs, HEAD_W))
        a = g - b + ig_t
        m_loc = jnp.max(a, axis=0, keepdims=True)
        w = jnp.exp(a - m_loc)
        m_in = m_ref[...]
        r_row = jnp.transpose(ig_t - b)[0:1, 0:cs]
        log_d = jnp.where(mask[d], b[:, 0:cs] + r_row, -jnp.inf)
        inter = b + m_in
        m_out = jnp.maximum(inter, jnp.max(log_d, axis=1, keepdims=True))
        s = s_raw * jnp.exp(log_d - m_out[:, 0:cs])
        e_int = jnp.exp(inter - m_out)
        vo = jnp.concatenate([v, ones_v], axis=1)
        kw = (k.astype(F32) * head_mask * w).astype(BF16)
        r_intra = _dot(s.astype(BF16), vo)
        c_loc = _dot_tn(kw, vo)
        yield
        r = r_intra + jnp.concatenate([e_int, e_int], axis=1) * r_state
        num, den = r[:, :HEAD_W], r[:, HEAD_W:]
        h_out = num / jnp.maximum(jnp.abs(den), jnp.exp(-m_out))
        m_new = jnp.maximum(g + m_in, m_loc)
        sp = jnp.exp(g + m_in - m_new)
        sl = jnp.exp(m_loc - m_new)
        c_new = jnp.concatenate([sp, sp], axis=1) * c_in + jnp.concatenate([sl, sl], axis=1) * c_loc
        full = (slice(None), slice(None))
        return [(h_ref, (rows, slice(None)), h_out), (c_ref, full, c_new), (m_ref, full, m_new)]

    _bidir_loop(n_ctx // cs, lt // cs, step, 4)
    _norm_gate_epilogue(hf_ref, hb_ref, o_ref, nw_ref, y_ref, lt, False, jax.nn.sigmoid)


def mlstm_mixer(p16, p32, gate_b, norm_w, batch, n_ctx):
    lt = p16.shape[0] // batch
    nh = MLSTM_HEADS
    gb = jnp.zeros((1, HEAD_W), F32).at[0, :4 * nh].set(gate_b.astype(F32).reshape(-1))

    return pl.pallas_call(
        functools.partial(_mlstm_kernel, n_ctx=n_ctx),
        grid=(batch, nh),
        in_specs=[pl.BlockSpec((lt, HEAD_W), lambda b, h: (b, EVEN16_MLSTM_Q + h // 2)),
                  pl.BlockSpec((lt, HEAD_W), lambda b, h: (b, EVEN16_MLSTM_K + h // 2)),
                  pl.BlockSpec((lt, HEAD_W), lambda b, h: (b, EVEN16_MLSTM_V + h)),
                  pl.BlockSpec((lt, HEAD_W), lambda b, h: (b, EVEN16_MLSTM_O + h)),
                  pl.BlockSpec((lt, HEAD_W), lambda b, h: (b, EVEN32_GATES)),
                  pl.BlockSpec((1, HEAD_W), lambda b, h: (0, 0)),
                  pl.BlockSpec((1, HEAD_W), lambda b, h: (0, h))],
        out_specs=pl.BlockSpec((lt, HEAD_W), lambda b, h: (b, h)),
        out_shape=jax.ShapeDtypeStruct((batch * lt, nh * HEAD_W), BF16),
        scratch_shapes=[pltpu.VMEM((lt, HEAD_W), F32), pltpu.VMEM((lt, HEAD_W), F32),
                        pltpu.VMEM((HEAD_W, 2 * HEAD_W), F32), pltpu.VMEM((HEAD_W, 2 * HEAD_W), F32),
                        pltpu.VMEM((1, HEAD_W), F32), pltpu.VMEM((1, HEAD_W), F32)],
        compiler_params=pltpu.CompilerParams(dimension_semantics=("arbitrary", "arbitrary"), vmem_limit_bytes=MIX_VMEM),
        name="mlstm_mixer",
    )(p16, p16, p16, p16, p32, gb, norm_w.reshape(1, -1).astype(F32))


ROW_BLOCKS_PER_BATCH_IN = 4
ROW_BLOCKS_PER_BATCH_OUT = 8
PROJ_TN = 512


def _row_mod(mod_ref, idx, is_ctx):
    return jnp.where(is_ctx, mod_ref[0, 0, idx:idx + 1, :], mod_ref[0, 1, idx:idx + 1, :])


def _is_ctx_rows(tm, blocks_per_batch, n_ctx):
    row = (pl.program_id(0) % blocks_per_batch) * tm + lax.broadcasted_iota(jnp.int32, (tm, 1), 0)
    return row < n_ctx


def _proj_in_kernel(h_ref, mod_ref, w_ref, o_ref, u_ref, *, n_ctx, blocks_per_batch):
    tm = h_ref.shape[0]

    @pl.when(pl.program_id(1) == 0)
    def _():
        is_ctx = _is_ctx_rows(tm, blocks_per_batch, n_ctx)
        u = h_ref[...] * (1.0 + _row_mod(mod_ref, 1, is_ctx)) + _row_mod(mod_ref, 0, is_ctx)
        u_ref[...] = u.astype(BF16)

    o_ref[...] = _dot(u_ref[...], w_ref[...]).astype(o_ref.dtype)


def proj_in(h, modtab, w, out_dtype, batch, n_ctx, tn=PROJ_TN):
    t, d = h.shape
    n = w.shape[1]
    bpb = ROW_BLOCKS_PER_BATCH_IN
    tm = t // batch // bpb
    return pl.pallas_call(
        functools.partial(_proj_in_kernel, n_ctx=n_ctx, blocks_per_batch=bpb),
        grid=(t // tm, n // tn),
        in_specs=[pl.BlockSpec((tm, d), lambda i, j: (i, 0)),
                  pl.BlockSpec((1, 2, 6, d), lambda i, j: (i // bpb, 0, 0, 0)),
                  pl.BlockSpec((d, tn), lambda i, j: (0, j))],
        out_specs=pl.BlockSpec((tm, tn), lambda i, j: (i, j)),
        out_shape=jax.ShapeDtypeStruct((t, n), out_dtype),
        scratch_shapes=[pltpu.VMEM((tm, d), BF16)],
        compiler_params=pltpu.CompilerParams(dimension_semantics=("arbitrary", "arbitrary"), vmem_limit_bytes=MIX_VMEM),
        name="proj_in",
    )(h, modtab, w)


def _layer_norm_rows(z, w, b):
    mu = jnp.mean(z, axis=1, keepdims=True)
    zc = z - mu
    var = jnp.mean(zc * zc, axis=1, keepdims=True)
    return zc * lax.rsqrt(var + EPS) * w + b


def _proj_out_kernel(ya_ref, yb_ref, wa_ref, wb_ref, h_ref, mod_ref, lnw_ref, lnb_ref, h1_ref, v_ref, *, n_ctx, blocks_per_batch):
    tm = h_ref.shape[0]
    is_ctx = _is_ctx_rows(tm, blocks_per_batch, n_ctx)
    y = _dot(ya_ref[...], wa_ref[...]) + _dot(yb_ref[...], wb_ref[...])
    z = DEEPNORM_ALPHA * h_ref[...] + _row_mod(mod_ref, 2, is_ctx) * y
    h1 = _layer_norm_rows(z, lnw_ref[...], lnb_ref[...])
    h1_ref[...] = h1
    v_ref[...] = h1 * (1.0 + _row_mod(mod_ref, 4, is_ctx)) + _row_mod(mod_ref, 3, is_ctx)


def proj_out_ln(ya, yb, wa, wb, h, modtab, ln_w, ln_b, batch, n_ctx):
    t, d = h.shape
    bpb = ROW_BLOCKS_PER_BATCH_OUT
    tm = t // batch // bpb
    ka, kb = ya.shape[1], yb.shape[1]
    row = lambda i: (i, 0)
    fixed = lambda i: (0, 0)
    return pl.pallas_call(
        functools.partial(_proj_out_kernel, n_ctx=n_ctx, blocks_per_batch=bpb),
        grid=(t // tm,),
        in_specs=[pl.BlockSpec((tm, ka), row), pl.BlockSpec((tm, kb), row),
                  pl.BlockSpec((ka, d), fixed), pl.BlockSpec((kb, d), fixed),
                  pl.BlockSpec((tm, d), row),
                  pl.BlockSpec((1, 2, 6, d), lambda i: (i // bpb, 0, 0, 0)),
                  pl.BlockSpec((1, d), fixed), pl.BlockSpec((1, d), fixed)],
        out_specs=[pl.BlockSpec((tm, d), row), pl.BlockSpec((tm, d), row)],
        out_shape=[jax.ShapeDtypeStruct((t, d), F32), jax.ShapeDtypeStruct((t, d), F32)],
        compiler_params=pltpu.CompilerParams(dimension_semantics=("arbitrary",), vmem_limit_bytes=MIX_VMEM),
        name="proj_out_ln",
    )(ya, yb, wa, wb, h, modtab, ln_w.reshape(1, d).astype(F32), ln_b.reshape(1, d).astype(F32))


FFT_N2 = 128
HY_LANES = 128
HY_VMEM = 56 * 1024 * 1024


def _np_cplx_tables(n_len):
    import numpy as np
    n = 2 * n_len
    n2 = FFT_N2
    n1 = n // n2
    w = lambda m, e: np.exp(-2j * np.pi * (e % m) / m)
    k1 = np.arange(n1)
    a = np.arange(n1)
    j2 = np.arange(n2)
    ta = w(n, j2[:, None, None] * k1[None, :, None]) * w(n1, k1[None, :, None] * a[None, None, :])
    ta_stack = np.concatenate([ta.real, ta.imag], axis=1)
    tinv = np.conj(ta).transpose(0, 2, 1) / n
    tinv_stack = np.concatenate([tinv.real, -tinv.imag], axis=2)[:, :n1 // 2]
    f2 = w(n2, j2[:, None] * j2[None, :])
    sb = np.block([[f2.real, -f2.imag], [f2.imag, f2.real]])
    sbi = np.block([[f2.real, f2.imag], [-f2.imag, f2.real]])
    return ta_stack, tinv_stack, sb, sbi


def _np_dense_dft(n_len):
    import numpy as np
    n = 2 * n_len
    k = np.arange(n)
    f = np.exp(-2j * np.pi * ((k[:, None] * k[None, :]) % n) / n)
    fwd = np.concatenate([f.real, f.imag], axis=0)
    g = np.conj(f)[:n_len] / n
    inv = np.concatenate([g.real, -g.imag], axis=1)
    return fwd, inv


def _fft_stage_a(load_rows, ta_ref, b_ref, n1_in, n1):
    def body(j, carry):
        x = load_rows(j, n1_in).astype(BF16)
        a = _dot(ta_ref[j], x)
        b_ref[0, pl.ds(j, n1, stride=FFT_N2), :] = a[:n1]
        b_ref[1, pl.ds(j, n1, stride=FFT_N2), :] = a[n1:]
        return carry
    lax.fori_loop(0, FFT_N2, body, 0, unroll=16)


def _fft_stage_b_rows(b_ref, sb_ref, k1):
    rows = pl.ds(pl.multiple_of(k1 * FFT_N2, FFT_N2), FFT_N2)
    rhs = jnp.concatenate([b_ref[0, rows, :], b_ref[1, rows, :]], axis=0).astype(BF16)
    return rows, _dot(sb_ref[...], rhs)


def _hyena_spectrum_kernel(k_ref, ta_ref, sb_ref, o_ref, b_ref):
    n1 = b_ref.shape[1] // FFT_N2
    _fft_stage_a(lambda j, cnt: k_ref[0, pl.ds(j, cnt, stride=FFT_N2), :], ta_ref, b_ref, n1, n1)

    def body(k1, carry):
        rows, x = _fft_stage_b_rows(b_ref, sb_ref, k1)
        o_ref[0, 0, rows, :] = x[:FFT_N2]
        o_ref[0, 1, rows, :] = x[FFT_N2:]
        return carry
    lax.fori_loop(0, n1, body, 0)


def hyena_spectrum(kern):
    n_ord, n, width = kern.shape
    n1 = n // FFT_N2
    ta, _, sb, _ = _np_cplx_tables(n // 2)
    return pl.pallas_call(
        _hyena_spectrum_kernel,
        grid=(n_ord, width // HY_LANES),
        in_specs=[pl.BlockSpec((1, n, HY_LANES), lambda o, c: (o, 0, c)),
                  pl.BlockSpec((FFT_N2, 2 * n1, n1), lambda o, c: (0, 0, 0)),
                  pl.BlockSpec((2 * FFT_N2, 2 * FFT_N2), lambda o, c: (0, 0))],
        out_specs=pl.BlockSpec((1, 2, n, HY_LANES), lambda o, c: (o, 0, 0, c)),
        out_shape=jax.ShapeDtypeStruct((n_ord, 2, n, width), F32),
        scratch_shapes=[pltpu.VMEM((2, n, HY_LANES), F32)],
        compiler_params=pltpu.CompilerParams(dimension_semantics=("arbitrary", "arbitrary"), vmem_limit_bytes=HY_VMEM),
        name="hyena_spectrum",
    )(kern, jnp.asarray(ta, BF16), jnp.asarray(sb, BF16))


def _short_conv(x_ref, w_ref, b_ref, n_ctx):
    x = x_ref[...].astype(F32)
    lt = x.shape[0]
    row = lax.broadcasted_iota(jnp.int32, (lt, 1), 0)
    prev = jnp.where((row == 0) | (row == n_ctx), 0.0, pltpu.roll(x, 1, axis=0))
    nxt = jnp.where((row == n_ctx - 1) | (row == lt - 1), 0.0, pltpu.roll(x, lt - 1, axis=0))
    return prev * w_ref[0:1, :] + x * w_ref[1:2, :] + nxt * w_ref[2:3, :] + b_ref[...]


def _hyena_conv_kernel(z_ref, g_ref, wz_ref, bz_ref, wg_ref, bg_ref, kl_ref, kc_ref, ta_ref, ti_ref, sb_ref, sbi_ref,
                       fc_ref, gc_ref, skip_ref, nw_ref, gm_ref, y_ref, zbuf, b_ref, *, n_ctx):
    order = pl.program_id(2)
    lt = zbuf.shape[0]
    n_lat = lt - n_ctx
    n1 = b_ref.shape[1] // FFT_N2

    @pl.when(order == 0)
    def _():
        zbuf[...] = _short_conv(z_ref, wz_ref, bz_ref, n_ctx)

    _fft_stage_a(lambda j, cnt: zbuf[pl.ds(n_ctx + j, cnt, stride=FFT_N2), :], ta_ref, b_ref, n1 // 2, n1)

    def freq(k1, carry):
        rows, x = _fft_stage_b_rows(b_ref, sb_ref, k1)
        xr, xi = x[:FFT_N2], x[FFT_N2:]
        kr, ki = kl_ref[0, 0, rows, :], kl_ref[0, 1, rows, :]
        y = jnp.concatenate([xr * kr - xi * ki, xr * ki + xi * kr], axis=0).astype(BF16)
        c = _dot(sbi_ref[...], y)
        b_ref[0, rows, :] = c[:FFT_N2]
        b_ref[1, rows, :] = c[FFT_N2:]
        return carry
    lax.fori_loop(0, n1, freq, 0, unroll=16)

    def inv_a(j, carry):
        rhs = jnp.concatenate([b_ref[0, pl.ds(j, n1, stride=FFT_N2), :], b_ref[1, pl.ds(j, n1, stride=FFT_N2), :]], axis=0)
        b_ref[0, pl.ds(j, n1 // 2, stride=FFT_N2), :] = _dot(ti_ref[j], rhs.astype(BF16))
        return carry
    lax.fori_loop(0, FFT_N2, inv_a, 0, unroll=16)

    zc = zbuf[0:n_ctx, :]
    xc = _dot(fc_ref[...], zc.astype(BF16))
    nc = 2 * n_ctx
    xr, xi = xc[:nc], xc[nc:]
    kr, ki = kc_ref[0, 0], kc_ref[0, 1]
    yc = jnp.concatenate([xr * kr - xi * ki, xr * ki + xi * kr], axis=0).astype(BF16)
    conv_c = _dot(gc_ref[...], yc)

    gate = _short_conv(g_ref, wg_ref, bg_ref, n_ctx)
    skip = skip_ref[0]
    zbuf[0:n_ctx, :] = gate[0:n_ctx] * (conv_c + zc * skip)
    zbuf[n_ctx:lt, :] = gate[n_ctx:lt] * (b_ref[0, 0:n_lat, :] + zbuf[n_ctx:lt, :] * skip)

    @pl.when(order == pl.num_programs(2) - 1)
    def _():
        z = zbuf[...]
        sq = z * z
        sq_hi = sq.astype(BF16)
        sq_lo = (sq - sq_hi.astype(F32)).astype(BF16)
        gm = gm_ref[...].astype(BF16)
        ms = _dot(sq_hi, gm) + _dot(sq_lo, gm)
        y_ref[...] = (z * lax.rsqrt(ms + EPS) * nw_ref[...]).astype(y_ref.dtype)


def hyena_mixer(p16, conv_w, conv_b, kf_lat, kf_ctx, skip, norm_w, batch, n_ctx):
    import numpy as np
    lt = p16.shape[0] // batch
    n_lat = lt - n_ctx
    width = HYENA_WIDTH
    cb = width // HY_LANES
    n = 2 * n_lat
    n1 = n // FFT_N2
    ta, tinv, sb, sbi = _np_cplx_tables(n_lat)
    ta = ta[:, :, :n1 // 2]
    fc, gc = _np_dense_dft(n_ctx)
    fc = fc[:, :n_ctx]
    gsz = width // HYENA_GROUPS
    gidx = np.arange(HY_LANES) // gsz
    gmean = (gidx[:, None] == gidx[None, :]).astype(np.float32) / gsz
    z_blk = HYENA_ORDER * cb
    cw = conv_w.astype(F32)
    cbias = conv_b.astype(F32).reshape(1, -1)
    const2 = lambda b, c, o: (0, 0)
    const3 = lambda b, c, o: (0, 0, 0)
    return pl.pallas_call(
        functools.partial(_hyena_conv_kernel, n_ctx=n_ctx),
        grid=(batch, cb, HYENA_ORDER),
        in_specs=[pl.BlockSpec((lt, HY_LANES), lambda b, c, o: (b, z_blk + c)),
                  pl.BlockSpec((lt, HY_LANES), lambda b, c, o: (b, o * cb + c)),
                  pl.BlockSpec((HYENA_SHORT, HY_LANES), lambda b, c, o: (0, z_blk + c)),
                  pl.BlockSpec((1, HY_LANES), lambda b, c, o: (0, z_blk + c)),
                  pl.BlockSpec((HYENA_SHORT, HY_LANES), lambda b, c, o: (0, o * cb + c)),
                  pl.BlockSpec((1, HY_LANES), lambda b, c, o: (0, o * cb + c)),
                  pl.BlockSpec((1, 2, n, HY_LANES), lambda b, c, o: (o, 0, 0, c)),
                  pl.BlockSpec((1, 2, 2 * n_ctx, HY_LANES), lambda b, c, o: (o, 0, 0, c)),
                  pl.BlockSpec((FFT_N2, 2 * n1, n1 // 2), const3),
                  pl.BlockSpec((FFT_N2, n1 // 2, 2 * n1), const3),
                  pl.BlockSpec((2 * FFT_N2, 2 * FFT_N2), const2),
                  pl.BlockSpec((2 * FFT_N2, 2 * FFT_N2), const2),
                  pl.BlockSpec((4 * n_ctx, n_ctx), const2),
                  pl.BlockSpec((n_ctx, 4 * n_ctx), const2),
                  pl.BlockSpec((1, 1, HY_LANES), lambda b, c, o: (o, 0, c)),
                  pl.BlockSpec((1, HY_LANES), lambda b, c, o: (0, c)),
                  pl.BlockSpec((HY_LANES, HY_LANES), const2)],
        out_specs=pl.BlockSpec((lt, HY_LANES), lambda b, c, o: (b, c)),
        out_shape=jax.ShapeDtypeStruct((batch * lt, width), BF16),
        scratch_shapes=[pltpu.VMEM((lt, HY_LANES), F32), pltpu.VMEM((2, n, HY_LANES), F32)],
        compiler_params=pltpu.CompilerParams(dimension_semantics=("arbitrary", "arbitrary", "arbitrary"), vmem_limit_bytes=HY_VMEM),
        name="hyena_mixer",
    )(p16, p16, cw, cbias, cw, cbias, kf_lat, kf_ctx,
      jnp.asarray(ta, BF16), jnp.asarray(tinv, BF16), jnp.asarray(sb, BF16), jnp.asarray(sbi, BF16),
      jnp.asarray(fc, BF16), jnp.asarray(gc, BF16), skip.astype(F32).reshape(HYENA_ORDER, 1, width),
      norm_w.astype(F32).reshape(1, width), jnp.asarray(gmean, F32))


def hyena_spectrum_dense(kern):
    n_ord, n, width = kern.shape
    fwd, _ = _np_dense_dft(n // 2)
    spec = pallas_matmul(jnp.asarray(fwd, F32), kern.transpose(1, 0, 2).reshape(n, n_ord * width), tm=2 * n, tn=width)
    return spec.reshape(2, n, n_ord, width).transpose(2, 0, 1, 3)


def hyena_kernels(length, fw1, fb1, fw2, fb2, fw3, fb3, freq, wout):
    h = hyena_filters(length, fw1, fb1, fw2, fb2, fw3, fb3, freq, wout)
    zero = jnp.zeros((HYENA_ORDER, 1, HYENA_WIDTH), F32)
    return jnp.concatenate([h[:, 0], zero, jnp.flip(h[:, 1, 1:], axis=1)], axis=1)


def rope_tables(n_ctx, n_lat):
    half = RET_DQK // 2
    inv = ROPE_BASE ** (-jnp.arange(0, half, 2, dtype=F32) / half)
    t = jnp.arange(n_lat)
    row = (t // GRID_W).astype(F32)
    col = (t % GRID_W).astype(F32)
    a_row = row[:, None] * inv[None, :]
    a_col = col[:, None] * inv[None, :]
    cos = jnp.concatenate([jnp.cos(a_row), jnp.cos(a_row), jnp.cos(a_col), jnp.cos(a_col)], -1)
    sin = jnp.concatenate([-jnp.sin(a_row), jnp.sin(a_row), -jnp.sin(a_col), jnp.sin(a_col)], -1)
    cos = jnp.concatenate([jnp.ones((n_ctx, HEAD_W), F32), cos], 0)
    sin = jnp.concatenate([jnp.zeros((n_ctx, HEAD_W), F32), sin], 0)
    return cos, sin


def kernel(x, c, ctx, c_ctx, ada_w, ada_b, ln_w, ln_b, even_w_in, mlstm_gate_b, mlstm_norm_w, hgrn_lb, hgrn_norm_w, even_w_out, odd_w_in, hy_conv_w, hy_conv_b, hy_f_w1, hy_f_b1, hy_f_w2, hy_f_b2, hy_f_w3, hy_f_b3, hy_f_freq, hy_f_wout, hy_skip, hy_norm_w, ret_log_decay, ret_norm_w, odd_w_out, router_w, router_bias, exp_w1, exp_w3, exp_w2, sh_w1, sh_w3, sh_w2):
    batch, n_lat, d = x.shape
    n_ctx = ctx.shape[1]
    lt = n_ctx + n_lat
    t = batch * lt
    sm = jax.nn.softmax(hgrn_lb.astype(F32), axis=0)
    lower_bounds = jnp.cumsum(sm, axis=0) - sm[0]
    acts = jnp.concatenate([jax.nn.silu(c.astype(F32)), jax.nn.silu(c_ctx.astype(F32))[None]], 0)
    acts = jnp.pad(acts, ((0, 16 - (batch + 1) % 16), (0, 0)))
    cos_t, sin_t = rope_tables(n_ctx, n_lat)
    h = jnp.concatenate([ctx, x], axis=1).reshape(t, d).astype(F32)
    for l in range(DEPTH):
        mods = pallas_matmul(acts, ada_w[l].astype(F32), tm=acts.shape[0], tn=512)[:batch + 1] + ada_b[l]
        mod_lat = mods[:batch].reshape(batch, 6, d)
        mod_ctx = jnp.broadcast_to(mods[batch].reshape(1, 6, d), (batch, 6, d))
        modtab = jnp.stack([mod_ctx, mod_lat], axis=1)
        if l % 2 == 0:
            e = l // 2
            w = even_w_in[e].astype(F32)
            c0 = sum(MLSTM_SPLITS[:4])
            c1 = c0 + MLSTM_SPLITS[4]
            c2 = c1 + HGRN_SPLITS[0] + HGRN_SPLITS[1]
            c3 = c2 + HGRN_SPLITS[2] + HGRN_SPLITS[3]
            w16 = jnp.concatenate([w[:, :c0], w[:, c1:c2], w[:, c3:]], 1).astype(BF16)
            w32 = jnp.concatenate([w[:, c0:c1], jnp.zeros((d, HEAD_W - MLSTM_SPLITS[4]), F32), w[:, c2:c3]], 1).astype(BF16)
            p16 = proj_in(h, modtab, w16, BF16, batch, n_ctx, tn=w16.shape[1] // 2)
            p32 = proj_in(h, modtab, w32, F32, batch, n_ctx, tn=w32.shape[1])
            ya = mlstm_mixer(p16, p32, mlstm_gate_b[e], mlstm_norm_w[e], batch, n_ctx)
            yb = hgrn_mixer(p16, p32, lower_bounds[e], hgrn_norm_w[e], batch, n_ctx)
            w_out = even_w_out[e]
        else:
            o = l // 2
            hw = ODD_SPLITS[0]
            p16 = proj_in(h, modtab, odd_w_in[o].astype(BF16), BF16, batch, n_ctx, tn=ODD_PROJ // 2)
            filt_p = [p.astype(F32) for p in (hy_f_w1[o], hy_f_b1[o], hy_f_w2[o], hy_f_b2[o], hy_f_w3[o], hy_f_b3[o], hy_f_freq[o], hy_f_wout[o])]
            kf_lat = hyena_spectrum(hyena_kernels(n_lat, *filt_p))
            kf_ctx = hyena_spectrum_dense(hyena_kernels(n_ctx, *filt_p))
            ya = hyena_mixer(p16, hy_conv_w[o], hy_conv_b[o], kf_lat, kf_ctx, hy_skip[o], hy_norm_w[o], batch, n_ctx)
            yb = retention_mixer(p16, hw // HEAD_W, ret_log_decay[o], ret_norm_w[o], cos_t, sin_t, batch, n_ctx)
            w_out = odd_w_out[o]
        ka = ya.shape[1]
        h1, v = proj_out_ln(ya, yb, w_out[:ka].astype(BF16), w_out[ka:].astype(BF16), h, modtab, ln_w[l, 0], ln_b[l, 0], batch, n_ctx)
        h = moe_ffn(v, router_w[l], router_bias[l], exp_w1[l], exp_w3[l], exp_w2[l], sh_w1[l], sh_w3[l], sh_w2[l],
                    h1, modtab, ln_w[l, 1], ln_b[l, 1], batch, n_ctx)
    return h.reshape(batch, lt, d)[:, n_ctx:]
```

```python
import functools
import math
import jax
import jax.numpy as jnp
from jax import lax
from jax.experimental import pallas as pl
from jax.experimental.pallas import tpu as pltpu

D_MODEL = 1024
BATCH = 8
SEQ = 4096
DEPTH = 4

GRID_W = 64
CTX_LEN = 256
N_EVEN = (DEPTH + 1) // 2
N_ODD = DEPTH // 2

MLSTM_HEADS = 4
MLSTM_DQK = 64
MLSTM_DV = 128
MLSTM_CHUNK = 64
HGRN_HEADS = 4
HGRN_DK = 128
HGRN_DV = 128
HGRN_CHUNK = 32
HYENA_WIDTH = 512
HYENA_GROUPS = 8
HYENA_ORDER = 2
HYENA_EMB = 33
HYENA_FILTER_HIDDEN = 64
HYENA_SHORT = 3
HYENA_TARGET = 1e-2
HYENA_FAST_PCT = 0.3
HYENA_SLOW_PCT = 1.5
RET_HEADS = 4
RET_DQK = 128
RET_DV = 128
RET_CHUNK = 64
ROPE_BASE = 10000.0
N_EXPERTS = 64
TOP_K = 8
EXPERT_HIDDEN = 256
SHARED_HIDDEN = 256
ROUTED_SCALE = 2.5
MOE_BLOCK = 128
DEEPNORM_ALPHA = (2 * DEPTH) ** 0.25
DEEPNORM_BETA = (8 * DEPTH) ** -0.25
EPS = 1e-5
NEG_BIG = -1e30

MLSTM_SPLITS = (MLSTM_HEADS * MLSTM_DQK, MLSTM_HEADS * MLSTM_DQK, MLSTM_HEADS * MLSTM_DV, MLSTM_HEADS * MLSTM_DV, 4 * MLSTM_HEADS)
HGRN_SPLITS = (HGRN_HEADS * HGRN_DK, HGRN_HEADS * HGRN_DV, HGRN_HEADS * HGRN_DK, HGRN_HEADS * HGRN_DK, HGRN_HEADS * HGRN_DV)
EVEN_SPLITS = MLSTM_SPLITS + HGRN_SPLITS
EVEN_PROJ = sum(EVEN_SPLITS)
EVEN_MIX = MLSTM_HEADS * MLSTM_DV + HGRN_HEADS * HGRN_DV
RET_SPLITS = (RET_HEADS * RET_DQK, RET_HEADS * RET_DQK, RET_HEADS * RET_DV, RET_HEADS * RET_DV)
ODD_SPLITS = ((HYENA_ORDER + 1) * HYENA_WIDTH,) + RET_SPLITS
ODD_PROJ = sum(ODD_SPLITS)
ODD_MIX = HYENA_WIDTH + RET_HEADS * RET_DV

F32 = jnp.float32
BF16 = jnp.bfloat16


def _mm_kernel(x_ref, w_ref, o_ref):
    o_ref[...] = jnp.dot(x_ref[...].astype(BF16), w_ref[...].astype(BF16), preferred_element_type=F32)


def pallas_matmul(x, w, tm=512, tn=512):
    t, k = x.shape
    n = w.shape[1]
    n_pad = -(-n // tn) * tn
    if n_pad != n:
        w = jnp.pad(w, ((0, 0), (0, n_pad - n)))
    out = pl.pallas_call(
        _mm_kernel,
        grid=(t // tm, n_pad // tn),
        in_specs=[pl.BlockSpec((tm, k), lambda i, j: (i, 0)), pl.BlockSpec((k, tn), lambda i, j: (0, j))],
        out_specs=pl.BlockSpec((tm, tn), lambda i, j: (i, j)),
        out_shape=jax.ShapeDtypeStruct((t, n_pad), F32),
    )(x, w)
    return out[:, :n]


def hyena_filters(length, fw1, fb1, fw2, fb2, fw3, fb3, freq, wout):
    pos = jnp.arange(length, dtype=F32)
    t = (pos / max(length - 1, 1))[:, None]
    bands = (HYENA_EMB - 1) // 2
    w = 2.0 * math.pi * pos[:, None] / length * jnp.linspace(1e-4, bands - 1, bands)[None, :]
    z = jnp.concatenate([t, jnp.cos(w), jnp.sin(w)], -1)
    hid = jnp.sin(freq * (z @ fw1 + fb1))
    hid = jnp.sin(freq * (hid @ fw2 + fb2))
    hid = jnp.sin(freq * (hid @ fw3 + fb3))
    h = (hid @ wout).reshape(length, HYENA_ORDER, 2, HYENA_WIDTH)
    deltas = jnp.abs(jnp.linspace(math.log(HYENA_TARGET) / HYENA_SLOW_PCT, math.log(HYENA_TARGET) / HYENA_FAST_PCT, HYENA_WIDTH))
    decay = jnp.exp(-t * deltas[None, :])
    return (h * decay[:, None, None, :]).transpose(1, 2, 0, 3)


ROUTE_TB = 512
DISPATCH_TB = 512
COMBINE_TB = 256
FFN_BM = 512


def _dot(a, b):
    return jnp.dot(a, b, preferred_element_type=F32)


def _route_kernel(v_ref, rw_ref, rb_ref, e_ref, rk_ref, g_ref, cnt_ref, carry_ref):
    tb, n_e = v_ref.shape[0], rw_ref.shape[1]

    @pl.when(pl.program_id(0) == 0)
    def _():
        carry_ref[...] = jnp.zeros_like(carry_ref)

    x = v_ref[...]
    w = rw_ref[...]
    xh = x.astype(BF16)
    xl = (x - xh.astype(F32)).astype(BF16)
    wh = w.astype(BF16)
    wl = (w - wh.astype(F32)).astype(BF16)
    logits = _dot(xh, wh) + (_dot(xh, wl) + _dot(xl, wh))
    scores = jax.nn.sigmoid(logits)
    work = scores + rb_ref[...]
    lane = lax.broadcasted_iota(jnp.int32, (tb, n_e), 1)
    col = lax.broadcasted_iota(jnp.int32, (tb, TOP_K), 1)
    e_out = jnp.zeros((tb, TOP_K), jnp.int32)
    g_out = jnp.zeros((tb, TOP_K), F32)
    mask = jnp.zeros((tb, n_e), F32)
    onehots = []
    for j in range(TOP_K):
        m = jnp.max(work, axis=1, keepdims=True)
        idx = jnp.min(jnp.where(work == m, lane, n_e), axis=1, keepdims=True)
        oh = lane == idx
        gj = jnp.sum(jnp.where(oh, scores, 0.0), axis=1, keepdims=True)
        e_out = jnp.where(col == j, idx, e_out)
        g_out = jnp.where(col == j, gj, g_out)
        work = jnp.where(oh, -jnp.inf, work)
        mask = mask + oh.astype(F32)
        onehots.append(oh)
    r_i = lax.broadcasted_iota(jnp.int32, (tb, tb), 0)
    c_i = lax.broadcasted_iota(jnp.int32, (tb, tb), 1)
    tri = (r_i > c_i).astype(BF16)
    rank = _dot(tri, mask.astype(BF16)) + carry_ref[...]
    rk_out = jnp.zeros((tb, TOP_K), jnp.int32)
    for j in range(TOP_K):
        rkj = jnp.sum(jnp.where(onehots[j], rank, 0.0), axis=1, keepdims=True)
        rk_out = jnp.where(col == j, rkj.astype(jnp.int32), rk_out)
    carry_ref[...] = carry_ref[...] + jnp.sum(mask, axis=0, keepdims=True)
    e_ref[...] = e_out
    rk_ref[...] = rk_out
    g_ref[...] = ROUTED_SCALE * g_out / jnp.sum(g_out, axis=1, keepdims=True)
    cnt_ref[...] = carry_ref[...]


def moe_route(v, router_w, router_bias):
    n_tok, d = v.shape
    tb = ROUTE_TB
    return pl.pallas_call(
        _route_kernel,
        grid=(n_tok // tb,),
        in_specs=[pl.BlockSpec((tb, d), lambda i: (i, 0)),
                  pl.BlockSpec((d, N_EXPERTS), lambda i: (0, 0)),
                  pl.BlockSpec((1, N_EXPERTS), lambda i: (0, 0))],
        out_specs=[pl.BlockSpec((tb, TOP_K), lambda i: (i, 0)),
                   pl.BlockSpec((tb, TOP_K), lambda i: (i, 0)),
                   pl.BlockSpec((tb, TOP_K), lambda i: (i, 0)),
                   pl.BlockSpec((1, N_EXPERTS), lambda i: (0, 0))],
        out_shape=[jax.ShapeDtypeStruct((n_tok, TOP_K), jnp.int32),
                   jax.ShapeDtypeStruct((n_tok, TOP_K), jnp.int32),
                   jax.ShapeDtypeStruct((n_tok, TOP_K), F32),
                   jax.ShapeDtypeStruct((1, N_EXPERTS), F32)],
        scratch_shapes=[pltpu.VMEM((1, N_EXPERTS), F32)],
        compiler_params=pltpu.CompilerParams(dimension_semantics=("arbitrary",)),
        name="moe_route",
    )(v, router_w, router_bias.reshape(1, N_EXPERTS))


U32 = jnp.uint32


def _pack_rows(x):
    w = x.shape[1] // 2
    lo = lax.bitcast_convert_type(x[:, :w].astype(BF16).astype(F32), U32)
    hi = lax.bitcast_convert_type(x[:, w:].astype(BF16).astype(F32), U32)
    return hi | (lo >> 16)


def _unpack_rows(u):
    lo = lax.bitcast_convert_type(u << 16, F32)
    hi = lax.bitcast_convert_type(u & jnp.uint32(0xFFFF0000), F32)
    return lo, hi


def _dispatch_kernel(zs_ref, zf_ref, nu_ref, pos_hbm, v_ref, xs_hbm, pos_smem, zbuf, pk_ref, sem_idx, sem_sc, sem_z):
    i = pl.program_id(0)
    tb = v_ref.shape[0]
    bm = zbuf.shape[0]
    n_blocks = xs_hbm.shape[0] // bm
    idx_cp = pltpu.make_async_copy(pos_hbm.at[i], pos_smem, sem_idx)
    idx_cp.start()
    pk_ref[...] = _pack_rows(v_ref[...])

    @pl.when(i == 0)
    def _():
        zbuf[...] = jnp.zeros_like(zbuf)
        for e in range(N_EXPERTS):
            @pl.when(zf_ref[e] != 0)
            def _():
                start = pl.multiple_of(zs_ref[e], bm)
                pltpu.make_async_copy(zbuf, xs_hbm.at[pl.ds(start, bm)], sem_z).start()
        for e in range(N_EXPERTS):
            @pl.when(zf_ref[e] != 0)
            def _():
                pltpu.make_async_copy(zbuf, xs_hbm.at[pl.ds(0, bm)], sem_z).wait()

        def tail(b, carry):
            cp = pltpu.make_async_copy(zbuf, xs_hbm.at[pl.ds(pl.multiple_of(b * bm, bm), bm)], sem_z)
            cp.start()
            cp.wait()
            return carry

        lax.fori_loop(nu_ref[0], n_blocks, tail, 0)

    idx_cp.wait()

    def body(t8, carry):
        row0 = pl.multiple_of(t8 * 8, 8)
        for r in range(8):
            for k in range(TOP_K):
                p = pos_smem[0, (row0 + r) * TOP_K + k]
                pltpu.make_async_copy(pk_ref.at[pl.ds(row0 + r, 1)], xs_hbm.at[pl.ds(p, 1)], sem_sc).start()
        return carry

    lax.fori_loop(0, tb // 8, body, 0)
    for k in range(TOP_K):
        pltpu.make_async_copy(pk_ref, pk_ref, sem_sc).wait()


def moe_dispatch(v, pos3, zero_start, zero_flag, n_used, n_rows):
    n_tok, d = v.shape
    tb = DISPATCH_TB
    dp = d // 2
    grid_spec = pltpu.PrefetchScalarGridSpec(
        num_scalar_prefetch=3,
        grid=(n_tok // tb,),
        in_specs=[pl.BlockSpec(memory_space=pl.ANY),
                  pl.BlockSpec((tb, d), lambda i, zs, zf, nu: (i, 0))],
        out_specs=pl.BlockSpec(memory_space=pl.ANY),
        scratch_shapes=[pltpu.SMEM((1, tb * TOP_K), jnp.int32),
                        pltpu.VMEM((FFN_BM, dp), U32),
                        pltpu.VMEM((tb, dp), U32),
                        pltpu.SemaphoreType.DMA(()),
                        pltpu.SemaphoreType.DMA(()),
                        pltpu.SemaphoreType.DMA(())],
    )
    return pl.pallas_call(
        _dispatch_kernel,
        grid_spec=grid_spec,
        out_shape=jax.ShapeDtypeStruct((n_rows, dp), U32),
        compiler_params=pltpu.CompilerParams(dimension_semantics=("arbitrary",)),
        name="moe_dispatch",
    )(zero_start, zero_flag, n_used, pos3, v)


def _ffn_kernel(be_ref, nu_ref, xs_ref, w1_ref, w3_ref, w2_ref, y_ref):
    used = pl.program_id(0) < nu_ref[0]

    @pl.when(used)
    def _():
        dp = xs_ref.shape[1]
        lo, hi = _unpack_rows(xs_ref[...])
        lo, hi = lo.astype(BF16), hi.astype(BF16)
        h1 = _dot(lo, w1_ref[0, :dp, :]) + _dot(hi, w1_ref[0, dp:, :])
        h3 = _dot(lo, w3_ref[0, :dp, :]) + _dot(hi, w3_ref[0, dp:, :])
        h = (h1 * jax.nn.sigmoid(h1) * h3).astype(BF16)
        y_ref[...] = _pack_rows(_dot(h, w2_ref[0]))

    @pl.when(jnp.logical_not(used))
    def _():
        y_ref[...] = jnp.zeros_like(y_ref)


def moe_expert_ffn(xs, block_e, n_used, w1, w3, w2):
    n_rows, dp = xs.shape
    d = 2 * dp
    bm = FFN_BM
    hid = w1.shape[2]

    def row_map(i, be, nu):
        return (jnp.minimum(i, nu[0] - 1), 0)

    def w_map(i, be, nu):
        return (be[jnp.minimum(i, nu[0] - 1)], 0, 0)

    grid_spec = pltpu.PrefetchScalarGridSpec(
        num_scalar_prefetch=2,
        grid=(n_rows // bm,),
        in_specs=[pl.BlockSpec((bm, dp), row_map),
                  pl.BlockSpec((1, d, hid), w_map),
                  pl.BlockSpec((1, d, hid), w_map),
                  pl.BlockSpec((1, hid, d), w_map)],
        out_specs=pl.BlockSpec((bm, dp), lambda i, be, nu: (i, 0)),
    )
    return pl.pallas_call(
        _ffn_kernel,
        grid_spec=grid_spec,
        out_shape=jax.ShapeDtypeStruct((n_rows, dp), U32),
        compiler_params=pltpu.CompilerParams(dimension_semantics=("arbitrary",)),
        name="moe_expert_ffn",
    )(block_e, n_used, xs, w1, w3, w2)


def _combine_kernel(pos_hbm, y_hbm, v_ref, g_ref, sw1_ref, sw3_ref, sw2_ref, h1_ref, mod_ref, lnw_ref, lnb_ref, o_ref,
                    pos0, pos1, yb0, yb1, sem_idx, sem_y, *, n_ctx, blocks_per_batch):
    i = pl.program_id(0)
    n = pl.num_programs(0)
    tb = v_ref.shape[0]
    dp = yb0.shape[1]
    slot = lax.rem(i, 2)
    pos_slots = (pos0, pos1)
    y_slots = (yb0, yb1)

    def fetch_idx(step, s):
        return pltpu.make_async_copy(pos_hbm.at[step], pos_slots[s], sem_idx.at[s])

    def issue_gather(s):
        def body(t8, carry):
            row0 = pl.multiple_of(t8 * 8, 8)
            for r in range(8):
                for k in range(TOP_K):
                    p = pos_slots[s][0, (row0 + r) * TOP_K + k]
                    pltpu.make_async_copy(y_hbm.at[pl.ds(p, 1)], y_slots[s].at[pl.ds(k * tb + row0 + r, 1)], sem_y.at[s]).start()
            return carry
        lax.fori_loop(0, tb // 8, body, 0)

    @pl.when(i == 0)
    def _():
        cp = fetch_idx(0, 0)
        cp.start()
        cp.wait()
        issue_gather(0)

    for s in range(2):
        @pl.when((i + 1 < n) & (slot == 1 - s))
        def _():
            cp = fetch_idx(i + 1, s)
            cp.start()
            cp.wait()
            issue_gather(s)

    x = v_ref[...].astype(BF16)
    h1 = _dot(x, sw1_ref[...])
    h3 = _dot(x, sw3_ref[...])
    shared = _dot((h1 * jax.nn.sigmoid(h1) * h3).astype(BF16), sw2_ref[...])
    def finish(s):
        yb = y_slots[s]
        for k in range(TOP_K):
            rows = yb.at[pl.ds(k * tb, tb)]
            pltpu.make_async_copy(rows, rows, sem_y.at[s]).wait()
        g = g_ref[...]
        acc_lo = shared[:, :dp]
        acc_hi = shared[:, dp:]
        for k in range(TOP_K):
            lo, hi = _unpack_rows(yb[pl.ds(k * tb, tb), :])
            acc_lo = acc_lo + g[:, k:k + 1] * lo
            acc_hi = acc_hi + g[:, k:k + 1] * hi
        f = jnp.concatenate([acc_lo, acc_hi], axis=1)
        is_ctx = _is_ctx_rows(tb, blocks_per_batch, n_ctx)
        z = DEEPNORM_ALPHA * h1_ref[...] + _row_mod(mod_ref, 5, is_ctx) * f
        o_ref[...] = _layer_norm_rows(z, lnw_ref[...], lnb_ref[...])

    for s in range(2):
        @pl.when(slot == s)
        def _():
            finish(s)


def moe_combine(pos3, y, v, gate, sw1, sw3, sw2, h1, modtab, ln_w, ln_b, batch, n_ctx):
    n_tok, d = v.shape
    tb = COMBINE_TB
    hid = sw1.shape[1]
    dp = y.shape[1]
    bpb = n_tok // batch // tb
    row = lambda i: (i, 0)
    fixed = lambda i: (0, 0)
    return pl.pallas_call(
        functools.partial(_combine_kernel, n_ctx=n_ctx, blocks_per_batch=bpb),
        grid=(n_tok // tb,),
        in_specs=[pl.BlockSpec(memory_space=pl.ANY),
                  pl.BlockSpec(memory_space=pl.ANY),
                  pl.BlockSpec((tb, d), row),
                  pl.BlockSpec((tb, TOP_K), row),
                  pl.BlockSpec((d, hid), fixed),
                  pl.BlockSpec((d, hid), fixed),
                  pl.BlockSpec((hid, d), fixed),
                  pl.BlockSpec((tb, d), row),
                  pl.BlockSpec((1, 2, 6, d), lambda i: (i // bpb, 0, 0, 0)),
                  pl.BlockSpec((1, d), fixed), pl.BlockSpec((1, d), fixed)],
        out_specs=pl.BlockSpec((tb, d), row),
        out_shape=jax.ShapeDtypeStruct((n_tok, d), F32),
        scratch_shapes=[pltpu.SMEM((1, tb * TOP_K), jnp.int32), pltpu.SMEM((1, tb * TOP_K), jnp.int32),
                        pltpu.VMEM((TOP_K * tb, dp), U32), pltpu.VMEM((TOP_K * tb, dp), U32),
                        pltpu.SemaphoreType.DMA((2,)),
                        pltpu.SemaphoreType.DMA((2,))],
        compiler_params=pltpu.CompilerParams(dimension_semantics=("arbitrary",), vmem_limit_bytes=48 * 1024 * 1024),
        name="moe_combine",
    )(pos3, y, v, gate, sw1, sw3, sw2, h1, modtab, ln_w.reshape(1, d).astype(F32), ln_b.reshape(1, d).astype(F32))


def moe_ffn(t, router_w, router_bias, w1, w3, w2, sw1, sw3, sw2, h1, modtab, ln_w, ln_b, batch, n_ctx):
    n_tok, d = t.shape
    tf = t.astype(F32)
    bm = FFN_BM
    e_sel, rank, gate, cnt = moe_route(tf, router_w.astype(F32), router_bias.astype(F32))
    counts = cnt[0].astype(jnp.int32)
    padded = (counts + bm - 1) // bm * bm
    pend = jnp.cumsum(padded)
    pstart = pend - padded
    n_blocks = -(-(n_tok * TOP_K + N_EXPERTS * (bm - 1)) // bm)
    expert_ids = jnp.arange(N_EXPERTS, dtype=jnp.int32)
    pos = rank + jnp.sum(jnp.where(e_sel[..., None] == expert_ids, pstart, 0), -1)
    block_start = jnp.arange(n_blocks, dtype=jnp.int32) * bm
    block_e = jnp.minimum(jnp.sum((pend[None, :] <= block_start[:, None]).astype(jnp.int32), axis=1), N_EXPERTS - 1)
    n_used = (pend[-1:] // bm).astype(jnp.int32)
    zero_start = jnp.maximum(pend - bm, 0).astype(jnp.int32)
    zero_flag = (counts > 0).astype(jnp.int32)
    xs = moe_dispatch(tf, pos.reshape(n_tok // DISPATCH_TB, 1, DISPATCH_TB * TOP_K), zero_start, zero_flag, n_used, n_blocks * bm)
    y = moe_expert_ffn(xs, block_e, n_used, w1.astype(BF16), w3.astype(BF16), w2.astype(BF16))
    return moe_combine(pos.reshape(n_tok // COMBINE_TB, 1, COMBINE_TB * TOP_K), y, tf, gate,
                       sw1.astype(BF16), sw3.astype(BF16), sw2.astype(BF16), h1, modtab, ln_w, ln_b, batch, n_ctx)


HEAD_W = 128
MIX_VMEM = 48 * 1024 * 1024
NORM_ROWS = 256


def _dot_nt(a, b):
    return lax.dot_general(a, b, (((1,), (1,)), ((), ())), preferred_element_type=F32)


def _dot_tn(a, b):
    return lax.dot_general(a, b, (((0,), (0,)), ((), ())), preferred_element_type=F32)


def _dot_hi(a, b):
    bh = b.astype(BF16)
    bl = (b - bh.astype(F32)).astype(BF16)
    a16 = a.astype(BF16)
    return _dot(a16, bh) + _dot(a16, bl)


def _tri(n, upper):
    r = lax.broadcasted_iota(jnp.int32, (n, n), 0)
    c = lax.broadcasted_iota(jnp.int32, (n, n), 1)
    return (r <= c) if upper else (r >= c)


def _bidir_loop(n_ctx_chunks, n_chunks, step, unroll):
    def body(c, carry):
        cb = jnp.where(c < n_ctx_chunks, n_ctx_chunks - 1 - c, n_chunks - 1 + n_ctx_chunks - c)
        live, stores = [step(c, False), step(cb, True)], []
        while live:
            waiting = []
            for gen in live:
                if isinstance(gen, list):
                    stores += gen
                    continue
                try:
                    next(gen)
                    waiting.append(gen)
                except StopIteration as done:
                    stores += done.value
            live = waiting
        for ref, idx, val in stores:
            ref[idx] = val
        return carry
    lax.fori_loop(0, n_chunks, body, 0, unroll=unroll)


def _norm_gate_epilogue(hf_ref, hb_ref, gate_ref, nw_ref, y_ref, n_rows, rms, gate_fn):
    def body(i, carry):
        rows = pl.ds(pl.multiple_of(i * NORM_ROWS, NORM_ROWS), NORM_ROWS)
        h = hf_ref[rows, :] + hb_ref[rows, :]
        if not rms:
            h = h - jnp.mean(h, axis=1, keepdims=True)
        h = h * lax.rsqrt(jnp.mean(h * h, axis=1, keepdims=True) + EPS)
        y_ref[rows, :] = (h * nw_ref[...] * gate_fn(gate_ref[rows, :].astype(F32))).astype(y_ref.dtype)
        return carry
    lax.fori_loop(0, n_rows // NORM_ROWS, body, 0)


def _silu(x):
    return x * jax.nn.sigmoid(x)


def _retention_kernel(ld_ref, q_ref, k_ref, v_ref, g_ref, cos_ref, sin_ref, perm_ref, nw_ref, y_ref,
                      qr_ref, kr_ref, hf_ref, hb_ref, sf_ref, sb_ref, *, n_ctx):
    cs = RET_CHUNK
    lt = q_ref.shape[0]
    hd = pl.program_id(1)

    def rope(i, carry):
        rows = pl.ds(pl.multiple_of(i * NORM_ROWS, NORM_ROWS), NORM_ROWS)
        cs_, sn_ = cos_ref[rows, :], sin_ref[rows, :]
        q = q_ref[rows, :]
        k = k_ref[rows, :]
        qr_ref[rows, :] = (q.astype(F32) * cs_ + _dot(q, perm_ref[...]) * sn_).astype(BF16)
        kr_ref[rows, :] = ((k.astype(F32) * cs_ + _dot(k, perm_ref[...]) * sn_) * RET_DQK ** -0.5).astype(BF16)
        return carry
    lax.fori_loop(0, lt // NORM_ROWS, rope, 0)

    pos_r = lax.broadcasted_iota(jnp.int32, (cs, cs), 0).astype(F32)
    pos_c = lax.broadcasted_iota(jnp.int32, (cs, cs), 1).astype(F32)
    pos_t = lax.broadcasted_iota(jnp.int32, (cs, HEAD_W), 0).astype(F32)
    consts = []
    for d in range(2):
        lg = ld_ref[d, hd]
        if d == 0:
            decay = jnp.where(pos_r >= pos_c, jnp.exp((pos_r - pos_c) * lg), 0.0)
            q_dec = jnp.exp((pos_t + 1.0) * lg)
            k_dec = jnp.exp((cs - 1.0 - pos_t) * lg)
        else:
            decay = jnp.where(pos_r <= pos_c, jnp.exp((pos_c - pos_r) * lg), 0.0)
            q_dec = jnp.exp((cs - pos_t) * lg)
            k_dec = jnp.exp(pos_t * lg)
        consts.append((decay, q_dec, k_dec, jnp.exp(cs * lg)))
    sf_ref[...] = jnp.zeros_like(sf_ref)
    sb_ref[...] = jnp.zeros_like(sb_ref)

    def step(chunk, reverse):
        decay, q_dec, k_dec, c_dec = consts[1 if reverse else 0]
        s_ref = sb_ref if reverse else sf_ref
        h_ref = hb_ref if reverse else hf_ref
        rows = pl.ds(pl.multiple_of(chunk * cs, cs), cs)
        q = qr_ref[rows, :]
        k = kr_ref[rows, :]
        v = v_ref[rows, :]
        sc = _dot_nt(q, k) * decay
        s_in = s_ref[...]
        inter = _dot((q.astype(F32) * q_dec).astype(BF16), s_in.astype(BF16))
        return [(h_ref, (rows, slice(None)), inter + _dot(sc.astype(BF16), v)),
                (s_ref, (slice(None), slice(None)), c_dec * s_in + _dot_tn((k.astype(F32) * k_dec).astype(BF16), v))]

    _bidir_loop(n_ctx // cs, lt // cs, step, 8)
    _norm_gate_epilogue(hf_ref, hb_ref, g_ref, nw_ref, y_ref, lt, False, _silu)


def retention_mixer(p16, col0, log_decay, norm_w, cos_t, sin_t, batch, n_ctx):
    lt = p16.shape[0] // batch
    nh = RET_HEADS
    half = RET_DQK // 2
    src = jnp.arange(HEAD_W)
    swapped = jnp.where((src % half) < half // 2, src + half // 2, src - half // 2)
    perm = (src[:, None] == swapped[None, :]).astype(BF16)

    def blk(off):
        return pl.BlockSpec((lt, HEAD_W), lambda b, h, ld: (b, off + h))

    grid_spec = pltpu.PrefetchScalarGridSpec(
        num_scalar_prefetch=1,
        grid=(batch, nh),
        in_specs=[blk(col0), blk(col0 + nh), blk(col0 + 2 * nh), blk(col0 + 3 * nh),
                  pl.BlockSpec((lt, HEAD_W), lambda b, h, ld: (0, 0)),
                  pl.BlockSpec((lt, HEAD_W), lambda b, h, ld: (0, 0)),
                  pl.BlockSpec((HEAD_W, HEAD_W), lambda b, h, ld: (0, 0)),
                  pl.BlockSpec((1, HEAD_W), lambda b, h, ld: (0, h))],
        out_specs=pl.BlockSpec((lt, HEAD_W), lambda b, h, ld: (b, h)),
        scratch_shapes=[pltpu.VMEM((lt, HEAD_W), BF16), pltpu.VMEM((lt, HEAD_W), BF16),
                        pltpu.VMEM((lt, HEAD_W), F32), pltpu.VMEM((lt, HEAD_W), F32),
                        pltpu.VMEM((RET_DQK, RET_DV), F32), pltpu.VMEM((RET_DQK, RET_DV), F32)],
    )
    return pl.pallas_call(
        functools.partial(_retention_kernel, n_ctx=n_ctx),
        grid_spec=grid_spec,
        out_shape=jax.ShapeDtypeStruct((batch * lt, nh * HEAD_W), BF16),
        compiler_params=pltpu.CompilerParams(dimension_semantics=("arbitrary", "arbitrary"), vmem_limit_bytes=MIX_VMEM),
        name="retention_mixer",
    )(log_decay.astype(F32), p16, p16, p16, p16, cos_t, sin_t, perm, norm_w.reshape(1, -1).astype(F32))


def _hgrn_kernel(q_ref, i_ref, g_ref, ff_ref, fb_ref, lb_ref, nw_ref, y_ref, hf_ref, hb_ref, sf_ref, sb_ref, *, n_ctx):
    cs = HGRN_CHUNK
    lt = q_ref.shape[0]
    lb = lb_ref[...]
    sf_ref[...] = jnp.zeros_like(sf_ref)
    sb_ref[...] = jnp.zeros_like(sb_ref)
    tri = (_tri(cs, False).astype(F32), _tri(cs, True).astype(F32))
    mask = (_tri(cs, False), _tri(cs, True))

    def step(chunk, reverse):
        d = 1 if reverse else 0
        f_ref = fb_ref if reverse else ff_ref
        s_ref = sb_ref if reverse else sf_ref
        h_ref = hb_ref if reverse else hf_ref
        rows = pl.ds(pl.multiple_of(chunk * cs, cs), cs)
        f = lb + (1.0 - lb) * jax.nn.sigmoid(f_ref[rows, :])
        kk = 1.0 - f
        a_cum = _dot_hi(tri[d], jnp.log(f))
        q = _silu(q_ref[rows, :].astype(F32))
        v = i_ref[rows, :]
        s_in = s_ref[...]
        yield
        a_end = a_cum[0:1, :] if reverse else a_cum[cs - 1:cs, :]
        mid = cs // 2 if reverse else cs // 2 - 1
        a_mid = a_cum[mid:mid + 1, :]
        inter = _dot_nt((q * jnp.exp(a_cum)).astype(BF16), s_in.astype(BF16))
        sc = _dot_nt((q * jnp.exp(a_cum - a_mid)).astype(BF16), (kk * jnp.exp(a_mid - a_cum)).astype(BF16))
        yield
        sc = jnp.where(mask[d], sc, 0.0)
        return [(h_ref, (rows, slice(None)), inter + _dot(sc.astype(BF16), v)),
                (s_ref, (slice(None), slice(None)), s_in * jnp.exp(a_end) + _dot_tn(v, (kk * jnp.exp(a_end - a_cum)).astype(BF16)))]

    _bidir_loop(n_ctx // cs, lt // cs, step, 4)
    _norm_gate_epilogue(hf_ref, hb_ref, g_ref, nw_ref, y_ref, lt, True, _silu)


EVEN16_MLSTM_Q, EVEN16_MLSTM_K, EVEN16_MLSTM_V, EVEN16_MLSTM_O = 0, 2, 4, 8
EVEN16_HGRN_Q, EVEN16_HGRN_I, EVEN16_HGRN_G = 12, 16, 20
EVEN32_GATES, EVEN32_FF, EVEN32_FB = 0, 1, 5


def hgrn_mixer(p16, p32, lb, norm_w, batch, n_ctx):
    lt = p16.shape[0] // batch
    nh = HGRN_HEADS

    def blk(off):
        return pl.BlockSpec((lt, HEAD_W), lambda b, h: (b, off + h))

    def vec():
        return pl.BlockSpec((1, HEAD_W), lambda b, h: (0, h))

    return pl.pallas_call(
        functools.partial(_hgrn_kernel, n_ctx=n_ctx),
        grid=(batch, nh),
        in_specs=[blk(EVEN16_HGRN_Q), blk(EVEN16_HGRN_I), blk(EVEN16_HGRN_G), blk(EVEN32_FF), blk(EVEN32_FB), vec(), vec()],
        out_specs=pl.BlockSpec((lt, HEAD_W), lambda b, h: (b, h)),
        out_shape=jax.ShapeDtypeStruct((batch * lt, nh * HEAD_W), BF16),
        scratch_shapes=[pltpu.VMEM((lt, HEAD_W), F32), pltpu.VMEM((lt, HEAD_W), F32),
                        pltpu.VMEM((HGRN_DV, HGRN_DK), F32), pltpu.VMEM((HGRN_DV, HGRN_DK), F32)],
        compiler_params=pltpu.CompilerParams(dimension_semantics=("arbitrary", "arbitrary"), vmem_limit_bytes=MIX_VMEM),
        name="hgrn_mixer",
    )(p16, p16, p16, p32, p32, lb.reshape(1, -1).astype(F32), norm_w.reshape(1, -1).astype(F32))


def _mlstm_kernel(q_ref, k_ref, v_ref, o_ref, gt_ref, gb_ref, nw_ref, y_ref, hf_ref, hb_ref, cf_ref, cb_ref, mf_ref, mb_ref, *, n_ctx):
    cs = MLSTM_CHUNK
    lt = q_ref.shape[0]
    hd = pl.program_id(1)
    lane = lax.broadcasted_iota(jnp.int32, (1, HEAD_W), 1)
    head_mask = ((lane // MLSTM_DQK) == (hd % 2)).astype(F32)
    gcol = lax.broadcasted_iota(jnp.int32, (1, HEAD_W), 1)
    tri = (_tri(cs, False).astype(F32), _tri(cs, True).astype(F32))
    mask = (_tri(cs, False), _tri(cs, True))
    ones_v = jnp.ones((cs, HEAD_W), BF16)
    for ref in (cf_ref, cb_ref):
        ref[...] = jnp.zeros_like(ref)
    for ref in (mf_ref, mb_ref):
        ref[...] = jnp.full_like(ref, NEG_BIG)

    def pick(tile, col):
        return jnp.sum(jnp.where(gcol == col, tile, 0.0), axis=1, keepdims=True)

    def step(chunk, reverse):
        d = 1 if reverse else 0
        c_ref = cb_ref if reverse else cf_ref
        m_ref = mb_ref if reverse else mf_ref
        h_ref = hb_ref if reverse else hf_ref
        rows = pl.ds(pl.multiple_of(chunk * cs, cs), cs)
        x = gt_ref[rows, :] + gb_ref[...]
        ci = hd + (2 * MLSTM_HEADS if reverse else 0)
        ig = pick(x, ci)
        lf = pick(jax.nn.log_sigmoid(x), ci + MLSTM_HEADS)
        lf_t = jnp.broadcast_to(lf, (cs, HEAD_W))
        b = _dot_hi(tri[d], lf_t)
        q = q_ref[rows, :]
        k = k_ref[rows, :]
        v = v_ref[rows, :]
        qm = (q.astype(F32) * (head_mask * MLSTM_DQK ** -0.5)).astype(BF16)
        s_raw = _dot_nt(qm, k)
        c_in = c_ref[...]
        r_state = _dot(qm, c_in.astype(BF16))
        yield
        g = b[0:1, :] if reverse else b[cs - 1:cs, :]
        ig_t = jnp.broadcast_to(ig, (cs, HEAD_W))
        a = g - b + ig_t
        m_loc = jnp.max(a, axis=0, keepdims=True)
        w = jnp.exp(a - m_loc)
        m_in = m_ref[...]
        r_row = jnp.transpose(ig_t - b)[0:1, 0:cs]
        log_d = jnp.where(mask[d], b[:, 0:cs] + r_row, -jnp.inf)
        inter = b + m_in
        m_out = jnp.maximum(inter, jnp.max(log_d, axis=1, keepdims=True))
        s = s_raw * jnp.exp(log_d - m_out[:, 0:cs])
        e_int = jnp.exp(inter - m_out)
        vo = jnp.concatenate([v, ones_v], axis=1)
        kw = (k.astype(F32) * head_mask * w).astype(BF16)
        r_intra = _dot(s.astype(BF16), vo)
        c_loc = _dot_tn(kw, vo)
        yield
        r = r_intra + jnp.concatenate([e_int, e_int], axis=1) * r_state
        num, den = r[:, :HEAD_W], r[:, HEAD_W:]
        h_out = num / jnp.maximum(jnp.abs(den), jnp.exp(-m_out))
        m_new = jnp.maximum(g + m_in, m_loc)
        sp = jnp.exp(g + m_in - m_new)
        sl = jnp.exp(m_loc - m_new)
        c_new = jnp.concatenate([sp, sp], axis=1) * c_in + jnp.concatenate([sl, sl], axis=1) * c_loc
        full = (slice(None), slice(None))
        return [(h_ref, (rows, slice(None)), h_out), (c_ref, full, c_new), (m_ref, full, m_new)]

    _bidir_loop(n_ctx // cs, lt // cs, step, 4)
    _norm_gate_epilogue(hf_ref, hb_ref, o_ref, nw_ref, y_ref, lt, False, jax.nn.sigmoid)


def mlstm_mixer(p16, p32, gate_b, norm_w, batch, n_ctx):
    lt = p16.shape[0] // batch
    nh = MLSTM_HEADS
    gb = jnp.zeros((1, HEAD_W), F32).at[0, :4 * nh].set(gate_b.astype(F32).reshape(-1))

    return pl.pallas_call(
        functools.partial(_mlstm_kernel, n_ctx=n_ctx),
        grid=(batch, nh),
        in_specs=[pl.BlockSpec((lt, HEAD_W), lambda b, h: (b, EVEN16_MLSTM_Q + h // 2)),
                  pl.BlockSpec((lt, HEAD_W), lambda b, h: (b, EVEN16_MLSTM_K + h // 2)),
                  pl.BlockSpec((lt, HEAD_W), lambda b, h: (b, EVEN16_MLSTM_V + h)),
                  pl.BlockSpec((lt, HEAD_W), lambda b, h: (b, EVEN16_MLSTM_O + h)),
                  pl.BlockSpec((lt, HEAD_W), lambda b, h: (b, EVEN32_GATES)),
                  pl.BlockSpec((1, HEAD_W), lambda b, h: (0, 0)),
                  pl.BlockSpec((1, HEAD_W), lambda b, h: (0, h))],
        out_specs=pl.BlockSpec((lt, HEAD_W), lambda b, h: (b, h)),
        out_shape=jax.ShapeDtypeStruct((batch * lt, nh * HEAD_W), BF16),
        scratch_shapes=[pltpu.VMEM((lt, HEAD_W), F32), pltpu.VMEM((lt, HEAD_W), F32),
                        pltpu.VMEM((HEAD_W, 2 * HEAD_W), F32), pltpu.VMEM((HEAD_W, 2 * HEAD_W), F32),
                        pltpu.VMEM((1, HEAD_W), F32), pltpu.VMEM((1, HEAD_W), F32)],
        compiler_params=pltpu.CompilerParams(dimension_semantics=("arbitrary", "arbitrary"), vmem_limit_bytes=MIX_VMEM),
        name="mlstm_mixer",
    )(p16, p16, p16, p16, p32, gb, norm_w.reshape(1, -1).astype(F32))


ROW_BLOCKS_PER_BATCH_IN = 4
ROW_BLOCKS_PER_BATCH_OUT = 8
PROJ_TN = 512


def _row_mod(mod_ref, idx, is_ctx):
    return jnp.where(is_ctx, mod_ref[0, 0, idx:idx + 1, :], mod_ref[0, 1, idx:idx + 1, :])


def _is_ctx_rows(tm, blocks_per_batch, n_ctx):
    row = (pl.program_id(0) % blocks_per_batch) * tm + lax.broadcasted_iota(jnp.int32, (tm, 1), 0)
    return row < n_ctx


def _proj_in_kernel(h_ref, mod_ref, w_ref, o_ref, u_ref, *, n_ctx, blocks_per_batch):
    tm = h_ref.shape[0]

    @pl.when(pl.program_id(1) == 0)
    def _():
        is_ctx = _is_ctx_rows(tm, blocks_per_batch, n_ctx)
        u = h_ref[...] * (1.0 + _row_mod(mod_ref, 1, is_ctx)) + _row_mod(mod_ref, 0, is_ctx)
        u_ref[...] = u.astype(BF16)

    o_ref[...] = _dot(u_ref[...], w_ref[...]).astype(o_ref.dtype)


def proj_in(h, modtab, w, out_dtype, batch, n_ctx, tn=PROJ_TN):
    t, d = h.shape
    n = w.shape[1]
    bpb = ROW_BLOCKS_PER_BATCH_IN
    tm = t // batch // bpb
    return pl.pallas_call(
        functools.partial(_proj_in_kernel, n_ctx=n_ctx, blocks_per_batch=bpb),
        grid=(t // tm, n // tn),
        in_specs=[pl.BlockSpec((tm, d), lambda i, j: (i, 0)),
                  pl.BlockSpec((1, 2, 6, d), lambda i, j: (i // bpb, 0, 0, 0)),
                  pl.BlockSpec((d, tn), lambda i, j: (0, j))],
        out_specs=pl.BlockSpec((tm, tn), lambda i, j: (i, j)),
        out_shape=jax.ShapeDtypeStruct((t, n), out_dtype),
        scratch_shapes=[pltpu.VMEM((tm, d), BF16)],
        compiler_params=pltpu.CompilerParams(dimension_semantics=("arbitrary", "arbitrary"), vmem_limit_bytes=MIX_VMEM),
        name="proj_in",
    )(h, modtab, w)


def _layer_norm_rows(z, w, b):
    mu = jnp.mean(z, axis=1, keepdims=True)
    zc = z - mu
    var = jnp.mean(zc * zc, axis=1, keepdims=True)
    return zc * lax.rsqrt(var + EPS) * w + b


def _proj_out_kernel(ya_ref, yb_ref, wa_ref, wb_ref, h_ref, mod_ref, lnw_ref, lnb_ref, h1_ref, v_ref, *, n_ctx, blocks_per_batch):
    tm = h_ref.shape[0]
    is_ctx = _is_ctx_rows(tm, blocks_per_batch, n_ctx)
    y = _dot(ya_ref[...], wa_ref[...]) + _dot(yb_ref[...], wb_ref[...])
    z = DEEPNORM_ALPHA * h_ref[...] + _row_mod(mod_ref, 2, is_ctx) * y
    h1 = _layer_norm_rows(z, lnw_ref[...], lnb_ref[...])
    h1_ref[...] = h1
    v_ref[...] = h1 * (1.0 + _row_mod(mod_ref, 4, is_ctx)) + _row_mod(mod_ref, 3, is_ctx)


def proj_out_ln(ya, yb, wa, wb, h, modtab, ln_w, ln_b, batch, n_ctx):
    t, d = h.shape
    bpb = ROW_BLOCKS_PER_BATCH_OUT
    tm = t // batch // bpb
    ka, kb = ya.shape[1], yb.shape[1]
    row = lambda i: (i, 0)
    fixed = lambda i: (0, 0)
    return pl.pallas_call(
        functools.partial(_proj_out_kernel, n_ctx=n_ctx, blocks_per_batch=bpb),
        grid=(t // tm,),
        in_specs=[pl.BlockSpec((tm, ka), row), pl.BlockSpec((tm, kb), row),
                  pl.BlockSpec((ka, d), fixed), pl.BlockSpec((kb, d), fixed),
                  pl.BlockSpec((tm, d), row),
                  pl.BlockSpec((1, 2, 6, d), lambda i: (i // bpb, 0, 0, 0)),
                  pl.BlockSpec((1, d), fixed), pl.BlockSpec((1, d), fixed)],
        out_specs=[pl.BlockSpec((tm, d), row), pl.BlockSpec((tm, d), row)],
        out_shape=[jax.ShapeDtypeStruct((t, d), F32), jax.ShapeDtypeStruct((t, d), F32)],
        compiler_params=pltpu.CompilerParams(dimension_semantics=("arbitrary",), vmem_limit_bytes=MIX_VMEM),
        name="proj_out_ln",
    )(ya, yb, wa, wb, h, modtab, ln_w.reshape(1, d).astype(F32), ln_b.reshape(1, d).astype(F32))


FFT_N2 = 128
HY_LANES = 128
HY_VMEM = 56 * 1024 * 1024


def _np_cplx_tables(n_len):
    import numpy as np
    n = 2 * n_len
    n2 = FFT_N2
    n1 = n // n2
    w = lambda m, e: np.exp(-2j * np.pi * (e % m) / m)
    k1 = np.arange(n1)
    a = np.arange(n1)
    j2 = np.arange(n2)
    ta = w(n, j2[:, None, None] * k1[None, :, None]) * w(n1, k1[None, :, None] * a[None, None, :])
    ta_stack = np.concatenate([ta.real, ta.imag], axis=1)
    tinv = np.conj(ta).transpose(0, 2, 1) / n
    tinv_stack = np.concatenate([tinv.real, -tinv.imag], axis=2)[:, :n1 // 2]
    f2 = w(n2, j2[:, None] * j2[None, :])
    sb = np.block([[f2.real, -f2.imag], [f2.imag, f2.real]])
    sbi = np.block([[f2.real, f2.imag], [-f2.imag, f2.real]])
    return ta_stack, tinv_stack, sb, sbi


def _np_dense_dft(n_len):
    import numpy as np
    n = 2 * n_len
    k = np.arange(n)
    f = np.exp(-2j * np.pi * ((k[:, None] * k[None, :]) % n) / n)
    fwd = np.concatenate([f.real, f.imag], axis=0)
    g = np.conj(f)[:n_len] / n
    inv = np.concatenate([g.real, -g.imag], axis=1)
    return fwd, inv


def _fft_stage_a(load_rows, ta_ref, b_ref, n1_in, n1):
    def body(j, carry):
        x = load_rows(j, n1_in).astype(BF16)
        a = _dot(ta_ref[j], x)
        b_ref[0, pl.ds(j, n1, stride=FFT_N2), :] = a[:n1]
        b_ref[1, pl.ds(j, n1, stride=FFT_N2), :] = a[n1:]
        return carry
    lax.fori_loop(0, FFT_N2, body, 0, unroll=16)


def _fft_stage_b_rows(b_ref, sb_ref, k1):
    rows = pl.ds(pl.multiple_of(k1 * FFT_N2, FFT_N2), FFT_N2)
    rhs = jnp.concatenate([b_ref[0, rows, :], b_ref[1, rows, :]], axis=0).astype(BF16)
    return rows, _dot(sb_ref[...], rhs)


def _hyena_spectrum_kernel(k_ref, ta_ref, sb_ref, o_ref, b_ref):
    n1 = b_ref.shape[1] // FFT_N2
    _fft_stage_a(lambda j, cnt: k_ref[0, pl.ds(j, cnt, stride=FFT_N2), :], ta_ref, b_ref, n1, n1)

    def body(k1, carry):
        rows, x = _fft_stage_b_rows(b_ref, sb_ref, k1)
        o_ref[0, 0, rows, :] = x[:FFT_N2]
        o_ref[0, 1, rows, :] = x[FFT_N2:]
        return carry
    lax.fori_loop(0, n1, body, 0)


def hyena_spectrum(kern):
    n_ord, n, width = kern.shape
    n1 = n // FFT_N2
    ta, _, sb, _ = _np_cplx_tables(n // 2)
    return pl.pallas_call(
        _hyena_spectrum_kernel,
        grid=(n_ord, width // HY_LANES),
        in_specs=[pl.BlockSpec((1, n, HY_LANES), lambda o, c: (o, 0, c)),
                  pl.BlockSpec((FFT_N2, 2 * n1, n1), lambda o, c: (0, 0, 0)),
                  pl.BlockSpec((2 * FFT_N2, 2 * FFT_N2), lambda o, c: (0, 0))],
        out_specs=pl.BlockSpec((1, 2, n, HY_LANES), lambda o, c: (o, 0, 0, c)),
        out_shape=jax.ShapeDtypeStruct((n_ord, 2, n, width), F32),
        scratch_shapes=[pltpu.VMEM((2, n, HY_LANES), F32)],
        compiler_params=pltpu.CompilerParams(dimension_semantics=("arbitrary", "arbitrary"), vmem_limit_bytes=HY_VMEM),
        name="hyena_spectrum",
    )(kern, jnp.asarray(ta, BF16), jnp.asarray(sb, BF16))


def _short_conv(x_ref, w_ref, b_ref, n_ctx):
    x = x_ref[...].astype(F32)
    lt = x.shape[0]
    row = lax.broadcasted_iota(jnp.int32, (lt, 1), 0)
    prev = jnp.where((row == 0) | (row == n_ctx), 0.0, pltpu.roll(x, 1, axis=0))
    nxt = jnp.where((row == n_ctx - 1) | (row == lt - 1), 0.0, pltpu.roll(x, lt - 1, axis=0))
    return prev * w_ref[0:1, :] + x * w_ref[1:2, :] + nxt * w_ref[2:3, :] + b_ref[...]


def _hyena_conv_kernel(z_ref, g_ref, wz_ref, bz_ref, wg_ref, bg_ref, kl_ref, kc_ref, ta_ref, ti_ref, sb_ref, sbi_ref,
                       fc_ref, gc_ref, skip_ref, nw_ref, gm_ref, y_ref, zbuf, b_ref, *, n_ctx):
    order = pl.program_id(2)
    lt = zbuf.shape[0]
    n_lat = lt - n_ctx
    n1 = b_ref.shape[1] // FFT_N2

    @pl.when(order == 0)
    def _():
        zbuf[...] = _short_conv(z_ref, wz_ref, bz_ref, n_ctx)

    _fft_stage_a(lambda j, cnt: zbuf[pl.ds(n_ctx + j, cnt, stride=FFT_N2), :], ta_ref, b_ref, n1 // 2, n1)

    def freq(k1, carry):
        rows, x = _fft_stage_b_rows(b_ref, sb_ref, k1)
        xr, xi = x[:FFT_N2], x[FFT_N2:]
        kr, ki = kl_ref[0, 0, rows, :], kl_ref[0, 1, rows, :]
        y = jnp.concatenate([xr * kr - xi * ki, xr * ki + xi * kr], axis=0).astype(BF16)
        c = _dot(sbi_ref[...], y)
        b_ref[0, rows, :] = c[:FFT_N2]
        b_ref[1, rows, :] = c[FFT_N2:]
        return carry
    lax.fori_loop(0, n1, freq, 0, unroll=16)

    def inv_a(j, carry):
        rhs = jnp.concatenate([b_ref[0, pl.ds(j, n1, stride=FFT_N2), :], b_ref[1, pl.ds(j, n1, stride=FFT_N2), :]], axis=0)
        b_ref[0, pl.ds(j, n1 // 2, stride=FFT_N2), :] = _dot(ti_ref[j], rhs.astype(BF16))
        return carry
    lax.fori_loop(0, FFT_N2, inv_a, 0, unroll=16)

    zc = zbuf[0:n_ctx, :]
    xc = _dot(fc_ref[...], zc.astype(BF16))
    nc = 2 * n_ctx
    xr, xi = xc[:nc], xc[nc:]
    kr, ki = kc_ref[0, 0], kc_ref[0, 1]
    yc = jnp.concatenate([xr * kr - xi * ki, xr * ki + xi * kr], axis=0).astype(BF16)
    conv_c = _dot(gc_ref[...], yc)

    gate = _short_conv(g_ref, wg_ref, bg_ref, n_ctx)
    skip = skip_ref[0]
    zbuf[0:n_ctx, :] = gate[0:n_ctx] * (conv_c + zc * skip)
    zbuf[n_ctx:lt, :] = gate[n_ctx:lt] * (b_ref[0, 0:n_lat, :] + zbuf[n_ctx:lt, :] * skip)

    @pl.when(order == pl.num_programs(2) - 1)
    def _():
        z = zbuf[...]
        sq = z * z
        sq_hi = sq.astype(BF16)
        sq_lo = (sq - sq_hi.astype(F32)).astype(BF16)
        gm = gm_ref[...].astype(BF16)
        ms = _dot(sq_hi, gm) + _dot(sq_lo, gm)
        y_ref[...] = (z * lax.rsqrt(ms + EPS) * nw_ref[...]).astype(y_ref.dtype)


def hyena_mixer(p16, conv_w, conv_b, kf_lat, kf_ctx, skip, norm_w, batch, n_ctx):
    import numpy as np
    lt = p16.shape[0] // batch
    n_lat = lt - n_ctx
    width = HYENA_WIDTH
    cb = width // HY_LANES
    n = 2 * n_lat
    n1 = n // FFT_N2
    ta, tinv, sb, sbi = _np_cplx_tables(n_lat)
    ta = ta[:, :, :n1 // 2]
    fc, gc = _np_dense_dft(n_ctx)
    fc = fc[:, :n_ctx]
    gsz = width // HYENA_GROUPS
    gidx = np.arange(HY_LANES) // gsz
    gmean = (gidx[:, None] == gidx[None, :]).astype(np.float32) / gsz
    z_blk = HYENA_ORDER * cb
    cw = conv_w.astype(F32)
    cbias = conv_b.astype(F32).reshape(1, -1)
    const2 = lambda b, c, o: (0, 0)
    const3 = lambda b, c, o: (0, 0, 0)
    return pl.pallas_call(
        functools.partial(_hyena_conv_kernel, n_ctx=n_ctx),
        grid=(batch, cb, HYENA_ORDER),
        in_specs=[pl.BlockSpec((lt, HY_LANES), lambda b, c, o: (b, z_blk + c)),
                  pl.BlockSpec((lt, HY_LANES), lambda b, c, o: (b, o * cb + c)),
                  pl.BlockSpec((HYENA_SHORT, HY_LANES), lambda b, c, o: (0, z_blk + c)),
                  pl.BlockSpec((1, HY_LANES), lambda b, c, o: (0, z_blk + c)),
                  pl.BlockSpec((HYENA_SHORT, HY_LANES), lambda b, c, o: (0, o * cb + c)),
                  pl.BlockSpec((1, HY_LANES), lambda b, c, o: (0, o * cb + c)),
                  pl.BlockSpec((1, 2, n, HY_LANES), lambda b, c, o: (o, 0, 0, c)),
                  pl.BlockSpec((1, 2, 2 * n_ctx, HY_LANES), lambda b, c, o: (o, 0, 0, c)),
                  pl.BlockSpec((FFT_N2, 2 * n1, n1 // 2), const3),
                  pl.BlockSpec((FFT_N2, n1 // 2, 2 * n1), const3),
                  pl.BlockSpec((2 * FFT_N2, 2 * FFT_N2), const2),
                  pl.BlockSpec((2 * FFT_N2, 2 * FFT_N2), const2),
                  pl.BlockSpec((4 * n_ctx, n_ctx), const2),
                  pl.BlockSpec((n_ctx, 4 * n_ctx), const2),
                  pl.BlockSpec((1, 1, HY_LANES), lambda b, c, o: (o, 0, c)),
                  pl.BlockSpec((1, HY_LANES), lambda b, c, o: (0, c)),
                  pl.BlockSpec((HY_LANES, HY_LANES), const2)],
        out_specs=pl.BlockSpec((lt, HY_LANES), lambda b, c, o: (b, c)),
        out_shape=jax.ShapeDtypeStruct((batch * lt, width), BF16),
        scratch_shapes=[pltpu.VMEM((lt, HY_LANES), F32), pltpu.VMEM((2, n, HY_LANES), F32)],
        compiler_params=pltpu.CompilerParams(dimension_semantics=("arbitrary", "arbitrary", "arbitrary"), vmem_limit_bytes=HY_VMEM),
        name="hyena_mixer",
    )(p16, p16, cw, cbias, cw, cbias, kf_lat, kf_ctx,
      jnp.asarray(ta, BF16), jnp.asarray(tinv, BF16), jnp.asarray(sb, BF16), jnp.asarray(sbi, BF16),
      jnp.asarray(fc, BF16), jnp.asarray(gc, BF16), skip.astype(F32).reshape(HYENA_ORDER, 1, width),
      norm_w.astype(F32).reshape(1, width), jnp.asarray(gmean, F32))


def hyena_spectrum_dense(kern):
    n_ord, n, width = kern.shape
    fwd, _ = _np_dense_dft(n // 2)
    spec = pallas_matmul(jnp.asarray(fwd, F32), kern.transpose(1, 0, 2).reshape(n, n_ord * width), tm=2 * n, tn=width)
    return spec.reshape(2, n, n_ord, width).transpose(2, 0, 1, 3)


def hyena_kernels(length, fw1, fb1, fw2, fb2, fw3, fb3, freq, wout):
    h = hyena_filters(length, fw1, fb1, fw2, fb2, fw3, fb3, freq, wout)
    zero = jnp.zeros((HYENA_ORDER, 1, HYENA_WIDTH), F32)
    return jnp.concatenate([h[:, 0], zero, jnp.flip(h[:, 1, 1:], axis=1)], axis=1)


def rope_tables(n_ctx, n_lat):
    half = RET_DQK // 2
    inv = ROPE_BASE ** (-jnp.arange(0, half, 2, dtype=F32) / half)
    t = jnp.arange(n_lat)
    row = (t // GRID_W).astype(F32)
    col = (t % GRID_W).astype(F32)
    a_row = row[:, None] * inv[None, :]
    a_col = col[:, None] * inv[None, :]
    cos = jnp.concatenate([jnp.cos(a_row), jnp.cos(a_row), jnp.cos(a_col), jnp.cos(a_col)], -1)
    sin = jnp.concatenate([-jnp.sin(a_row), jnp.sin(a_row), -jnp.sin(a_col), jnp.sin(a_col)], -1)
    cos = jnp.concatenate([jnp.ones((n_ctx, HEAD_W), F32), cos], 0)
    sin = jnp.concatenate([jnp.zeros((n_ctx, HEAD_W), F32), sin], 0)
    return cos, sin


def kernel(x, c, ctx, c_ctx, ada_w, ada_b, ln_w, ln_b, even_w_in, mlstm_gate_b, mlstm_norm_w, hgrn_lb, hgrn_norm_w, even_w_out, odd_w_in, hy_conv_w, hy_conv_b, hy_f_w1, hy_f_b1, hy_f_w2, hy_f_b2, hy_f_w3, hy_f_b3, hy_f_freq, hy_f_wout, hy_skip, hy_norm_w, ret_log_decay, ret_norm_w, odd_w_out, router_w, router_bias, exp_w1, exp_w3, exp_w2, sh_w1, sh_w3, sh_w2):
    batch, n_lat, d = x.shape
    n_ctx = ctx.shape[1]
    lt = n_ctx + n_lat
    t = batch * lt
    sm = jax.nn.softmax(hgrn_lb.astype(F32), axis=0)
    lower_bounds = jnp.cumsum(sm, axis=0) - sm[0]
    acts = jnp.concatenate([jax.nn.silu(c.astype(F32)), jax.nn.silu(c_ctx.astype(F32))[None]], 0)
    acts = jnp.pad(acts, ((0, 16 - (batch + 1) % 16), (0, 0)))
    cos_t, sin_t = rope_tables(n_ctx, n_lat)
    h = jnp.concatenate([ctx, x], axis=1).reshape(t, d).astype(F32)
    for l in range(DEPTH):
        mods = pallas_matmul(acts, ada_w[l].astype(F32), tm=acts.shape[0], tn=512)[:batch + 1] + ada_b[l]
        mod_lat = mods[:batch].reshape(batch, 6, d)
        mod_ctx = jnp.broadcast_to(mods[batch].reshape(1, 6, d), (batch, 6, d))
        modtab = jnp.stack([mod_ctx, mod_lat], axis=1)
        if l % 2 == 0:
            e = l // 2
            w = even_w_in[e].astype(F32)
            c0 = sum(MLSTM_SPLITS[:4])
            c1 = c0 + MLSTM_SPLITS[4]
            c2 = c1 + HGRN_SPLITS[0] + HGRN_SPLITS[1]
            c3 = c2 + HGRN_SPLITS[2] + HGRN_SPLITS[3]
            w16 = jnp.concatenate([w[:, :c0], w[:, c1:c2], w[:, c3:]], 1).astype(BF16)
            w32 = jnp.concatenate([w[:, c0:c1], jnp.zeros((d, HEAD_W - MLSTM_SPLITS[4]), F32), w[:, c2:c3]], 1).astype(BF16)
            p16 = proj_in(h, modtab, w16, BF16, batch, n_ctx, tn=w16.shape[1] // 2)
            p32 = proj_in(h, modtab, w32, F32, batch, n_ctx, tn=w32.shape[1])
            ya = mlstm_mixer(p16, p32, mlstm_gate_b[e], mlstm_norm_w[e], batch, n_ctx)
            yb = hgrn_mixer(p16, p32, lower_bounds[e], hgrn_norm_w[e], batch, n_ctx)
            w_out = even_w_out[e]
        else:
            o = l // 2
            hw = ODD_SPLITS[0]
            p16 = proj_in(h, modtab, odd_w_in[o].astype(BF16), BF16, batch, n_ctx, tn=ODD_PROJ // 2)
            filt_p = [p.astype(F32) for p in (hy_f_w1[o], hy_f_b1[o], hy_f_w2[o], hy_f_b2[o], hy_f_w3[o], hy_f_b3[o], hy_f_freq[o], hy_f_wout[o])]
            kf_lat = hyena_spectrum(hyena_kernels(n_lat, *filt_p))
            kf_ctx = hyena_spectrum_dense(hyena_kernels(n_ctx, *filt_p))
            ya = hyena_mixer(p16, hy_conv_w[o], hy_conv_b[o], kf_lat, kf_ctx, hy_skip[o], hy_norm_w[o], batch, n_ctx)
            yb = retention_mixer(p16, hw // HEAD_W, ret_log_decay[o], ret_norm_w[o], cos_t, sin_t, batch, n_ctx)
            w_out = odd_w_out[o]
        ka = ya.shape[1]
        h1, v = proj_out_ln(ya, yb, w_out[:ka].astype(BF16), w_out[ka:].astype(BF16), h, modtab, ln_w[l, 0], ln_b[l, 0], batch, n_ctx)
        h = moe_ffn(v, router_w[l], router_bias[l], exp_w1[l], exp_w3[l], exp_w2[l], sh_w1[l], sh_w3[l], sh_w2[l],
                    h1, modtab, ln_w[l, 1], ln_b[l, 1], batch, n_ctx)
    return h.reshape(batch, lt, d)[:, n_ctx:]
```
